```python
import jax, jax.numpy as jnp
from jax import lax
import numpy as np

D_MODEL = 2048
BATCH = 4
SEQ = 4096
DEPTH = 2

GRID_W = 64
CTX_LEN = 256
HEAD_DIM = 128
N_GROUPS = 4
GROUP_W = D_MODEL // N_GROUPS
MIX_W = N_GROUPS * GROUP_W
ML_HEADS = GROUP_W // HEAD_DIM
ML_CHUNK = 128
N_GATE = 2 * 2 * ML_HEADS
SG_CHUNK = 128
SG_GROUPS = 4
SG_CH = GROUP_W // SG_GROUPS
CONV_W = 31
ATT_Q_HEADS = GROUP_W // HEAD_DIM
ATT_KV_HEADS = ATT_Q_HEADS // 2
ATT_BLOCK = 128
ROPE_THETA = 10000.0
AXIS_DIM = HEAD_DIM // 2
D_FF = 256 * ((8 * D_MODEL // 3 + 255) // 256)
FFN_CONV_W = 3
EPS = 1e-6
IN_SPLITS = (GROUP_W, GROUP_W, GROUP_W, GROUP_W, N_GATE,
             GROUP_W, GROUP_W,
             GROUP_W, GROUP_W,
             ATT_Q_HEADS * HEAD_DIM, ATT_KV_HEADS * HEAD_DIM, ATT_KV_HEADS * HEAD_DIM)
D_IN = sum(IN_SPLITS)

kernel_name = 'hybrid_parallel_groups_diffusion_block'


def rmsnorm(x, g):
    xf = x.astype(jnp.float32)
    y = xf * lax.rsqrt(jnp.mean(xf * xf, axis=-1, keepdims=True) + EPS)
    return (y * g.astype(jnp.float32)).astype(x.dtype)


def layernorm(x, g, b):
    xf = x.astype(jnp.float32)
    mu = jnp.mean(xf, axis=-1, keepdims=True)
    var = jnp.mean(jnp.square(xf - mu), axis=-1, keepdims=True)
    y = (xf - mu) * lax.rsqrt(var + EPS)
    return (y * g.astype(jnp.float32) + b.astype(jnp.float32)).astype(x.dtype)


def modulate(x, shift, scale):
    return x * (1.0 + scale) + shift


def split_cols(z):
    points = np.cumsum(IN_SPLITS)[:-1].tolist()
    return jnp.split(z, points, axis=-1)


def dwconv(x, w, b):
    y = lax.conv_general_dilated(x, w[:, None, :], window_strides=(1,), padding='SAME',
                                 dimension_numbers=('NWC', 'WIO', 'NWC'),
                                 feature_group_count=x.shape[-1])
    return y + b


def rope_2d(rows):
    t = jnp.arange(rows * GRID_W)
    row = (t // GRID_W).astype(jnp.float32)
    col = (t % GRID_W).astype(jnp.float32)
    inv = jnp.power(ROPE_THETA, -jnp.arange(0, AXIS_DIM, 2, dtype=jnp.float32) / AXIS_DIM)
    ang = jnp.concatenate([row[:, None] * inv, col[:, None] * inv], axis=-1)
    return jnp.cos(ang), jnp.sin(ang)


def apply_rope(x, cos, sin):
    half = x.shape[-1] // 2
    x1, x2 = x[..., :half], x[..., half:]
    c = cos[None, :, None, :].astype(x.dtype)
    s = sin[None, :, None, :].astype(x.dtype)
    return jnp.concatenate([x1 * c - x2 * s, x2 * c + x1 * s], axis=-1)


def mlstm_chunk_scan(q, k, v, li, lf, state):
    B, H, T, d = q.shape
    n_chunks = T // ML_CHUNK

    def to_chunks(a):
        return jnp.moveaxis(a.reshape((B, H, n_chunks, ML_CHUNK) + a.shape[3:]), 2, 0)

    tri = jnp.tril(jnp.ones((ML_CHUNK, ML_CHUNK), dtype=bool))

    def step(carry, inp):
        C, nv, m = carry
        qc, kc, vc, ic, fc = inp
        b = jnp.cumsum(fc, axis=-1)
        dmat = jnp.where(tri, b[..., :, None] - b[..., None, :] + ic[..., None, :], -jnp.inf)
        inter = b + m[..., None]
        mt = jnp.maximum(inter, jnp.max(dmat, axis=-1))
        w_intra = jnp.exp(dmat - mt[..., None])
        w_inter = jnp.exp(inter - mt)
        s = jnp.einsum('bhtd,bhsd->bhts', qc, kc) * w_intra
        num = w_inter[..., None] * jnp.einsum('bhtd,bhde->bhte', qc, C) + jnp.einsum('bhts,bhse->bhte', s, vc)
        den = w_inter * jnp.einsum('bhtd,bhd->bht', qc, nv) + jnp.sum(s, axis=-1)
        h = num / jnp.maximum(jnp.abs(den), jnp.exp(-mt))[..., None]
        b_last = b[..., -1]
        g = b_last[..., None] - b + ic
        m_new = jnp.maximum(b_last + m, jnp.max(g, axis=-1))
        a = jnp.exp(b_last + m - m_new)
        w = jnp.exp(g - m_new[..., None])
        C_new = a[..., None, None] * C + jnp.einsum('bhs,bhsd,bhse->bhde', w, kc, vc)
        n_new = a[..., None] * nv + jnp.einsum('bhs,bhsd->bhd', w, kc)
        return (C_new, n_new, m_new), h

    state, hs = lax.scan(step, state, (to_chunks(q), to_chunks(k), to_chunks(v), to_chunks(li), to_chunks(lf)))
    return jnp.moveaxis(hs, 0, 2).reshape(B, H, T, d), state


def mlstm_out(h, o, norm_g):
    B, H, T, d = h.shape
    mu = jnp.mean(h, axis=-1, keepdims=True)
    var = jnp.mean(jnp.square(h - mu), axis=-1, keepdims=True)
    hn = ((h - mu) * lax.rsqrt(var + EPS)).transpose(0, 2, 1, 3).reshape(B, T, H * d)
    hn = hn * norm_g.astype(jnp.float32)
    return (hn * jax.nn.sigmoid(o.astype(jnp.float32))).astype(o.dtype)


def mlstm_mixer(lat, ctx, gate_b, norm_g, need_ctx):
    def prep(parts):
        q, k, v, o, g = parts
        B, T, _ = q.shape

        def hd(a):
            return a.reshape(B, T, ML_HEADS, HEAD_DIM).transpose(0, 2, 1, 3).astype(jnp.float32)
        gg = (g + gate_b).astype(jnp.float32).reshape(B, T, 2, 2, ML_HEADS).transpose(2, 3, 0, 4, 1)
        return hd(q), hd(k) * (HEAD_DIM ** -0.5), hd(v), o, gg

    ql, kl, vl, ol, gl = prep(lat)
    qc, kc, vc, oc, gc = prep(ctx)
    B = qc.shape[0]
    state0 = (jnp.zeros((B, ML_HEADS, HEAD_DIM, HEAD_DIM), jnp.float32),
              jnp.zeros((B, ML_HEADS, HEAD_DIM), jnp.float32),
              jnp.zeros((B, ML_HEADS), jnp.float32))
    h_l = jnp.zeros_like(ql)
    h_c = jnp.zeros_like(qc)
    for direction in range(2):
        if direction == 0:
            flip = lambda a: a
        else:
            flip = lambda a: jnp.flip(a, axis=2)
        hc, st = mlstm_chunk_scan(flip(qc), flip(kc), flip(vc), flip(gc[direction, 0]),
                                  flip(jax.nn.log_sigmoid(gc[direction, 1])), state0)
        hl, _ = mlstm_chunk_scan(flip(ql), flip(kl), flip(vl), flip(gl[direction, 0]),
                                 flip(jax.nn.log_sigmoid(gl[direction, 1])), st)
        h_l = h_l + flip(hl)
        if need_ctx:
            h_c = h_c + flip(hc)
    y_l = mlstm_out(h_l, ol, norm_g)
    y_c = mlstm_out(h_c, oc, norm_g) if need_ctx else None
    return y_l, y_c


def spatial_gating(zu, zv, ln_g, ln_b, w_s, b_s):
    u = jax.nn.gelu(zu)
    v = layernorm(jax.nn.gelu(zv), ln_g, ln_b)
    B, T, _ = u.shape
    vg = v.reshape(B, T // SG_CHUNK, SG_CHUNK, SG_GROUPS, SG_CH)
    mixed = jnp.einsum('gts,bnsgc->bntgc', w_s, vg) + b_s.T[None, None, :, :, None]
    return u * mixed.reshape(B, T, GROUP_W)


def conformer_conv(za, zg, w_dw, b_dw, ln_g, ln_b):
    y = za * jax.nn.sigmoid(zg)
    y = dwconv(y, w_dw, b_dw)
    return jax.nn.silu(layernorm(y, ln_g, ln_b))


def attend(q, k, v):
    B, Tq, Hq, d = q.shape
    Hkv = k.shape[2]
    G = Hq // Hkv
    nb = Tq // ATT_BLOCK
    qb = jnp.moveaxis(q.reshape(B, nb, ATT_BLOCK, Hkv, G, d), 1, 0)
    scale = d ** -0.5

    def one(qblk):
        s = jnp.einsum('bqhgd,bkhd->bhgqk', qblk, k).astype(jnp.float32) * scale
        p = jax.nn.softmax(s, axis=-1).astype(v.dtype)
        return jnp.einsum('bhgqk,bkhd->bqhgd', p, v)

    o = lax.map(one, qb)
    return jnp.moveaxis(o, 0, 1).reshape(B, Tq, Hq * d)


def conv_glu_ffn(h, w_up, w_dw, b_dw, w_down):
    g, u = jnp.split(h @ w_up, 2, axis=-1)
    return (jax.nn.silu(dwconv(g, w_dw, b_dw)) * u) @ w_down


def heads(a, n):
    return a.reshape(a.shape[0], a.shape[1], n, HEAD_DIM)


def hybrid_layer(x, xc, mod_l, mod_c, cos, sin, pre_mix_g, post_mix_g, w_in, ml_gate_b, ml_norm_g,
                 sg_ln_g, sg_ln_b, sg_w, sg_b, cv_w, cv_b, cv_ln_g, cv_ln_b, at_qn_g, at_kn_g, w_out,
                 pre_ffn_g, post_ffn_g, w_up, ffn_cv_w, ffn_cv_b, w_down, need_ctx):
    sh1, sc1, g1, sh2, sc2, g2 = jnp.split(mod_l, 6, axis=-1)
    sh1c, sc1c, g1c, sh2c, sc2c, g2c = jnp.split(mod_c, 6, axis=-1)

    zl = split_cols(modulate(rmsnorm(x, pre_mix_g), sh1, sc1) @ w_in)
    zc = split_cols(modulate(rmsnorm(xc, pre_mix_g), sh1c, sc1c) @ w_in)

    ya_l, ya_c = mlstm_mixer(zl[0:5], zc[0:5], ml_gate_b, ml_norm_g, need_ctx)
    yb_l = spatial_gating(zl[5], zl[6], sg_ln_g, sg_ln_b, sg_w, sg_b)
    yc_l = conformer_conv(zl[7], zl[8], cv_w, cv_b, cv_ln_g, cv_ln_b)

    q_l = apply_rope(rmsnorm(heads(zl[9], ATT_Q_HEADS), at_qn_g), cos, sin)
    k_l = apply_rope(rmsnorm(heads(zl[10], ATT_KV_HEADS), at_kn_g), cos, sin)
    v_l = heads(zl[11], ATT_KV_HEADS)
    k_c = rmsnorm(heads(zc[10], ATT_KV_HEADS), at_kn_g)
    v_c = heads(zc[11], ATT_KV_HEADS)
    yd_l = attend(q_l, jnp.concatenate([k_c, k_l], axis=1), jnp.concatenate([v_c, v_l], axis=1))

    y_l = jnp.concatenate([ya_l, yb_l, yc_l, yd_l], axis=-1) @ w_out
    x = x + g1 * rmsnorm(y_l, post_mix_g)

    h2 = modulate(rmsnorm(x, pre_ffn_g), sh2, sc2)
    x = x + g2 * rmsnorm(conv_glu_ffn(h2, w_up, ffn_cv_w, ffn_cv_b, w_down), post_ffn_g)

    if need_ctx:
        yb_c = spatial_gating(zc[5], zc[6], sg_ln_g, sg_ln_b, sg_w, sg_b)
        yc_c = conformer_conv(zc[7], zc[8], cv_w, cv_b, cv_ln_g, cv_ln_b)
        q_c = rmsnorm(heads(zc[9], ATT_Q_HEADS), at_qn_g)
        yd_c = attend(q_c, k_c, v_c)
        y_c = jnp.concatenate([ya_c, yb_c, yc_c, yd_c], axis=-1) @ w_out
        xc = xc + g1c * rmsnorm(y_c, post_mix_g)
        h2c = modulate(rmsnorm(xc, pre_ffn_g), sh2c, sc2c)
        xc = xc + g2c * rmsnorm(conv_glu_ffn(h2c, w_up, ffn_cv_w, ffn_cv_b, w_down), post_ffn_g)
    return x, xc


def setup_inputs(seed: int = 0) -> dict:
    key = jax.random.key(seed)
    ks = jax.random.split(key, 32)

    def nrm(k, shape, scale):
        return jax.random.normal(k, shape, jnp.float32) * scale

    def gain(k, n):
        return 1.0 + 0.05 * jax.random.normal(k, (DEPTH, n), jnp.float32)

    i_bias = 0.1 * jax.random.normal(ks[9], (DEPTH, 2, 1, ML_HEADS), jnp.float32)
    f_bias = 3.0 + 3.0 * jax.random.uniform(ks[10], (DEPTH, 2, 1, ML_HEADS), jnp.float32)
    ml_gate_b = jnp.concatenate([i_bias, f_bias], axis=2).reshape(DEPTH, N_GATE)
    return {
        'x': nrm(ks[0], (BATCH, SEQ, D_MODEL), 1.0),
        'c': nrm(ks[1], (BATCH, D_MODEL), 1.0),
        'ctx': nrm(ks[2], (BATCH, CTX_LEN, D_MODEL), 1.0),
        'c_ctx': nrm(ks[3], (D_MODEL,), 1.0),
        'w_ada': nrm(ks[4], (DEPTH, D_MODEL, 6 * D_MODEL), 0.5 * D_MODEL ** -0.5),
        'b_ada': nrm(ks[5], (DEPTH, 6 * D_MODEL), 0.02),
        'pre_mix_g': gain(ks[6], D_MODEL),
        'post_mix_g': gain(ks[7], D_MODEL),
        'w_in': nrm(ks[8], (DEPTH, D_MODEL, D_IN), D_MODEL ** -0.5),
        'ml_gate_b': ml_gate_b,
        'ml_norm_g': gain(ks[11], GROUP_W),
        'sg_ln_g': gain(ks[12], GROUP_W),
        'sg_ln_b': nrm(ks[13], (DEPTH, GROUP_W), 0.02),
        'sg_w': nrm(ks[14], (DEPTH, SG_GROUPS, SG_CHUNK, SG_CHUNK), SG_CHUNK ** -0.5),
        'sg_b': 1.0 + 0.1 * jax.random.normal(ks[15], (DEPTH, SG_GROUPS, SG_CHUNK), jnp.float32),
        'cv_w': nrm(ks[16], (DEPTH, CONV_W, GROUP_W), CONV_W ** -0.5),
        'cv_b': nrm(ks[17], (DEPTH, GROUP_W), 0.02),
        'cv_ln_g': gain(ks[18], GROUP_W),
        'cv_ln_b': nrm(ks[19], (DEPTH, GROUP_W), 0.02),
        'at_qn_g': gain(ks[20], HEAD_DIM),
        'at_kn_g': gain(ks[21], HEAD_DIM),
        'w_out': nrm(ks[22], (DEPTH, MIX_W, D_MODEL), MIX_W ** -0.5),
        'pre_ffn_g': gain(ks[23], D_MODEL),
        'post_ffn_g': gain(ks[24], D_MODEL),
        'w_up': nrm(ks[25], (DEPTH, D_MODEL, 2 * D_FF), D_MODEL ** -0.5),
        'ffn_cv_w': nrm(ks[26], (DEPTH, FFN_CONV_W, D_FF), FFN_CONV_W ** -0.5),
        'ffn_cv_b': nrm(ks[27], (DEPTH, D_FF), 0.02),
        'w_down': nrm(ks[28], (DEPTH, D_FF, D_MODEL), D_FF ** -0.5),
    }


def reference(x, c, ctx, c_ctx, w_ada, b_ada, pre_mix_g, post_mix_g, w_in, ml_gate_b, ml_norm_g,
              sg_ln_g, sg_ln_b, sg_w, sg_b, cv_w, cv_b, cv_ln_g, cv_ln_b, at_qn_g, at_kn_g, w_out,
              pre_ffn_g, post_ffn_g, w_up, ffn_cv_w, ffn_cv_b, w_down):
    ROWS = x.shape[1] // GRID_W
    cos, sin = rope_2d(ROWS)
    s_lat = jax.nn.silu(c)
    s_ctx = jax.nn.silu(c_ctx)
    xc = ctx
    for i in range(DEPTH):
        mod_l = (s_lat @ w_ada[i] + b_ada[i])[:, None, :]
        mod_c = (s_ctx @ w_ada[i] + b_ada[i])[None, None, :]
        x, xc = hybrid_layer(x, xc, mod_l, mod_c, cos, sin, pre_mix_g[i], post_mix_g[i], w_in[i],
                             ml_gate_b[i], ml_norm_g[i], sg_ln_g[i], sg_ln_b[i], sg_w[i], sg_b[i],
                             cv_w[i], cv_b[i], cv_ln_g[i], cv_ln_b[i], at_qn_g[i], at_kn_g[i], w_out[i],
                             pre_ffn_g[i], post_ffn_g[i], w_up[i], ffn_cv_w[i], ffn_cv_b[i], w_down[i],
                             i < DEPTH - 1)
    return x
```

```python
import functools
import math

import jax
import jax.numpy as jnp
from jax import lax
from jax.experimental import pallas as pl
from jax.experimental.pallas import tpu as pltpu

F32 = jnp.float32
BF16 = jnp.bfloat16

EPS = 1e-6
LANE = 128
HEAD_DIM = 128
GROUP_W = 512
ML_HEADS = 4
CHUNK = 128
SG_GROUPS = 4
CONV_W = 31
CONV_HALO = 16
ATT_Q_HEADS = 4
ATT_KV_HEADS = 2
GRID_W = 64
ROPE_THETA = 10000.0
FFN_HALO = 16
N_GATE = 16

COL_MQ, COL_MK, COL_MV, COL_MO = 0, 512, 1024, 1536
COL_SU, COL_SV = 2048, 2560
COL_CA, COL_CG = 3072, 3584
COL_AQ, COL_AK, COL_AV = 4096, 4608, 4864
COL_GATE = 5120
NZ = 5376

VMEM_LIMIT = 48 * 1024 * 1024


def _cparams(sem):
    return pltpu.CompilerParams(dimension_semantics=sem, vmem_limit_bytes=VMEM_LIMIT)


def _rms(x, g):
    ms = jnp.mean(x * x, axis=-1, keepdims=True)
    return x * lax.rsqrt(ms + EPS) * g


def _layernorm(x, g, b):
    mu = jnp.mean(x, axis=-1, keepdims=True)
    d = x - mu
    var = jnp.mean(d * d, axis=-1, keepdims=True)
    return d * lax.rsqrt(var + EPS) * g + b


def _sigmoid(x):
    return 1.0 / (1.0 + jnp.exp(-x))


def _gelu_tanh(x):
    c = math.sqrt(2.0 / math.pi)
    return x * (0.5 * (1.0 + jnp.tanh(c * (x + 0.044715 * (x * x * x)))))


def _log_sigmoid(x):
    return jnp.minimum(x, 0.0) - jnp.log1p(jnp.exp(-jnp.abs(x)))


def _ada_kernel(s_ref, w_ref, b_ref, o_ref):
    s = s_ref[...]
    s = s * _sigmoid(s)
    o_ref[...] = jnp.dot(s.astype(BF16), w_ref[...].astype(BF16), preferred_element_type=F32) + b_ref[...]


def _ada(s_rows, w_ada, b_ada, tn=1024):
    depth, d, n = w_ada.shape
    r = s_rows.shape[0]
    return pl.pallas_call(
        _ada_kernel,
        grid=(depth, n // tn),
        in_specs=[pl.BlockSpec((r, d), lambda l, j: (0, 0)),
                  pl.BlockSpec((None, d, tn), lambda l, j: (l, 0, j)),
                  pl.BlockSpec((None, 1, tn), lambda l, j: (l, 0, j))],
        out_specs=pl.BlockSpec((None, r, tn), lambda l, j: (l, 0, j)),
        out_shape=jax.ShapeDtypeStruct((depth, r, n), F32),
        compiler_params=_cparams(("parallel", "parallel")),
        name="ada_mod",
    )(s_rows, w_ada, b_ada.reshape(depth, 1, n))


def _in_proj_kernel(x_ref, sh_ref, sc_ref, g_ref, w_ref, *rest):
    o_ref, h_scr = rest[-2], rest[-1]

    @pl.when(pl.program_id(2) == 0)
    def _():
        h = _rms(x_ref[...], g_ref[...]) * (1.0 + sc_ref[...]) + sh_ref[...]
        h_scr[...] = h.astype(BF16)

    o_ref[...] = jnp.dot(h_scr[...], w_ref[...], preferred_element_type=F32)


def _in_proj(xs, shift, scale, g, w, tt, row_off, tm, zbuf=None, tn=768):
    b, t, d = xs.shape
    nz = w.shape[1]
    batched = shift.shape[0] > 1
    mod_map = (lambda bb, i, j: (bb, 0, 0)) if batched else (lambda bb, i, j: (0, 0, 0))
    blk_off = row_off // tm
    in_specs = [pl.BlockSpec((None, tm, d), lambda bb, i, j: (bb, i, 0)),
                pl.BlockSpec((None, 1, d), mod_map),
                pl.BlockSpec((None, 1, d), mod_map),
                pl.BlockSpec((1, d), lambda bb, i, j: (0, 0)),
                pl.BlockSpec((d, tn), lambda bb, i, j: (0, j))]
    args = [xs, shift, scale, g, w]
    aliases = {}
    if zbuf is not None:
        in_specs.append(pl.BlockSpec(memory_space=pl.ANY))
        args.append(zbuf)
        aliases = {5: 0}
    return pl.pallas_call(
        _in_proj_kernel,
        grid=(b, t // tm, nz // tn),
        in_specs=in_specs,
        out_specs=pl.BlockSpec((None, tm, tn), lambda bb, i, j: (bb, i + blk_off, j)),
        out_shape=jax.ShapeDtypeStruct((b, tt, nz), F32),
        scratch_shapes=[pltpu.VMEM((tm, d), BF16)],
        input_output_aliases=aliases,
        compiler_params=_cparams(("parallel", "parallel", "arbitrary")),
        name="in_proj",
    )(*args)


def _mlstm_kernel(q_ref, k_ref, v_ref, o_ref, gt_ref, gb_ref, ng_ref, out_ref, hf_scr, hb_scr, *, n_lat, n_ctx):
    L = CHUNK
    nc = n_lat + n_ctx
    head = pl.program_id(1)
    row = lax.broadcasted_iota(jnp.int32, (L, L), 0)
    col = lax.broadcasted_iota(jnp.int32, (L, L), 1)
    lower = col <= row
    upper = col >= row
    lower_f = lower.astype(F32)
    upper_f = upper.astype(F32)
    k_scale = HEAD_DIM ** -0.5

    def chunk(c, direction, state):
        C, n, m = state
        sl = pl.ds(pl.multiple_of(c * L, L), L)
        q = q_ref[sl, :].astype(BF16)
        kf = k_ref[sl, :] * k_scale
        v = v_ref[sl, :].astype(BF16)
        G = gt_ref[sl, :] + gb_ref[...]
        GT = G.T
        Gf = _log_sigmoid(G)
        GfT = _log_sigmoid(GT)
        ci = direction * 8 + head
        cf = direction * 8 + 4 + head
        tri_c = lower_f if direction == 0 else upper_f
        tri_r = upper_f if direction == 0 else lower_f
        Bc = jnp.dot(tri_c, Gf, precision=lax.Precision.HIGHEST, preferred_element_type=F32)
        Br = jnp.dot(GfT, tri_r, precision=lax.Precision.HIGHEST, preferred_element_type=F32)
        b_col = jnp.sum(jnp.where(col == cf, Bc, 0.0), axis=1, keepdims=True)
        f_col = jnp.sum(jnp.where(col == cf, Gf, 0.0), axis=1, keepdims=True)
        i_col = jnp.sum(jnp.where(col == ci, G, 0.0), axis=1, keepdims=True)
        b_row = jnp.sum(jnp.where(row == cf, Br, 0.0), axis=0, keepdims=True)
        i_row = jnp.sum(jnp.where(row == ci, GT, 0.0), axis=0, keepdims=True)
        total = jnp.sum(f_col, axis=0, keepdims=True)

        mask = lower if direction == 0 else upper
        dmat = jnp.where(mask, b_col - b_row + i_row, -jnp.inf)
        inter = b_col + m
        mt = jnp.maximum(inter, jnp.max(dmat, axis=1, keepdims=True))
        w_intra = jnp.exp(dmat - mt)
        w_inter = jnp.exp(inter - mt)
        kb = kf.astype(BF16)
        s = lax.dot_general(q, kb, (((1,), (1,)), ((), ())), preferred_element_type=F32) * w_intra
        qc = jnp.dot(q, C.astype(BF16), preferred_element_type=F32)
        num = w_inter * qc + jnp.dot(s.astype(BF16), v, preferred_element_type=F32)
        qn = jnp.sum(q.astype(F32) * n, axis=1, keepdims=True)
        den = w_inter * qn + jnp.sum(s, axis=1, keepdims=True)
        h = num / jnp.maximum(jnp.abs(den), jnp.exp(-mt))

        g_col = total - b_col + i_col
        m_new = jnp.maximum(total + m, jnp.max(g_col, axis=0, keepdims=True))
        a = jnp.exp(total + m - m_new)
        kw = kf * jnp.exp(g_col - m_new)
        C_new = a * C + jnp.dot(kw.T.astype(BF16), v, preferred_element_type=F32)
        n_new = a * n + jnp.sum(kw, axis=0, keepdims=True)
        return h, (C_new, n_new, m_new)

    def step(s, carry):
        st_f, st_b = carry
        c_f = jnp.where(s < n_ctx, n_lat + s, s - n_ctx)
        c_b = nc - 1 - s
        h_f, st_f = chunk(c_f, 0, st_f)
        h_b, st_b = chunk(c_b, 1, st_b)
        hf_scr[pl.ds(pl.multiple_of(c_f * L, L), L), :] = h_f
        hb_scr[pl.ds(pl.multiple_of(c_b * L, L), L), :] = h_b
        return st_f, st_b

    zero = (jnp.zeros((HEAD_DIM, HEAD_DIM), F32), jnp.zeros((1, HEAD_DIM), F32), jnp.zeros((1, 1), F32))
    lax.fori_loop(0, nc, step, (zero, zero))

    def finish(c, carry):
        sl = pl.ds(pl.multiple_of(c * L, L), L)
        h = hf_scr[sl, :] + hb_scr[sl, :]
        hn = _layernorm(h, ng_ref[...], 0.0)
        out_ref[sl, :] = (hn * _sigmoid(o_ref[sl, :])).astype(BF16)
        return carry

    lax.fori_loop(0, nc, finish, 0)


def _mlstm(z, gate_b, norm_g, t_lat, t_ctx):
    b, tt, _ = z.shape
    blk = lambda off: pl.BlockSpec((None, tt, LANE), lambda bb, h: (bb, 0, off // LANE + h))
    return pl.pallas_call(
        functools.partial(_mlstm_kernel, n_lat=t_lat // CHUNK, n_ctx=t_ctx // CHUNK),
        grid=(b, ML_HEADS),
        in_specs=[blk(COL_MQ), blk(COL_MK), blk(COL_MV), blk(COL_MO),
                  pl.BlockSpec((None, tt, LANE), lambda bb, h: (bb, 0, COL_GATE // LANE)),
                  pl.BlockSpec((1, LANE), lambda bb, h: (0, 0)),
                  pl.BlockSpec((1, LANE), lambda bb, h: (0, h))],
        out_specs=pl.BlockSpec((None, tt, LANE), lambda bb, h: (bb, 0, h)),
        out_shape=jax.ShapeDtypeStruct((b, tt, GROUP_W), BF16),
        scratch_shapes=[pltpu.VMEM((tt, LANE), F32), pltpu.VMEM((tt, LANE), F32)],
        compiler_params=_cparams(("parallel", "parallel")),
        name="mlstm",
    )(z, z, z, z, z, gate_b, norm_g)


def _sg_kernel(u_ref, v_ref, lng_ref, lnb_ref, w_ref, bs_ref, o_ref, *, n_chunks):
    def body(c, carry):
        sl = pl.ds(pl.multiple_of(c * CHUNK, CHUNK), CHUNK)
        u = _gelu_tanh(u_ref[sl, :])
        v = _layernorm(_gelu_tanh(v_ref[sl, :]), lng_ref[...], lnb_ref[...]).astype(BF16)
        for g in range(SG_GROUPS):
            gs = slice(g * LANE, (g + 1) * LANE)
            mixed = jnp.dot(w_ref[g], v[:, gs], preferred_element_type=F32) + bs_ref[g]
            o_ref[sl, gs] = (u[:, gs] * mixed).astype(BF16)
        return carry

    lax.fori_loop(0, n_chunks, body, 0)


def _spatial_gating(z, ln_g, ln_b, w_s, b_s_full, n_rows, tr):
    b = z.shape[0]
    blk = lambda off: pl.BlockSpec((None, tr, GROUP_W), lambda bb, i: (bb, i, off // GROUP_W))
    const2 = lambda bb, i: (0, 0)
    const3 = lambda bb, i: (0, 0, 0)
    return pl.pallas_call(
        functools.partial(_sg_kernel, n_chunks=tr // CHUNK),
        grid=(b, n_rows // tr),
        in_specs=[blk(COL_SU), blk(COL_SV),
                  pl.BlockSpec((1, GROUP_W), const2), pl.BlockSpec((1, GROUP_W), const2),
                  pl.BlockSpec((SG_GROUPS, CHUNK, CHUNK), const3),
                  pl.BlockSpec((SG_GROUPS, CHUNK, LANE), const3)],
        out_specs=pl.BlockSpec((None, tr, GROUP_W), lambda bb, i: (bb, i, 0)),
        out_shape=jax.ShapeDtypeStruct((b, z.shape[1], GROUP_W), BF16),
        compiler_params=_cparams(("parallel", "parallel")),
        name="spatial_gating",
    )(z, z, ln_g, ln_b, w_s, b_s_full)


def _conf_kernel(a_ref, g_ref, ap_ref, gp_ref, an_ref, gn_ref, w_ref, b_ref, lng_ref, lnb_ref, o_ref,
                 y_scr, cv_scr, *, tc, seq_starts, seq_ends):
    i = pl.program_id(1)
    has_prev = functools.reduce(jnp.logical_and, [i != s for s in seq_starts])
    has_next = functools.reduce(jnp.logical_and, [i != e for e in seq_ends])
    H = CONV_HALO
    y_scr[H:H + tc, :] = a_ref[...] * _sigmoid(g_ref[...])
    y_scr[0:H, :] = jnp.where(has_prev, ap_ref[...] * _sigmoid(gp_ref[...]), 0.0)
    y_scr[H + tc:, :] = jnp.where(has_next, an_ref[...] * _sigmoid(gn_ref[...]), 0.0)
    rb = 64
    base = H - (CONV_W // 2)
    for cb in range(GROUP_W // LANE):
        cs = slice(cb * LANE, (cb + 1) * LANE)
        for r in range(tc // rb):
            acc = jnp.zeros((rb, LANE), F32)
            for k in range(CONV_W):
                acc = acc + y_scr[r * rb + base + k:r * rb + base + k + rb, cs] * w_ref[k:k + 1, cs]
            cv_scr[r * rb:(r + 1) * rb, cs] = acc
    y = _layernorm(cv_scr[...] + b_ref[...], lng_ref[...], lnb_ref[...])
    o_ref[...] = (y * _sigmoid(y)).astype(BF16)


def _conformer(z, w_dw, b_dw, ln_g, ln_b, t_lat, n_rows, tc):
    b = z.shape[0]
    hb = tc // CONV_HALO
    n_tiles = n_rows // tc
    n_halo = z.shape[1] // CONV_HALO
    ca, cg = COL_CA // GROUP_W, COL_CG // GROUP_W
    cur = lambda cblk: pl.BlockSpec((None, tc, GROUP_W), lambda bb, i: (bb, i, cblk))
    prev = lambda cblk: pl.BlockSpec((None, CONV_HALO, GROUP_W),
                                     lambda bb, i: (bb, jnp.maximum(i * hb - 1, 0), cblk))
    nxt = lambda cblk: pl.BlockSpec((None, CONV_HALO, GROUP_W),
                                    lambda bb, i: (bb, jnp.minimum((i + 1) * hb, n_halo - 1), cblk))
    const2 = lambda bb, i: (0, 0)
    seq_starts = (0, t_lat // tc)
    seq_ends = (t_lat // tc - 1, z.shape[1] // tc - 1)
    return pl.pallas_call(
        functools.partial(_conf_kernel, tc=tc, seq_starts=seq_starts, seq_ends=seq_ends),
        grid=(b, n_tiles),
        in_specs=[cur(ca), cur(cg), prev(ca), prev(cg), nxt(ca), nxt(cg),
                  pl.BlockSpec((CONV_W, GROUP_W), const2), pl.BlockSpec((1, GROUP_W), const2),
                  pl.BlockSpec((1, GROUP_W), const2), pl.BlockSpec((1, GROUP_W), const2)],
        out_specs=pl.BlockSpec((None, tc, GROUP_W), lambda bb, i: (bb, i, 0)),
        out_shape=jax.ShapeDtypeStruct((b, z.shape[1], GROUP_W), BF16),
        scratch_shapes=[pltpu.VMEM((tc + 2 * CONV_HALO, GROUP_W), F32), pltpu.VMEM((tc, GROUP_W), F32)],
        compiler_params=_cparams(("parallel", "parallel")),
        name="conformer_conv",
    )(z, z, z, z, z, z, w_dw, b_dw, ln_g, ln_b)


def _attn_prep_kernel(q_ref, k_ref, v_ref, cos_ref, sin_ref, qg_ref, kg_ref, qo_ref, ko_ref, vo_ref):
    cosf = cos_ref[...]
    sinf = sin_ref[...]

    def norm_rope(x, g, scale):
        y = _rms(x, g)
        return (y * cosf + pltpu.roll(y, HEAD_DIM // 2, 1) * sinf) * scale

    for h in range(ATT_Q_HEADS):
        hs = slice(h * HEAD_DIM, (h + 1) * HEAD_DIM)
        qo_ref[:, hs] = norm_rope(q_ref[:, hs], qg_ref[...], HEAD_DIM ** -0.5).astype(BF16)
    for h in range(ATT_KV_HEADS):
        hs = slice(h * HEAD_DIM, (h + 1) * HEAD_DIM)
        ko_ref[:, hs] = norm_rope(k_ref[:, hs], kg_ref[...], 1.0).astype(BF16)
    vo_ref[...] = v_ref[...].astype(BF16)


def _attn_prep(z, cosf, sinf, qg, kg, tr=256):
    b, tt, _ = z.shape
    qw, kw = ATT_Q_HEADS * HEAD_DIM, ATT_KV_HEADS * HEAD_DIM
    const2 = lambda bb, i: (0, 0)
    return pl.pallas_call(
        _attn_prep_kernel,
        grid=(b, tt // tr),
        in_specs=[pl.BlockSpec((None, tr, qw), lambda bb, i: (bb, i, COL_AQ // qw)),
                  pl.BlockSpec((None, tr, kw), lambda bb, i: (bb, i, COL_AK // kw)),
                  pl.BlockSpec((None, tr, kw), lambda bb, i: (bb, i, COL_AV // kw)),
                  pl.BlockSpec((tr, HEAD_DIM), lambda bb, i: (i, 0)),
                  pl.BlockSpec((tr, HEAD_DIM), lambda bb, i: (i, 0)),
                  pl.BlockSpec((1, HEAD_DIM), const2), pl.BlockSpec((1, HEAD_DIM), const2)],
        out_specs=[pl.BlockSpec((None, tr, qw), lambda bb, i: (bb, i, 0)),
                   pl.BlockSpec((None, tr, kw), lambda bb, i: (bb, i, 0)),
                   pl.BlockSpec((None, tr, kw), lambda bb, i: (bb, i, 0))],
        out_shape=[jax.ShapeDtypeStruct((b, tt, qw), BF16),
                   jax.ShapeDtypeStruct((b, tt, kw), BF16),
                   jax.ShapeDtypeStruct((b, tt, kw), BF16)],
        compiler_params=_cparams(("parallel", "parallel")),
        name="attn_prep",
    )(z, z, z, cosf, sinf, qg, kg)


def _attn_kernel(q_ref, k_ref, v_ref, o_ref, *, t_lat, n_lat_tiles):
    i = pl.program_id(2)
    group = ATT_Q_HEADS // ATT_KV_HEADS

    def run(k, v):
        for g in range(group):
            hs = slice(g * HEAD_DIM, (g + 1) * HEAD_DIM)
            s = lax.dot_general(q_ref[:, hs], k, (((1,), (1,)), ((), ())), preferred_element_type=F32)
            p = jnp.exp(s - jnp.max(s, axis=-1, keepdims=True))
            l = jnp.sum(p, axis=-1, keepdims=True)
            o = jnp.dot(p.astype(BF16), v, preferred_element_type=F32)
            o_ref[:, hs] = (o / l).astype(BF16)

    @pl.when(i < n_lat_tiles)
    def _():
        run(k_ref[...], v_ref[...])

    @pl.when(i >= n_lat_tiles)
    def _():
        run(k_ref[t_lat:, :], v_ref[t_lat:, :])


def _attention(qr, kr, vr, t_lat, n_rows, tq=256):
    b, tt, _ = qr.shape
    gw = (ATT_Q_HEADS // ATT_KV_HEADS) * HEAD_DIM
    return pl.pallas_call(
        functools.partial(_attn_kernel, t_lat=t_lat, n_lat_tiles=t_lat // tq),
        grid=(b, ATT_KV_HEADS, n_rows // tq),
        in_specs=[pl.BlockSpec((None, tq, gw), lambda bb, h, i: (bb, i, h)),
                  pl.BlockSpec((None, tt, HEAD_DIM), lambda bb, h, i: (bb, 0, h)),
                  pl.BlockSpec((None, tt, HEAD_DIM), lambda bb, h, i: (bb, 0, h))],
        out_specs=pl.BlockSpec((None, tq, gw), lambda bb, h, i: (bb, i, h)),
        out_shape=jax.ShapeDtypeStruct((b, tt, ATT_Q_HEADS * HEAD_DIM), BF16),
        compiler_params=_cparams(("parallel", "parallel", "arbitrary")),
        name="attention",
    )(qr, kr, vr)


def _out_proj_kernel(ya_ref, yb_ref, yc_ref, yd_ref, w_ref, x_ref, gate_ref, pg_ref, o_ref):
    acc = jnp.dot(ya_ref[...], w_ref[0 * GROUP_W:1 * GROUP_W, :], preferred_element_type=F32)
    acc += jnp.dot(yb_ref[...], w_ref[1 * GROUP_W:2 * GROUP_W, :], preferred_element_type=F32)
    acc += jnp.dot(yc_ref[...], w_ref[2 * GROUP_W:3 * GROUP_W, :], preferred_element_type=F32)
    acc += jnp.dot(yd_ref[...], w_ref[3 * GROUP_W:4 * GROUP_W, :], preferred_element_type=F32)
    o_ref[...] = x_ref[...] + gate_ref[...] * _rms(acc, pg_ref[...])


def _out_proj(ys, w_out, xs, gate, post_g, row_off, tm):
    b, t, d = xs.shape
    batched = gate.shape[0] > 1
    mod_map = (lambda bb, i: (bb, 0, 0)) if batched else (lambda bb, i: (0, 0, 0))
    blk_off = row_off // tm
    yspec = pl.BlockSpec((None, tm, GROUP_W), lambda bb, i: (bb, i + blk_off, 0))
    return pl.pallas_call(
        _out_proj_kernel,
        grid=(b, t // tm),
        in_specs=[yspec, yspec, yspec, yspec,
                  pl.BlockSpec(w_out.shape, lambda bb, i: (0, 0)),
                  pl.BlockSpec((None, tm, d), lambda bb, i: (bb, i, 0)),
                  pl.BlockSpec((None, 1, d), mod_map),
                  pl.BlockSpec((1, d), lambda bb, i: (0, 0))],
        out_specs=pl.BlockSpec((None, tm, d), lambda bb, i: (bb, i, 0)),
        out_shape=jax.ShapeDtypeStruct((b, t, d), F32),
        compiler_params=_cparams(("parallel", "parallel")),
        name="out_proj",
    )(*ys, w_out, xs, gate, post_g)


def _ffn_kernel(x_ref, xp_ref, xn_ref, sh_ref, sc_ref, gate_ref, pre_g_ref, post_g_ref, wg_ref, wu_ref,
                cw_ref, cb_ref, wd_ref, o_ref, h_scr, acc_scr, *, tm, n_tiles, n_ff):
    i = pl.program_id(1)
    j = pl.program_id(2)
    H = FFN_HALO

    @pl.when(j == 0)
    def _():
        def pre(x):
            return _rms(x, pre_g_ref[...]) * (1.0 + sc_ref[...]) + sh_ref[...]
        h_scr[H:H + tm, :] = pre(x_ref[...]).astype(BF16)
        h_scr[0:H, :] = jnp.where(i > 0, pre(xp_ref[...]), 0.0).astype(BF16)
        h_scr[H + tm:, :] = jnp.where(i < n_tiles - 1, pre(xn_ref[...]), 0.0).astype(BF16)
        acc_scr[...] = jnp.zeros_like(acc_scr)

    n_ext = tm + 2 * H
    g_ext = jnp.dot(h_scr[...], wg_ref[...], preferred_element_type=F32)
    u = jnp.dot(h_scr[H:H + tm, :], wu_ref[...], preferred_element_type=F32)
    g_prev = pltpu.roll(g_ext, 1, 0)[H:H + tm]
    g_next = pltpu.roll(g_ext, n_ext - 1, 0)[H:H + tm]
    conv = g_prev * cw_ref[0:1, :] + g_ext[H:H + tm] * cw_ref[1:2, :] + g_next * cw_ref[2:3, :] + cb_ref[...]
    act = (conv * _sigmoid(conv)) * u
    acc_scr[...] += jnp.dot(act.astype(BF16), wd_ref[...], preferred_element_type=F32)

    @pl.when(j == n_ff - 1)
    def _():
        o_ref[...] = x_ref[...] + gate_ref[...] * _rms(acc_scr[...], post_g_ref[...])


def _ffn(xs, shift, scale, gate, pre_g, post_g, w_up, cv_w, cv_b, w_down, tm, tf=512):
    b, t, d = xs.shape
    d_ff = w_down.shape[0]
    n_ff = d_ff // tf
    n_tiles = t // tm
    hb = tm // FFN_HALO
    n_halo = t // FFN_HALO
    batched = shift.shape[0] > 1
    mod_map = (lambda bb, i, j: (bb, 0, 0)) if batched else (lambda bb, i, j: (0, 0, 0))
    const2 = lambda bb, i, j: (0, 0)
    return pl.pallas_call(
        functools.partial(_ffn_kernel, tm=tm, n_tiles=n_tiles, n_ff=n_ff),
        grid=(b, n_tiles, n_ff),
        in_specs=[pl.BlockSpec((None, tm, d), lambda bb, i, j: (bb, i, 0)),
                  pl.BlockSpec((None, FFN_HALO, d), lambda bb, i, j: (bb, jnp.maximum(i * hb - 1, 0), 0)),
                  pl.BlockSpec((None, FFN_HALO, d), lambda bb, i, j: (bb, jnp.minimum((i + 1) * hb, n_halo - 1), 0)),
                  pl.BlockSpec((None, 1, d), mod_map), pl.BlockSpec((None, 1, d), mod_map),
                  pl.BlockSpec((None, 1, d), mod_map),
                  pl.BlockSpec((1, d), const2), pl.BlockSpec((1, d), const2),
                  pl.BlockSpec((d, tf), lambda bb, i, j: (0, j)),
                  pl.BlockSpec((d, tf), lambda bb, i, j: (0, j + n_ff)),
                  pl.BlockSpec((cv_w.shape[0], tf), lambda bb, i, j: (0, j)),
                  pl.BlockSpec((1, tf), lambda bb, i, j: (0, j)),
                  pl.BlockSpec((tf, d), lambda bb, i, j: (j, 0))],
        out_specs=pl.BlockSpec((None, tm, d), lambda bb, i, j: (bb, i, 0)),
        out_shape=jax.ShapeDtypeStruct((b, t, d), F32),
        scratch_shapes=[pltpu.VMEM((tm + 2 * FFN_HALO, d), BF16), pltpu.VMEM((tm, d), F32)],
        compiler_params=_cparams(("parallel", "parallel", "arbitrary")),
        name="conv_glu_ffn",
    )(xs, xs, xs, shift, scale, gate, pre_g, post_g, w_up, w_up, cv_w, cv_b, w_down)


def _pack_w_in(w_in):
    d = w_in.shape[0]
    main = jnp.concatenate([w_in[:, :4 * GROUP_W], w_in[:, 4 * GROUP_W + N_GATE:]], axis=1)
    gates = w_in[:, 4 * GROUP_W:4 * GROUP_W + N_GATE]
    pad = jnp.zeros((d, NZ - main.shape[1] - N_GATE), w_in.dtype)
    return jnp.concatenate([main, gates, pad], axis=1).astype(BF16)


def _rope_tables(t_lat, t_ctx):
    t = jnp.arange(t_lat)
    row = (t // GRID_W).astype(F32)
    colp = (t % GRID_W).astype(F32)
    axis_dim = HEAD_DIM // 2
    inv = jnp.power(ROPE_THETA, -jnp.arange(0, axis_dim, 2, dtype=F32) / axis_dim)
    ang = jnp.concatenate([row[:, None] * inv, colp[:, None] * inv], axis=-1)
    cos, sin = jnp.cos(ang), jnp.sin(ang)
    cosf = jnp.concatenate([cos, cos], axis=-1)
    sinf = jnp.concatenate([-sin, sin], axis=-1)
    cosf = jnp.concatenate([cosf, jnp.ones((t_ctx, HEAD_DIM), F32)], axis=0)
    sinf = jnp.concatenate([sinf, jnp.zeros((t_ctx, HEAD_DIM), F32)], axis=0)
    return cosf, sinf


def kernel(x, c, ctx, c_ctx, w_ada, b_ada, pre_mix_g, post_mix_g, w_in, ml_gate_b, ml_norm_g, sg_ln_g, sg_ln_b,
           sg_w, sg_b, cv_w, cv_b, cv_ln_g, cv_ln_b, at_qn_g, at_kn_g, w_out, pre_ffn_g, post_ffn_g, w_up,
           ffn_cv_w, ffn_cv_b, w_down):
    b, t_lat, d = x.shape
    t_ctx = ctx.shape[1]
    tt = t_lat + t_ctx
    depth = w_ada.shape[0]
    tm_lat = min(512, t_lat)
    assert t_lat % tm_lat == 0 and t_lat % t_ctx == 0 and t_ctx % CHUNK == 0 and t_ctx % 256 == 0

    n_rows = -(-(b + 1) // 8) * 8
    s_rows = jnp.concatenate([c, c_ctx[None, :], jnp.zeros((n_rows - b - 1, d), F32)], axis=0)
    mod = _ada(s_rows, w_ada, b_ada)
    cosf, sinf = _rope_tables(t_lat, t_ctx)
    row2 = lambda a: a.reshape(1, -1)

    xc = ctx
    for i in range(depth):
        need_ctx = i < depth - 1
        mod_l = [m[:, None, :] for m in jnp.split(mod[i, :b], 6, axis=-1)]
        mod_c = [m[:, None, :] for m in jnp.split(mod[i, b:b + 1], 6, axis=-1)]
        sh1, sc1, g1, sh2, sc2, g2 = mod_l
        sh1c, sc1c, g1c, sh2c, sc2c, g2c = mod_c

        w_in_p = _pack_w_in(w_in[i])
        z = _in_proj(x, sh1, sc1, row2(pre_mix_g[i]), w_in_p, tt, 0, tm_lat)
        z = _in_proj(xc, sh1c, sc1c, row2(pre_mix_g[i]), w_in_p, tt, t_lat, t_ctx, zbuf=z)

        gate_b = jnp.concatenate([ml_gate_b[i], jnp.zeros((LANE - N_GATE,), F32)]).reshape(1, LANE)
        ya = _mlstm(z, gate_b, row2(ml_norm_g[i]), t_lat, t_ctx)

        mix_rows = tt if need_ctx else t_lat
        sg_tr = 2176 if (need_ctx and tt % 2176 == 0) else (2048 if t_lat % 2048 == 0 else 256)
        sg_tr = sg_tr if mix_rows % sg_tr == 0 else 256
        b_s_full = jnp.broadcast_to(sg_b[i][:, :, None], (SG_GROUPS, CHUNK, LANE))
        yb = _spatial_gating(z, row2(sg_ln_g[i]), row2(sg_ln_b[i]), sg_w[i].astype(BF16), b_s_full, mix_rows, sg_tr)

        yc = _conformer(z, cv_w[i], row2(cv_b[i]), row2(cv_ln_g[i]), row2(cv_ln_b[i]), t_lat, mix_rows, 256)

        qr, kr, vr = _attn_prep(z, cosf, sinf, row2(at_qn_g[i]), row2(at_kn_g[i]))
        yd = _attention(qr, kr, vr, t_lat, mix_rows)

        w_out_b = w_out[i].astype(BF16)
        w_up_b = w_up[i].astype(BF16)
        w_down_b = w_down[i].astype(BF16)
        ys = (ya, yb, yc, yd)
        x = _out_proj(ys, w_out_b, x, g1, row2(post_mix_g[i]), 0, tm_lat)
        x = _ffn(x, sh2, sc2, g2, row2(pre_ffn_g[i]), row2(post_ffn_g[i]), w_up_b, ffn_cv_w[i],
                 row2(ffn_cv_b[i]), w_down_b, tm_lat)
        if need_ctx:
            xc = _out_proj(ys, w_out_b, xc, g1c, row2(post_mix_g[i]), t_lat, t_ctx)
            xc = _ffn(xc, sh2c, sc2c, g2c, row2(pre_ffn_g[i]), row2(post_ffn_g[i]), w_up_b, ffn_cv_w[i],
                      row2(ffn_cv_b[i]), w_down_b, t_ctx)
    return x
```

```python
import functools
import math

import jax
import jax.numpy as jnp
from jax import lax
from jax.experimental import pallas as pl
from jax.experimental.pallas import tpu as pltpu

F32 = jnp.float32
BF16 = jnp.bfloat16

EPS = 1e-6
LANE = 128
HEAD_DIM = 128
GROUP_W = 512
ML_HEADS = 4
CHUNK = 128
SG_GROUPS = 4
CONV_W = 31
CONV_HALO = 16
ATT_Q_HEADS = 4
ATT_KV_HEADS = 2
GRID_W = 64
ROPE_THETA = 10000.0
FFN_HALO = 16
N_GATE = 16
GATE_ROWS = 32

COL_MQ, COL_MK, COL_MV, COL_MO = 0, 512, 1024, 1536
COL_SU, COL_SV = 2048, 2560
COL_CA, COL_CG = 3072, 3584
COL_AQ, COL_AK, COL_AV = 4096, 4608, 4864
NZ = 5120

VMEM_LIMIT = 52 * 1024 * 1024


def _cparams(sem):
    return pltpu.CompilerParams(dimension_semantics=sem, vmem_limit_bytes=VMEM_LIMIT)


def _rms(x, g):
    ms = jnp.mean(x * x, axis=-1, keepdims=True)
    return x * lax.rsqrt(ms + EPS) * g


def _layernorm(x, g, b):
    mu = jnp.mean(x, axis=-1, keepdims=True)
    d = x - mu
    var = jnp.mean(d * d, axis=-1, keepdims=True)
    return d * lax.rsqrt(var + EPS) * g + b


def _sigmoid(x):
    return 1.0 / (1.0 + jnp.exp(-x))


def _gelu_tanh(x):
    c = math.sqrt(2.0 / math.pi)
    return x * (0.5 * (1.0 + jnp.tanh(c * (x + 0.044715 * (x * x * x)))))


def _log_sigmoid(x):
    return jnp.minimum(x, 0.0) - jnp.log1p(jnp.exp(-jnp.abs(x)))


def _ada_kernel(s_ref, w_ref, b_ref, o_ref):
    s = s_ref[...]
    s = s * _sigmoid(s)
    o_ref[...] = jnp.dot(s.astype(BF16), w_ref[...].astype(BF16), preferred_element_type=F32) + b_ref[...]


def _ada(s_rows, w_ada, b_ada, tn=1024):
    depth, d, n = w_ada.shape
    r = s_rows.shape[0]
    return pl.pallas_call(
        _ada_kernel,
        grid=(depth, n // tn),
        in_specs=[pl.BlockSpec((r, d), lambda l, j: (0, 0)),
                  pl.BlockSpec((None, d, tn), lambda l, j: (l, 0, j)),
                  pl.BlockSpec((None, 1, tn), lambda l, j: (l, 0, j))],
        out_specs=pl.BlockSpec((None, r, tn), lambda l, j: (l, 0, j)),
        out_shape=jax.ShapeDtypeStruct((depth, r, n), F32),
        compiler_params=_cparams(("parallel", "parallel")),
        name="ada_mod",
    )(s_rows, w_ada, b_ada.reshape(depth, 1, n))


def _in_proj_kernel(x_ref, sh_ref, sc_ref, g_ref, w_ref, wg_ref, *rest, tm, rows_per_step):
    o_ref, og_ref, h_scr = rest[-3], rest[-2], rest[-1]

    @pl.when(pl.program_id(2) == 0)
    def _():
        def body(r, carry):
            sl = pl.ds(pl.multiple_of(r * rows_per_step, rows_per_step), rows_per_step)
            h = _rms(x_ref[sl, :], g_ref[...]) * (1.0 + sc_ref[...]) + sh_ref[...]
            h_scr[sl, :] = h.astype(BF16)
            return carry
        lax.fori_loop(0, tm // rows_per_step, body, 0)
        og_ref[...] = jnp.dot(h_scr[...], wg_ref[...], preferred_element_type=F32)

    o_ref[...] = jnp.dot(h_scr[...], w_ref[...], preferred_element_type=F32).astype(BF16)


def _in_proj(xs, shift, scale, g, w, w_gate, tt, row_off, tm, bufs=None, tn=1280):
    b, t, d = xs.shape
    nz = w.shape[1]
    batched = shift.shape[0] > 1
    mod_map = (lambda bb, i, j: (bb, 0, 0)) if batched else (lambda bb, i, j: (0, 0, 0))
    blk_off = row_off // tm
    in_specs = [pl.BlockSpec((None, tm, d), lambda bb, i, j: (bb, i, 0)),
                pl.BlockSpec((None, 1, d), mod_map),
                pl.BlockSpec((None, 1, d), mod_map),
                pl.BlockSpec((1, d), lambda bb, i, j: (0, 0)),
                pl.BlockSpec((d, tn), lambda bb, i, j: (0, j)),
                pl.BlockSpec((d, LANE), lambda bb, i, j: (0, 0))]
    args = [xs, shift, scale, g, w, w_gate]
    aliases = {}
    if bufs is not None:
        in_specs += [pl.BlockSpec(memory_space=pl.ANY), pl.BlockSpec(memory_space=pl.ANY)]
        args += list(bufs)
        aliases = {6: 0, 7: 1}
    return pl.pallas_call(
        functools.partial(_in_proj_kernel, tm=tm, rows_per_step=min(tm, 256)),
        grid=(b, t // tm, nz // tn),
        in_specs=in_specs,
        out_specs=[pl.BlockSpec((None, tm, tn), lambda bb, i, j: (bb, i + blk_off, j)),
                   pl.BlockSpec((None, tm, LANE), lambda bb, i, j: (bb, i + blk_off, 0))],
        out_shape=[jax.ShapeDtypeStruct((b, tt, nz), BF16), jax.ShapeDtypeStruct((b, tt, LANE), F32)],
        scratch_shapes=[pltpu.VMEM((tm, d), BF16)],
        input_output_aliases=aliases,
        compiler_params=_cparams(("parallel", "parallel", "arbitrary")),
        name="in_proj",
    )(*args)


def _prefix_lanes(x, op, fill):
    lane = lax.broadcasted_iota(jnp.int32, x.shape, 1)
    sh = 1
    while sh < x.shape[1]:
        x = op(x, jnp.where(lane >= sh, pltpu.roll(x, sh, 1), fill))
        sh *= 2
    return x


def _cummax_rows(x, reverse):
    n = x.shape[0]
    r = lax.broadcasted_iota(jnp.int32, x.shape, 0)
    sh = 1
    while sh < n:
        if reverse:
            x = jnp.maximum(x, jnp.where(r < n - sh, pltpu.roll(x, n - sh, 0), -jnp.inf))
        else:
            x = jnp.maximum(x, jnp.where(r >= sh, pltpu.roll(x, sh, 0), -jnp.inf))
        sh *= 2
    return x


def _mlstm_kernel(q_ref, k_ref, v_ref, o_ref, gt_ref, gb_ref, ng_ref, out_ref,
                  gc_scr, gr_scr, u_scr, cp_scr, st_scr, sa_scr, sc_scr, sm_scr, *, n_lat, n_ctx):
    L = CHUNK
    nc = n_lat + n_ctx
    head = pl.program_id(1)
    row = lax.broadcasted_iota(jnp.int32, (L, L), 0)
    col = lax.broadcasted_iota(jnp.int32, (L, L), 1)
    lower = col <= row
    upper = col >= row
    k_scale = HEAD_DIM ** -0.5
    ones_b = jnp.ones((L, LANE), BF16)

    def rows_of(c):
        return pl.ds(pl.multiple_of(c * L, L), L)

    def step_of(c, direction):
        if direction == 0:
            return jnp.where(c < n_lat, c + n_ctx, c - n_lat)
        return nc - 1 - c

    @pl.when(head == 0)
    def _():
        lower_f = lower.astype(F32)
        upper_f = upper.astype(F32)
        is_cum = (col % 8) >= 4
        is_bwd = (col % 16) >= 8

        def prep(c, carry):
            G = gt_ref[rows_of(c), :] + gb_ref[...]
            Gf = _log_sigmoid(G)
            Bf = jnp.dot(lower_f, Gf, precision=lax.Precision.HIGHEST, preferred_element_type=F32)
            Bb = jnp.dot(upper_f, Gf, precision=lax.Precision.HIGHEST, preferred_element_type=F32)
            Bsel = jnp.where(is_bwd, Bb, Bf)
            R = G - pltpu.roll(Bsel, LANE - 4, 1)
            CM = jnp.where(is_bwd, _cummax_rows(R, True), _cummax_rows(R, False))
            low = jnp.where(is_cum, Bsel, G)
            high = jnp.where(is_cum, pltpu.roll(CM, 20, 1), pltpu.roll(R, 16, 1))
            Gc = jnp.where(col < 16, low, jnp.where(col < 32, high, 0.0))
            gc_scr[rows_of(c), :] = Gc
            gr_scr[c] = Gc.T[0:GATE_ROWS, :]
            return carry

        lax.fori_loop(0, nc, prep, 0)

    def lane_col(x, lane_idx):
        return jnp.sum(jnp.where(col == lane_idx, x, 0.0), axis=1, keepdims=True)

    row_g = lax.broadcasted_iota(jnp.int32, (GATE_ROWS, L), 0)
    lane_r = lax.broadcasted_iota(jnp.int32, (1, LANE), 1)

    def gate_row(x, idx):
        return jnp.sum(jnp.where(row_g == idx, x, 0.0), axis=0, keepdims=True)

    st_scr[...] = jnp.zeros_like(st_scr)

    def summarize(c, carry):
        kT = k_ref[rows_of(c), :].astype(F32).T
        vaug = jnp.concatenate([v_ref[rows_of(c), :], ones_b], axis=1)
        grv = gr_scr[c]
        for direction in (0, 1):
            ci = direction * 8 + head
            b_row = gate_row(grv, ci + 4)
            i_row = gate_row(grv, ci)
            total = b_row[:, L - 1:L] if direction == 0 else b_row[:, 0:1]
            g_row = total - b_row + i_row
            mg = jnp.max(g_row, axis=1, keepdims=True)
            kwT = kT * (jnp.exp(g_row - mg) * k_scale)
            step = step_of(c, direction)
            u_scr[direction * nc + step] = jnp.dot(kwT.astype(BF16), vaug, preferred_element_type=F32)
            st_scr[direction, pl.ds(step, 1), :] = jnp.where(lane_r < 64, total, mg)
        return carry

    lax.fori_loop(0, nc, summarize, 0, unroll=2)

    lane8 = lax.broadcasted_iota(jnp.int32, (8, LANE), 1)
    for direction in (0, 1):
        xt = st_scr[direction].T
        valid = lane8 < nc
        T = jnp.where(valid, jnp.broadcast_to(xt[0:1, :], (8, LANE)), 0.0)
        G = jnp.where(valid, jnp.broadcast_to(xt[64:65, :], (8, LANE)), -jnp.inf)
        PT = _prefix_lanes(T, jnp.add, 0.0)
        mvec = PT + jnp.maximum(_prefix_lanes(G - PT, jnp.maximum, -jnp.inf), 0.0)
        mprev = jnp.where(lane8 >= 1, pltpu.roll(mvec, 1, 1), 0.0)
        a = jnp.exp(T + mprev - mvec)
        c2 = jnp.exp(G - mvec)
        sa_scr[direction] = jnp.broadcast_to(a[0:1, :], (LANE, LANE)).T
        sc_scr[direction] = jnp.broadcast_to(c2[0:1, :], (LANE, LANE)).T
        sm_scr[direction] = jnp.broadcast_to(mprev[0:1, :], (LANE, LANE)).T

    def scan(s, carry):
        new = []
        for direction in (0, 1):
            C, Nb = carry[direction]
            idx = direction * nc + s
            cp_scr[idx] = jnp.concatenate([C, Nb], axis=1).astype(BF16)
            a = sa_scr[direction, pl.ds(s, 1), :]
            c2 = sc_scr[direction, pl.ds(s, 1), :]
            U = u_scr[idx]
            new.append((a * C + c2 * U[:, :HEAD_DIM], a * Nb + c2 * U[:, HEAD_DIM:]))
        return tuple(new)

    zero = (jnp.zeros((HEAD_DIM, HEAD_DIM), F32), jnp.zeros((HEAD_DIM, LANE), F32))
    lax.fori_loop(0, nc, scan, (zero, zero))

    def emit(c, carry):
        q = q_ref[rows_of(c), :]
        vaug = jnp.concatenate([v_ref[rows_of(c), :], ones_b], axis=1)
        gcv = gc_scr[rows_of(c), :]
        grv = gr_scr[c]
        qk = lax.dot_general(q, k_ref[rows_of(c), :], (((1,), (1,)), ((), ())), preferred_element_type=F32) * k_scale
        h = jnp.zeros((L, HEAD_DIM), F32)
        for direction in (0, 1):
            ci = direction * 8 + head
            step = step_of(c, direction)
            b_col = lane_col(gcv, ci + 4)
            cm_col = lane_col(gcv, 16 + ci + 4)
            r_row = gate_row(grv, 16 + ci)
            p = jnp.exp(jnp.where(lower if direction == 0 else upper, r_row - cm_col, -jnp.inf))
            intra = jnp.dot((qk * p).astype(BF16), vaug, preferred_element_type=F32)
            inter = jnp.dot(q, cp_scr[direction * nc + step], preferred_element_type=F32)
            mp = sm_scr[direction, pl.ds(step, 1), :]
            mx = jnp.maximum(mp, cm_col)
            e1 = jnp.exp(mp - mx)
            e2 = jnp.exp(cm_col - mx)
            num = e1 * inter[:, :HEAD_DIM] + e2 * intra[:, :HEAD_DIM]
            den = e1 * inter[:, HEAD_DIM:] + e2 * intra[:, HEAD_DIM:]
            h = h + num / jnp.maximum(jnp.abs(den), jnp.exp(-(b_col + mx)))
        hn = _layernorm(h, ng_ref[...], 0.0)
        out_ref[rows_of(c), :] = (hn * _sigmoid(o_ref[rows_of(c), :].astype(F32))).astype(BF16)
        return carry

    lax.fori_loop(0, nc, emit, 0, unroll=2)


def _mlstm(z, zg, gate_b, norm_g, t_lat, t_ctx):
    b, tt, _ = z.shape
    nc = tt // CHUNK
    assert nc <= 64
    blk = lambda off: pl.BlockSpec((None, tt, LANE), lambda bb, h: (bb, 0, off // LANE + h))
    return pl.pallas_call(
        functools.partial(_mlstm_kernel, n_lat=t_lat // CHUNK, n_ctx=t_ctx // CHUNK),
        grid=(b, ML_HEADS),
        in_specs=[blk(COL_MQ), blk(COL_MK), blk(COL_MV), blk(COL_MO),
                  pl.BlockSpec((None, tt, LANE), lambda bb, h: (bb, 0, 0)),
                  pl.BlockSpec((1, LANE), lambda bb, h: (0, 0)),
                  pl.BlockSpec((1, LANE), lambda bb, h: (0, h))],
        out_specs=pl.BlockSpec((None, tt, LANE), lambda bb, h: (bb, 0, h)),
        out_shape=jax.ShapeDtypeStruct((b, tt, GROUP_W), BF16),
        scratch_shapes=[pltpu.VMEM((tt, LANE), F32),
                        pltpu.VMEM((nc, GATE_ROWS, LANE), F32),
                        pltpu.VMEM((2 * nc, HEAD_DIM, 2 * HEAD_DIM), F32),
                        pltpu.VMEM((2 * nc, HEAD_DIM, 2 * HEAD_DIM), BF16),
                        pltpu.VMEM((2, LANE, LANE), F32),
                        pltpu.VMEM((2, LANE, LANE), F32),
                        pltpu.VMEM((2, LANE, LANE), F32),
                        pltpu.VMEM((2, LANE, LANE), F32)],
        compiler_params=_cparams(("parallel", "arbitrary")),
        name="mlstm",
    )(z, z, z, z, zg, gate_b, norm_g)


def _sg_kernel(u_ref, v_ref, lng_ref, lnb_ref, w_ref, bs_ref, o_ref, *, n_chunks):
    def body(c, carry):
        sl = pl.ds(pl.multiple_of(c * CHUNK, CHUNK), CHUNK)
        u = _gelu_tanh(u_ref[sl, :].astype(F32))
        v = _layernorm(_gelu_tanh(v_ref[sl, :].astype(F32)), lng_ref[...], lnb_ref[...]).astype(BF16)
        for g in range(SG_GROUPS):
            gs = slice(g * LANE, (g + 1) * LANE)
            mixed = jnp.dot(w_ref[g], v[:, gs], preferred_element_type=F32) + bs_ref[g]
            o_ref[sl, gs] = (u[:, gs] * mixed).astype(BF16)
        return carry

    lax.fori_loop(0, n_chunks, body, 0)


def _spatial_gating(z, ln_g, ln_b, w_s, b_s_full, n_rows, tr):
    b = z.shape[0]
    blk = lambda off: pl.BlockSpec((None, tr, GROUP_W), lambda bb, i: (bb, i, off // GROUP_W))
    const2 = lambda bb, i: (0, 0)
    const3 = lambda bb, i: (0, 0, 0)
    return pl.pallas_call(
        functools.partial(_sg_kernel, n_chunks=tr // CHUNK),
        grid=(b, n_rows // tr),
        in_specs=[blk(COL_SU), blk(COL_SV),
                  pl.BlockSpec((1, GROUP_W), const2), pl.BlockSpec((1, GROUP_W), const2),
                  pl.BlockSpec((SG_GROUPS, CHUNK, CHUNK), const3),
                  pl.BlockSpec((SG_GROUPS, CHUNK, LANE), const3)],
        out_specs=pl.BlockSpec((None, tr, GROUP_W), lambda bb, i: (bb, i, 0)),
        out_shape=jax.ShapeDtypeStruct((b, z.shape[1], GROUP_W), BF16),
        compiler_params=_cparams(("parallel", "parallel")),
        name="spatial_gating",
    )(z, z, ln_g, ln_b, w_s, b_s_full)


def _conf_kernel(a_ref, g_ref, ap_ref, gp_ref, an_ref, gn_ref, w_ref, b_ref, lng_ref, lnb_ref, o_ref,
                 y_scr, cv_scr, *, tc, seq_starts, seq_ends):
    i = pl.program_id(1)
    has_prev = functools.reduce(jnp.logical_and, [i != s for s in seq_starts])
    has_next = functools.reduce(jnp.logical_and, [i != e for e in seq_ends])
    H = CONV_HALO

    def glu(a, g):
        return a[...].astype(F32) * _sigmoid(g[...].astype(F32))

    y_scr[H:H + tc, :] = glu(a_ref, g_ref)
    y_scr[0:H, :] = jnp.where(has_prev, glu(ap_ref, gp_ref), 0.0)
    y_scr[H + tc:, :] = jnp.where(has_next, glu(an_ref, gn_ref), 0.0)
    rb = 64
    base = H - (CONV_W // 2)
    for cb in range(GROUP_W // LANE):
        cs = slice(cb * LANE, (cb + 1) * LANE)
        for r in range(tc // rb):
            acc = jnp.zeros((rb, LANE), F32)
            for k in range(CONV_W):
                acc = acc + y_scr[r * rb + base + k:r * rb + base + k + rb, cs] * w_ref[k:k + 1, cs]
            cv_scr[r * rb:(r + 1) * rb, cs] = acc
    y = _layernorm(cv_scr[...] + b_ref[...], lng_ref[...], lnb_ref[...])
    o_ref[...] = (y * _sigmoid(y)).astype(BF16)


def _conformer(z, w_dw, b_dw, ln_g, ln_b, t_lat, n_rows, tc):
    b = z.shape[0]
    hb = tc // CONV_HALO
    n_tiles = n_rows // tc
    n_halo = z.shape[1] // CONV_HALO
    ca, cg = COL_CA // GROUP_W, COL_CG // GROUP_W
    cur = lambda cblk: pl.BlockSpec((None, tc, GROUP_W), lambda bb, i: (bb, i, cblk))
    prev = lambda cblk: pl.BlockSpec((None, CONV_HALO, GROUP_W),
                                     lambda bb, i: (bb, jnp.maximum(i * hb - 1, 0), cblk))
    nxt = lambda cblk: pl.BlockSpec((None, CONV_HALO, GROUP_W),
                                    lambda bb, i: (bb, jnp.minimum((i + 1) * hb, n_halo - 1), cblk))
    const2 = lambda bb, i: (0, 0)
    seq_starts = (0, t_lat // tc)
    seq_ends = (t_lat // tc - 1, z.shape[1] // tc - 1)
    return pl.pallas_call(
        functools.partial(_conf_kernel, tc=tc, seq_starts=seq_starts, seq_ends=seq_ends),
        grid=(b, n_tiles),
        in_specs=[cur(ca), cur(cg), prev(ca), prev(cg), nxt(ca), nxt(cg),
                  pl.BlockSpec((CONV_W, GROUP_W), const2), pl.BlockSpec((1, GROUP_W), const2),
                  pl.BlockSpec((1, GROUP_W), const2), pl.BlockSpec((1, GROUP_W), const2)],
        out_specs=pl.BlockSpec((None, tc, GROUP_W), lambda bb, i: (bb, i, 0)),
        out_shape=jax.ShapeDtypeStruct((b, z.shape[1], GROUP_W), BF16),
        scratch_shapes=[pltpu.VMEM((tc + 2 * CONV_HALO, GROUP_W), F32), pltpu.VMEM((tc, GROUP_W), F32)],
        compiler_params=_cparams(("parallel", "parallel")),
        name="conformer_conv",
    )(z, z, z, z, z, z, w_dw, b_dw, ln_g, ln_b)


def _attn_prep_kernel(q_ref, k_ref, cos_ref, sin_ref, qg_ref, kg_ref, qo_ref, ko_ref):
    cosf = cos_ref[...]
    sinf = sin_ref[...]

    def norm_rope(x, g, scale):
        y = _rms(x.astype(F32), g)
        return (y * cosf + pltpu.roll(y, HEAD_DIM // 2, 1) * sinf) * scale

    for h in range(ATT_Q_HEADS):
        hs = slice(h * HEAD_DIM, (h + 1) * HEAD_DIM)
        qo_ref[:, hs] = norm_rope(q_ref[:, hs], qg_ref[...], HEAD_DIM ** -0.5).astype(BF16)
    for h in range(ATT_KV_HEADS):
        hs = slice(h * HEAD_DIM, (h + 1) * HEAD_DIM)
        ko_ref[:, hs] = norm_rope(k_ref[:, hs], kg_ref[...], 1.0).astype(BF16)


def _attn_prep(z, cosf, sinf, qg, kg, tr=256):
    b, tt, _ = z.shape
    qw, kw = ATT_Q_HEADS * HEAD_DIM, ATT_KV_HEADS * HEAD_DIM
    const2 = lambda bb, i: (0, 0)
    return pl.pallas_call(
        _attn_prep_kernel,
        grid=(b, tt // tr),
        in_specs=[pl.BlockSpec((None, tr, qw), lambda bb, i: (bb, i, COL_AQ // qw)),
                  pl.BlockSpec((None, tr, kw), lambda bb, i: (bb, i, COL_AK // kw)),
                  pl.BlockSpec((tr, HEAD_DIM), lambda bb, i: (i, 0)),
                  pl.BlockSpec((tr, HEAD_DIM), lambda bb, i: (i, 0)),
                  pl.BlockSpec((1, HEAD_DIM), const2), pl.BlockSpec((1, HEAD_DIM), const2)],
        out_specs=[pl.BlockSpec((None, tr, qw), lambda bb, i: (bb, i, 0)),
                   pl.BlockSpec((None, tr, kw), lambda bb, i: (bb, i, 0))],
        out_shape=[jax.ShapeDtypeStruct((b, tt, qw), BF16),
                   jax.ShapeDtypeStruct((b, tt, kw), BF16)],
        compiler_params=_cparams(("parallel", "parallel")),
        name="attn_prep",
    )(z, z, cosf, sinf, qg, kg)


def _attn_kernel(q_ref, k_ref, v_ref, o_ref, *, t_lat, n_lat_tiles):
    i = pl.program_id(2)
    group = ATT_Q_HEADS // ATT_KV_HEADS

    def run(k, v):
        for g in range(group):
            hs = slice(g * HEAD_DIM, (g + 1) * HEAD_DIM)
            s = lax.dot_general(q_ref[:, hs], k, (((1,), (1,)), ((), ())), preferred_element_type=F32)
            p = jnp.exp(s - jnp.max(s, axis=-1, keepdims=True))
            l = jnp.sum(p, axis=-1, keepdims=True)
            o = jnp.dot(p.astype(BF16), v, preferred_element_type=F32)
            o_ref[:, hs] = (o / l).astype(BF16)

    @pl.when(i < n_lat_tiles)
    def _():
        run(k_ref[...], v_ref[...])

    @pl.when(i >= n_lat_tiles)
    def _():
        run(k_ref[t_lat:, :], v_ref[t_lat:, :])


def _attention(qr, kr, z, t_lat, n_rows, tq=256):
    b, tt, _ = qr.shape
    gw = (ATT_Q_HEADS // ATT_KV_HEADS) * HEAD_DIM
    return pl.pallas_call(
        functools.partial(_attn_kernel, t_lat=t_lat, n_lat_tiles=t_lat // tq),
        grid=(b, ATT_KV_HEADS, n_rows // tq),
        in_specs=[pl.BlockSpec((None, tq, gw), lambda bb, h, i: (bb, i, h)),
                  pl.BlockSpec((None, tt, HEAD_DIM), lambda bb, h, i: (bb, 0, h)),
                  pl.BlockSpec((None, tt, HEAD_DIM), lambda bb, h, i: (bb, 0, COL_AV // HEAD_DIM + h))],
        out_specs=pl.BlockSpec((None, tq, gw), lambda bb, h, i: (bb, i, h)),
        out_shape=jax.ShapeDtypeStruct((b, tt, ATT_Q_HEADS * HEAD_DIM), BF16),
        compiler_params=_cparams(("parallel", "parallel", "arbitrary")),
        name="attention",
    )(qr, kr, z)


def _out_proj_kernel(ya_ref, yb_ref, yc_ref, yd_ref, w_ref, x_ref, gate_ref, pg_ref, o_ref):
    acc = jnp.dot(ya_ref[...], w_ref[0 * GROUP_W:1 * GROUP_W, :], preferred_element_type=F32)
    acc += jnp.dot(yb_ref[...], w_ref[1 * GROUP_W:2 * GROUP_W, :], preferred_element_type=F32)
    acc += jnp.dot(yc_ref[...], w_ref[2 * GROUP_W:3 * GROUP_W, :], preferred_element_type=F32)
    acc += jnp.dot(yd_ref[...], w_ref[3 * GROUP_W:4 * GROUP_W, :], preferred_element_type=F32)
    o_ref[...] = x_ref[...] + gate_ref[...] * _rms(acc, pg_ref[...])


def _out_proj(ys, w_out, xs, gate, post_g, row_off, tm):
    b, t, d = xs.shape
    batched = gate.shape[0] > 1
    mod_map = (lambda bb, i: (bb, 0, 0)) if batched else (lambda bb, i: (0, 0, 0))
    blk_off = row_off // tm
    yspec = pl.BlockSpec((None, tm, GROUP_W), lambda bb, i: (bb, i + blk_off, 0))
    return pl.pallas_call(
        _out_proj_kernel,
        grid=(b, t // tm),
        in_specs=[yspec, yspec, yspec, yspec,
                  pl.BlockSpec(w_out.shape, lambda bb, i: (0, 0)),
                  pl.BlockSpec((None, tm, d), lambda bb, i: (bb, i, 0)),
                  pl.BlockSpec((None, 1, d), mod_map),
                  pl.BlockSpec((1, d), lambda bb, i: (0, 0))],
        out_specs=pl.BlockSpec((None, tm, d), lambda bb, i: (bb, i, 0)),
        out_shape=jax.ShapeDtypeStruct((b, t, d), F32),
        compiler_params=_cparams(("parallel", "parallel")),
        name="out_proj",
    )(*ys, w_out, xs, gate, post_g)


def _ffn_kernel(x_ref, xp_ref, xn_ref, sh_ref, sc_ref, gate_ref, pre_g_ref, post_g_ref, wg_ref, wu_ref,
                cw_ref, cb_ref, wd_ref, o_ref, h_scr, acc_scr, *, tm, n_tiles, n_ff):
    i = pl.program_id(1)
    j = pl.program_id(2)
    H = FFN_HALO

    @pl.when(j == 0)
    def _():
        def pre(x):
            return _rms(x, pre_g_ref[...]) * (1.0 + sc_ref[...]) + sh_ref[...]
        h_scr[H:H + tm, :] = pre(x_ref[...]).astype(BF16)
        h_scr[0:H, :] = jnp.where(i > 0, pre(xp_ref[...]), 0.0).astype(BF16)
        h_scr[H + tm:, :] = jnp.where(i < n_tiles - 1, pre(xn_ref[...]), 0.0).astype(BF16)
        acc_scr[...] = jnp.zeros_like(acc_scr)

    n_ext = tm + 2 * H
    g_ext = jnp.dot(h_scr[...], wg_ref[...], preferred_element_type=F32)
    u = jnp.dot(h_scr[H:H + tm, :], wu_ref[...], preferred_element_type=F32)
    g_prev = pltpu.roll(g_ext, 1, 0)[H:H + tm]
    g_next = pltpu.roll(g_ext, n_ext - 1, 0)[H:H + tm]
    conv = g_prev * cw_ref[0:1, :] + g_ext[H:H + tm] * cw_ref[1:2, :] + g_next * cw_ref[2:3, :] + cb_ref[...]
    act = (conv * _sigmoid(conv)) * u
    acc_scr[...] += jnp.dot(act.astype(BF16), wd_ref[...], preferred_element_type=F32)

    @pl.when(j == n_ff - 1)
    def _():
        o_ref[...] = x_ref[...] + gate_ref[...] * _rms(acc_scr[...], post_g_ref[...])


def _ffn(xs, shift, scale, gate, pre_g, post_g, w_up, cv_w, cv_b, w_down, tm, tf=512):
    b, t, d = xs.shape
    d_ff = w_down.shape[0]
    n_ff = d_ff // tf
    n_tiles = t // tm
    hb = tm // FFN_HALO
    n_halo = t // FFN_HALO
    batched = shift.shape[0] > 1
    mod_map = (lambda bb, i, j: (bb, 0, 0)) if batched else (lambda bb, i, j: (0, 0, 0))
    const2 = lambda bb, i, j: (0, 0)
    return pl.pallas_call(
        functools.partial(_ffn_kernel, tm=tm, n_tiles=n_tiles, n_ff=n_ff),
        grid=(b, n_tiles, n_ff),
        in_specs=[pl.BlockSpec((None, tm, d), lambda bb, i, j: (bb, i, 0)),
                  pl.BlockSpec((None, FFN_HALO, d), lambda bb, i, j: (bb, jnp.maximum(i * hb - 1, 0), 0)),
                  pl.BlockSpec((None, FFN_HALO, d), lambda bb, i, j: (bb, jnp.minimum((i + 1) * hb, n_halo - 1), 0)),
                  pl.BlockSpec((None, 1, d), mod_map), pl.BlockSpec((None, 1, d), mod_map),
                  pl.BlockSpec((None, 1, d), mod_map),
                  pl.BlockSpec((1, d), const2), pl.BlockSpec((1, d), const2),
                  pl.BlockSpec((d, tf), lambda bb, i, j: (0, j)),
                  pl.BlockSpec((d, tf), lambda bb, i, j: (0, j + n_ff)),
                  pl.BlockSpec((cv_w.shape[0], tf), lambda bb, i, j: (0, j)),
                  pl.BlockSpec((1, tf), lambda bb, i, j: (0, j)),
                  pl.BlockSpec((tf, d), lambda bb, i, j: (j, 0))],
        out_specs=pl.BlockSpec((None, tm, d), lambda bb, i, j: (bb, i, 0)),
        out_shape=jax.ShapeDtypeStruct((b, t, d), F32),
        scratch_shapes=[pltpu.VMEM((tm + 2 * FFN_HALO, d), BF16), pltpu.VMEM((tm, d), F32)],
        compiler_params=_cparams(("parallel", "parallel", "arbitrary")),
        name="conv_glu_ffn",
    )(xs, xs, xs, shift, scale, gate, pre_g, post_g, w_up, w_up, cv_w, cv_b, w_down)


def _pack_w_in(w_in):
    d = w_in.shape[0]
    main = jnp.concatenate([w_in[:, :4 * GROUP_W], w_in[:, 4 * GROUP_W + N_GATE:]], axis=1).astype(BF16)
    gates = w_in[:, 4 * GROUP_W:4 * GROUP_W + N_GATE]
    gates = jnp.concatenate([gates, jnp.zeros((d, LANE - N_GATE), w_in.dtype)], axis=1).astype(BF16)
    return main, gates


def _rope_tables(t_lat, t_ctx):
    t = jnp.arange(t_lat)
    row = (t // GRID_W).astype(F32)
    colp = (t % GRID_W).astype(F32)
    axis_dim = HEAD_DIM // 2
    inv = jnp.power(ROPE_THETA, -jnp.arange(0, axis_dim, 2, dtype=F32) / axis_dim)
    ang = jnp.concatenate([row[:, None] * inv, colp[:, None] * inv], axis=-1)
    cos, sin = jnp.cos(ang), jnp.sin(ang)
    cosf = jnp.concatenate([cos, cos], axis=-1)
    sinf = jnp.concatenate([-sin, sin], axis=-1)
    cosf = jnp.concatenate([cosf, jnp.ones((t_ctx, HEAD_DIM), F32)], axis=0)
    sinf = jnp.concatenate([sinf, jnp.zeros((t_ctx, HEAD_DIM), F32)], axis=0)
    return cosf, sinf


def kernel(x, c, ctx, c_ctx, w_ada, b_ada, pre_mix_g, post_mix_g, w_in, ml_gate_b, ml_norm_g, sg_ln_g, sg_ln_b,
           sg_w, sg_b, cv_w, cv_b, cv_ln_g, cv_ln_b, at_qn_g, at_kn_g, w_out, pre_ffn_g, post_ffn_g, w_up,
           ffn_cv_w, ffn_cv_b, w_down):
    b, t_lat, d = x.shape
    t_ctx = ctx.shape[1]
    tt = t_lat + t_ctx
    depth = w_ada.shape[0]
    tm_in = min(1024, t_lat)
    tm_lat = min(512, t_lat)
    assert t_lat % tm_in == 0 and t_lat % t_ctx == 0 and t_ctx % CHUNK == 0 and t_ctx % 256 == 0

    n_rows = -(-(b + 1) // 8) * 8
    s_rows = jnp.concatenate([c, c_ctx[None, :], jnp.zeros((n_rows - b - 1, d), F32)], axis=0)
    mod = _ada(s_rows, w_ada, b_ada)
    cosf, sinf = _rope_tables(t_lat, t_ctx)
    row2 = lambda a: a.reshape(1, -1)

    xc = ctx
    for i in range(depth):
        need_ctx = i < depth - 1
        mod_l = [m[:, None, :] for m in jnp.split(mod[i, :b], 6, axis=-1)]
        mod_c = [m[:, None, :] for m in jnp.split(mod[i, b:b + 1], 6, axis=-1)]
        sh1, sc1, g1, sh2, sc2, g2 = mod_l
        sh1c, sc1c, g1c, sh2c, sc2c, g2c = mod_c

        w_main, w_gate = _pack_w_in(w_in[i])
        bufs = _in_proj(x, sh1, sc1, row2(pre_mix_g[i]), w_main, w_gate, tt, 0, tm_in)
        z, zg = _in_proj(xc, sh1c, sc1c, row2(pre_mix_g[i]), w_main, w_gate, tt, t_lat, t_ctx, bufs=bufs)

        gate_b = jnp.concatenate([ml_gate_b[i], jnp.zeros((LANE - N_GATE,), F32)]).reshape(1, LANE)
        ya = _mlstm(z, zg, gate_b, row2(ml_norm_g[i]), t_lat, t_ctx)

        mix_rows = tt if need_ctx else t_lat
        sg_tr = 2176 if (need_ctx and tt % 2176 == 0) else (2048 if t_lat % 2048 == 0 else 256)
        sg_tr = sg_tr if mix_rows % sg_tr == 0 else 256
        b_s_full = jnp.broadcast_to(sg_b[i][:, :, None], (SG_GROUPS, CHUNK, LANE))
        yb = _spatial_gating(z, row2(sg_ln_g[i]), row2(sg_ln_b[i]), sg_w[i].astype(BF16), b_s_full, mix_rows, sg_tr)

        yc = _conformer(z, cv_w[i], row2(cv_b[i]), row2(cv_ln_g[i]), row2(cv_ln_b[i]), t_lat, mix_rows, 256)

        qr, kr = _attn_prep(z, cosf, sinf, row2(at_qn_g[i]), row2(at_kn_g[i]))
        yd = _attention(qr, kr, z, t_lat, mix_rows)

        w_out_b = w_out[i].astype(BF16)
        w_up_b = w_up[i].astype(BF16)
        w_down_b = w_down[i].astype(BF16)
        ys = (ya, yb, yc, yd)
        x = _out_proj(ys, w_out_b, x, g1, row2(post_mix_g[i]), 0, tm_lat)
        x = _ffn(x, sh2, sc2, g2, row2(pre_ffn_g[i]), row2(post_ffn_g[i]), w_up_b, ffn_cv_w[i],
                 row2(ffn_cv_b[i]), w_down_b, tm_lat)
        if need_ctx:
            xc = _out_proj(ys, w_out_b, xc, g1c, row2(post_mix_g[i]), t_lat, t_ctx)
            xc = _ffn(xc, sh2c, sc2c, g2c, row2(pre_ffn_g[i]), row2(post_ffn_g[i]), w_up_b, ffn_cv_w[i],
                      row2(ffn_cv_b[i]), w_down_b, t_ctx)
    return x
```

```python
import functools
import math

import jax
import jax.numpy as jnp
from jax import lax
from jax.experimental import pallas as pl
from jax.experimental.pallas import tpu as pltpu

F32 = jnp.float32
BF16 = jnp.bfloat16

EPS = 1e-6
LANE = 128
SUBLANES = 8
HEAD_DIM = 128
GROUP_W = 512
ML_HEADS = 4
CHUNK = 128
SG_GROUPS = 4
CONV_W = 31
CONV_HALO = 16
ATT_Q_HEADS = 4
ATT_KV_HEADS = 2
GRID_W = 64
ROPE_THETA = 10000.0
FFN_HALO = 16
N_GATE = 16
GATE_ROWS = 32

COL_MQ, COL_MK, COL_MV, COL_MO = 0, 512, 1024, 1536
COL_SU, COL_SV = 2048, 2560
COL_CA, COL_CG = 3072, 3584
COL_AQ, COL_AK, COL_AV = 4096, 4608, 4864
NZ = 5120

VMEM_LIMIT = 52 * 1024 * 1024


def _cparams(sem):
    return pltpu.CompilerParams(dimension_semantics=sem, vmem_limit_bytes=VMEM_LIMIT)


def _rms(x, g):
    ms = jnp.mean(x * x, axis=-1, keepdims=True)
    return x * lax.rsqrt(ms + EPS) * g


def _layernorm(x, g, b):
    mu = jnp.mean(x, axis=-1, keepdims=True)
    d = x - mu
    var = jnp.mean(d * d, axis=-1, keepdims=True)
    return d * lax.rsqrt(var + EPS) * g + b


def _sigmoid(x):
    return 1.0 / (1.0 + jnp.exp(-x))


def _gelu_tanh(x):
    c = math.sqrt(2.0 / math.pi)
    return x * (0.5 * (1.0 + jnp.tanh(c * (x + 0.044715 * (x * x * x)))))


def _log_sigmoid(x):
    return jnp.minimum(x, 0.0) - jnp.log1p(jnp.exp(-jnp.abs(x)))


def _ada_kernel(s_ref, w_ref, b_ref, o_ref):
    s = s_ref[...]
    s = s * _sigmoid(s)
    o_ref[...] = jnp.dot(s.astype(BF16), w_ref[...].astype(BF16), preferred_element_type=F32) + b_ref[...]


def _ada(s_rows, w_ada, b_ada, tn=1024):
    depth, d, n = w_ada.shape
    r = s_rows.shape[0]
    return pl.pallas_call(
        _ada_kernel,
        grid=(depth, n // tn),
        in_specs=[pl.BlockSpec((r, d), lambda l, j: (0, 0)),
                  pl.BlockSpec((None, d, tn), lambda l, j: (l, 0, j)),
                  pl.BlockSpec((None, 1, tn), lambda l, j: (l, 0, j))],
        out_specs=pl.BlockSpec((None, r, tn), lambda l, j: (l, 0, j)),
        out_shape=jax.ShapeDtypeStruct((depth, r, n), F32),
        compiler_params=_cparams(("parallel", "parallel")),
        name="ada_mod",
    )(s_rows, w_ada, b_ada.reshape(depth, 1, n))


def _in_proj_kernel(x_ref, sh_ref, sc_ref, g_ref, w_ref, wg_ref, *rest, tm, rows_per_step):
    o_ref, og_ref, h_scr = rest[-3], rest[-2], rest[-1]

    @pl.when(pl.program_id(2) == 0)
    def _():
        def body(r, carry):
            sl = pl.ds(pl.multiple_of(r * rows_per_step, rows_per_step), rows_per_step)
            h = _rms(x_ref[sl, :], g_ref[...]) * (1.0 + sc_ref[...]) + sh_ref[...]
            h_scr[sl, :] = h.astype(BF16)
            return carry
        lax.fori_loop(0, tm // rows_per_step, body, 0)
        og_ref[...] = jnp.dot(h_scr[...], wg_ref[...], preferred_element_type=F32)

    o_ref[...] = jnp.dot(h_scr[...], w_ref[...], preferred_element_type=F32).astype(BF16)


def _in_proj(xs, shift, scale, g, w, w_gate, tt, row_off, tm, bufs=None, tn=1280):
    b, t, d = xs.shape
    nz = w.shape[1]
    batched = shift.shape[0] > 1
    mod_map = (lambda bb, i, j: (bb, 0, 0)) if batched else (lambda bb, i, j: (0, 0, 0))
    blk_off = row_off // tm
    in_specs = [pl.BlockSpec((None, tm, d), lambda bb, i, j: (bb, i, 0)),
                pl.BlockSpec((None, 1, d), mod_map),
                pl.BlockSpec((None, 1, d), mod_map),
                pl.BlockSpec((1, d), lambda bb, i, j: (0, 0)),
                pl.BlockSpec((d, tn), lambda bb, i, j: (0, j)),
                pl.BlockSpec((d, LANE), lambda bb, i, j: (0, 0))]
    args = [xs, shift, scale, g, w, w_gate]
    aliases = {}
    if bufs is not None:
        in_specs += [pl.BlockSpec(memory_space=pl.ANY), pl.BlockSpec(memory_space=pl.ANY)]
        args += list(bufs)
        aliases = {6: 0, 7: 1}
    return pl.pallas_call(
        functools.partial(_in_proj_kernel, tm=tm, rows_per_step=min(tm, 256)),
        grid=(b, t // tm, nz // tn),
        in_specs=in_specs,
        out_specs=[pl.BlockSpec((None, tm, tn), lambda bb, i, j: (bb, i + blk_off, j)),
                   pl.BlockSpec((None, tm, LANE), lambda bb, i, j: (bb, i + blk_off, 0))],
        out_shape=[jax.ShapeDtypeStruct((b, tt, nz), BF16), jax.ShapeDtypeStruct((b, tt, LANE), F32)],
        scratch_shapes=[pltpu.VMEM((tm, d), BF16)],
        input_output_aliases=aliases,
        compiler_params=_cparams(("parallel", "parallel", "arbitrary")),
        name="in_proj",
    )(*args)


def _prefix_lanes(x, op, fill):
    lane = lax.broadcasted_iota(jnp.int32, x.shape, 1)
    sh = 1
    while sh < x.shape[1]:
        x = op(x, jnp.where(lane >= sh, pltpu.roll(x, sh, 1), fill))
        sh *= 2
    return x


def _suffix_lanes(x, op, fill):
    n = x.shape[1]
    lane = lax.broadcasted_iota(jnp.int32, x.shape, 1)
    sh = 1
    while sh < n:
        x = op(x, jnp.where(lane < n - sh, pltpu.roll(x, n - sh, 1), fill))
        sh *= 2
    return x


def _mlstm_kernel(q_ref, k_ref, v_ref, o_ref, gt_ref, gb_ref, ng_ref, out_ref,
                  rb_scr, gr_scr, vt_scr, u_scr, cp_scr, st_scr, sa_scr, sc_scr, sm_scr, *, n_lat, n_ctx):
    L = CHUNK
    nc = n_lat + n_ctx
    VT_ROWS = vt_scr.shape[1]
    head = pl.program_id(1)
    row = lax.broadcasted_iota(jnp.int32, (L, L), 0)
    col = lax.broadcasted_iota(jnp.int32, (L, L), 1)
    lower = col <= row
    upper = col >= row
    k_scale = HEAD_DIM ** -0.5
    HALF = GATE_ROWS // 2

    def rows_of(c):
        return pl.ds(pl.multiple_of(c * L, L), L)

    def step_of(c, direction):
        if direction == 0:
            return jnp.where(c < n_lat, c + n_ctx, c - n_lat)
        return nc - 1 - c

    @pl.when(head == 0)
    def _():
        lower_f = lower.astype(F32)
        upper_f = upper.astype(F32)
        n_all = nc * HALF
        rg = lax.broadcasted_iota(jnp.int32, (n_all, L), 0) % HALF
        is_cum = (rg % 8) >= 4
        is_bwd = rg >= 8

        def to_rows(c, carry):
            gr_scr[c, 0:HALF, :] = (gt_ref[rows_of(c), :] + gb_ref[...]).T[0:HALF, :]
            return carry

        lax.fori_loop(0, nc, to_rows, 0, unroll=2)

        G = gr_scr[:, 0:HALF, :].reshape(n_all, L)
        Gf = _log_sigmoid(G)
        Bf = jnp.dot(Gf, upper_f, precision=lax.Precision.HIGHEST, preferred_element_type=F32)
        Bb = jnp.dot(Gf, lower_f, precision=lax.Precision.HIGHEST, preferred_element_type=F32)
        Bsel = jnp.where(is_bwd, Bb, Bf)
        R = G - pltpu.roll(Bsel, n_all - 4, 0)
        CM = jnp.where(is_bwd, _suffix_lanes(R, jnp.maximum, -jnp.inf), _prefix_lanes(R, jnp.maximum, -jnp.inf))
        gr_scr[:, 0:HALF, :] = jnp.where(is_cum, Bsel, G).reshape(nc, HALF, L)
        gr_scr[:, HALF:GATE_ROWS, :] = jnp.where(is_cum, pltpu.roll(CM, 4, 0), R).reshape(nc, HALF, L)

    row_g = lax.broadcasted_iota(jnp.int32, (GATE_ROWS, L), 0)
    lane_r = lax.broadcasted_iota(jnp.int32, (1, LANE), 1)

    def gate_row(x, idx):
        return jnp.sum(jnp.where(row_g == idx, x, 0.0), axis=0, keepdims=True)

    st_scr[...] = jnp.zeros_like(st_scr)
    ones_rows = jnp.ones((VT_ROWS - HEAD_DIM, L), F32)

    def summarize(c, carry):
        vt = jnp.concatenate([v_ref[rows_of(c), :].astype(F32).T, ones_rows], axis=0)
        vt_scr[c] = vt.astype(BF16)
        k = k_ref[rows_of(c), :]
        grv = gr_scr[c]
        for direction in (0, 1):
            ci = direction * 8 + head
            b_row = gate_row(grv, ci + 4)
            i_row = gate_row(grv, ci)
            total = b_row[:, L - 1:L] if direction == 0 else b_row[:, 0:1]
            g_row = total - b_row + i_row
            mg = jnp.max(g_row, axis=1, keepdims=True)
            w_row = jnp.exp(g_row - mg) * k_scale
            step = step_of(c, direction)
            u_scr[direction * nc + step] = jnp.dot((vt * w_row).astype(BF16), k, preferred_element_type=F32)
            st_scr[direction, pl.ds(step, 1), :] = jnp.where(lane_r < 64, total, mg)
            rb_scr[direction * nc + c] = jnp.broadcast_to(gate_row(grv, 16 + ci), (L, L)).T
        return carry

    lax.fori_loop(0, nc, summarize, 0, unroll=2)

    lane8 = lax.broadcasted_iota(jnp.int32, (8, LANE), 1)
    for direction in (0, 1):
        xt = st_scr[direction].T
        valid = lane8 < nc
        T = jnp.where(valid, jnp.broadcast_to(xt[0:1, :], (8, LANE)), 0.0)
        G = jnp.where(valid, jnp.broadcast_to(xt[64:65, :], (8, LANE)), -jnp.inf)
        PT = _prefix_lanes(T, jnp.add, 0.0)
        mvec = PT + jnp.maximum(_prefix_lanes(G - PT, jnp.maximum, -jnp.inf), 0.0)
        mprev = jnp.where(lane8 >= 1, pltpu.roll(mvec, 1, 1), 0.0)
        a = jnp.exp(T + mprev - mvec)
        c2 = jnp.exp(G - mvec)
        sa_scr[direction] = jnp.broadcast_to(a[0:1, :], (LANE, LANE)).T
        sc_scr[direction] = jnp.broadcast_to(c2[0:1, :], (LANE, LANE)).T
        sm_scr[direction] = jnp.broadcast_to(mprev[0:1, :], (LANE, LANE)).T

    def scan(s, carry):
        new = []
        for direction in (0, 1):
            ct = carry[direction]
            idx = direction * nc + s
            cp_scr[idx] = ct.astype(BF16)
            a = sa_scr[direction, pl.ds(s, 1), :]
            c2 = sc_scr[direction, pl.ds(s, 1), :]
            new.append(a * ct + c2 * u_scr[idx])
        return tuple(new)

    zero = jnp.zeros((VT_ROWS, HEAD_DIM), F32)
    lax.fori_loop(0, nc, scan, (zero, zero))

    nt_dims = (((1,), (1,)), ((), ()))

    def emit(c, carry):
        q = q_ref[rows_of(c), :]
        vt = vt_scr[c]
        grv = gr_scr[c]
        qk_t = lax.dot_general(k_ref[rows_of(c), :], q, nt_dims, preferred_element_type=F32) * k_scale
        h_t = jnp.zeros((HEAD_DIM, L), F32)
        for direction in (0, 1):
            ci = direction * 8 + head
            step = step_of(c, direction)
            r_col = rb_scr[direction * nc + c]
            cm_row = gate_row(grv, 16 + ci + 4)
            b_row = gate_row(grv, ci + 4)
            p_t = jnp.exp(jnp.where(upper if direction == 0 else lower, r_col - cm_row, -jnp.inf))
            intra = jnp.dot(vt, (qk_t * p_t).astype(BF16), preferred_element_type=F32)
            inter = lax.dot_general(cp_scr[direction * nc + step], q, nt_dims, preferred_element_type=F32)
            mp = sm_scr[direction, pl.ds(step, 1), :]
            mx = jnp.maximum(mp, cm_row)
            e1 = jnp.exp(mp - mx)
            e2 = jnp.exp(cm_row - mx)
            num = e1 * inter[:HEAD_DIM] + e2 * intra[:HEAD_DIM]
            den = e1 * inter[HEAD_DIM:HEAD_DIM + 1] + e2 * intra[HEAD_DIM:HEAD_DIM + 1]
            h_t = h_t + num * (1.0 / jnp.maximum(jnp.abs(den), jnp.exp(-(b_row + mx))))
        mu = jnp.mean(h_t, axis=0, keepdims=True)
        dev = h_t - mu
        var = jnp.mean(dev * dev, axis=0, keepdims=True)
        hn = (dev * lax.rsqrt(var + EPS)).T * ng_ref[...]
        out_ref[rows_of(c), :] = (hn * _sigmoid(o_ref[rows_of(c), :].astype(F32))).astype(BF16)
        return carry

    lax.fori_loop(0, nc, emit, 0, unroll=2)


def _mlstm(z, zg, gate_b, norm_g, t_lat, t_ctx):
    b, tt, _ = z.shape
    nc = tt // CHUNK
    assert nc <= 64
    vt_rows = HEAD_DIM + 16
    blk = lambda off: pl.BlockSpec((None, tt, LANE), lambda bb, h: (bb, 0, off // LANE + h))
    return pl.pallas_call(
        functools.partial(_mlstm_kernel, n_lat=t_lat // CHUNK, n_ctx=t_ctx // CHUNK),
        grid=(b, ML_HEADS),
        in_specs=[blk(COL_MQ), blk(COL_MK), blk(COL_MV), blk(COL_MO),
                  pl.BlockSpec((None, tt, LANE), lambda bb, h: (bb, 0, 0)),
                  pl.BlockSpec((1, LANE), lambda bb, h: (0, 0)),
                  pl.BlockSpec((1, LANE), lambda bb, h: (0, h))],
        out_specs=pl.BlockSpec((None, tt, LANE), lambda bb, h: (bb, 0, h)),
        out_shape=jax.ShapeDtypeStruct((b, tt, GROUP_W), BF16),
        scratch_shapes=[pltpu.VMEM((2 * nc, CHUNK, CHUNK), F32),
                        pltpu.VMEM((nc, GATE_ROWS, LANE), F32),
                        pltpu.VMEM((nc, vt_rows, CHUNK), BF16),
                        pltpu.VMEM((2 * nc, vt_rows, HEAD_DIM), F32),
                        pltpu.VMEM((2 * nc, vt_rows, HEAD_DIM), BF16),
                        pltpu.VMEM((2, LANE, LANE), F32),
                        pltpu.VMEM((2, LANE, LANE), F32),
                        pltpu.VMEM((2, LANE, LANE), F32),
                        pltpu.VMEM((2, LANE, LANE), F32)],
        compiler_params=_cparams(("parallel", "arbitrary")),
        name="mlstm",
    )(z, z, z, z, zg, gate_b, norm_g)


def _sg_kernel(u_ref, v_ref, lng_ref, lnb_ref, w_ref, bs_ref, o_ref, *, n_chunks):
    def body(c, carry):
        sl = pl.ds(pl.multiple_of(c * CHUNK, CHUNK), CHUNK)
        u = _gelu_tanh(u_ref[sl, :].astype(F32))
        v = _layernorm(_gelu_tanh(v_ref[sl, :].astype(F32)), lng_ref[...], lnb_ref[...]).astype(BF16)
        for g in range(SG_GROUPS):
            gs = slice(g * LANE, (g + 1) * LANE)
            mixed = jnp.dot(w_ref[g], v[:, gs], preferred_element_type=F32) + bs_ref[g]
            o_ref[sl, gs] = (u[:, gs] * mixed).astype(BF16)
        return carry

    lax.fori_loop(0, n_chunks, body, 0)


def _spatial_gating(z, ln_g, ln_b, w_s, b_s_full, n_rows, tr):
    b = z.shape[0]
    blk = lambda off: pl.BlockSpec((None, tr, GROUP_W), lambda bb, i: (bb, i, off // GROUP_W))
    const2 = lambda bb, i: (0, 0)
    const3 = lambda bb, i: (0, 0, 0)
    return pl.pallas_call(
        functools.partial(_sg_kernel, n_chunks=tr // CHUNK),
        grid=(b, n_rows // tr),
        in_specs=[blk(COL_SU), blk(COL_SV),
                  pl.BlockSpec((1, GROUP_W), const2), pl.BlockSpec((1, GROUP_W), const2),
                  pl.BlockSpec((SG_GROUPS, CHUNK, CHUNK), const3),
                  pl.BlockSpec((SG_GROUPS, CHUNK, LANE), const3)],
        out_specs=pl.BlockSpec((None, tr, GROUP_W), lambda bb, i: (bb, i, 0)),
        out_shape=jax.ShapeDtypeStruct((b, z.shape[1], GROUP_W), BF16),
        compiler_params=_cparams(("parallel", "parallel")),
        name="spatial_gating",
    )(z, z, ln_g, ln_b, w_s, b_s_full)


def _conf_kernel(a_ref, g_ref, ap_ref, gp_ref, an_ref, gn_ref, w_ref, b_ref, lng_ref, lnb_ref, o_ref,
                 y_scr, ysh_scr, cv_scr, *, tc, seq_starts, seq_ends):
    i = pl.program_id(1)
    has_prev = functools.reduce(jnp.logical_and, [i != s for s in seq_starts])
    has_next = functools.reduce(jnp.logical_and, [i != e for e in seq_ends])
    H = CONV_HALO

    def glu(a, g):
        return a[...].astype(F32) * _sigmoid(g[...].astype(F32))

    y_scr[H:H + tc, :] = glu(a_ref, g_ref)
    y_scr[0:H, :] = jnp.where(has_prev, glu(ap_ref, gp_ref), 0.0)
    y_scr[H + tc:, :] = jnp.where(has_next, glu(an_ref, gn_ref), 0.0)
    n_sh = ysh_scr.shape[1]
    for j in range(SUBLANES):
        ysh_scr[j] = y_scr[j:j + n_sh, :]
    rb = 64
    base = H - (CONV_W // 2)
    for cb in range(GROUP_W // LANE):
        cs = slice(cb * LANE, (cb + 1) * LANE)
        for r in range(tc // rb):
            acc = jnp.zeros((rb, LANE), F32)
            for k in range(CONV_W):
                a, j = divmod(base + k, SUBLANES)
                acc = acc + ysh_scr[j, r * rb + SUBLANES * a:r * rb + SUBLANES * a + rb, cs] * w_ref[k:k + 1, cs]
            cv_scr[r * rb:(r + 1) * rb, cs] = acc
    y = _layernorm(cv_scr[...] + b_ref[...], lng_ref[...], lnb_ref[...])
    o_ref[...] = (y * _sigmoid(y)).astype(BF16)


def _conformer(z, w_dw, b_dw, ln_g, ln_b, t_lat, n_rows, tc):
    b = z.shape[0]
    hb = tc // CONV_HALO
    n_tiles = n_rows // tc
    n_halo = z.shape[1] // CONV_HALO
    ca, cg = COL_CA // GROUP_W, COL_CG // GROUP_W
    cur = lambda cblk: pl.BlockSpec((None, tc, GROUP_W), lambda bb, i: (bb, i, cblk))
    prev = lambda cblk: pl.BlockSpec((None, CONV_HALO, GROUP_W),
                                     lambda bb, i: (bb, jnp.maximum(i * hb - 1, 0), cblk))
    nxt = lambda cblk: pl.BlockSpec((None, CONV_HALO, GROUP_W),
                                    lambda bb, i: (bb, jnp.minimum((i + 1) * hb, n_halo - 1), cblk))
    const2 = lambda bb, i: (0, 0)
    seq_starts = (0, t_lat // tc)
    seq_ends = (t_lat // tc - 1, z.shape[1] // tc - 1)
    return pl.pallas_call(
        functools.partial(_conf_kernel, tc=tc, seq_starts=seq_starts, seq_ends=seq_ends),
        grid=(b, n_tiles),
        in_specs=[cur(ca), cur(cg), prev(ca), prev(cg), nxt(ca), nxt(cg),
                  pl.BlockSpec((CONV_W, GROUP_W), const2), pl.BlockSpec((1, GROUP_W), const2),
                  pl.BlockSpec((1, GROUP_W), const2), pl.BlockSpec((1, GROUP_W), const2)],
        out_specs=pl.BlockSpec((None, tc, GROUP_W), lambda bb, i: (bb, i, 0)),
        out_shape=jax.ShapeDtypeStruct((b, z.shape[1], GROUP_W), BF16),
        scratch_shapes=[pltpu.VMEM((tc + 2 * CONV_HALO, GROUP_W), F32),
                        pltpu.VMEM((SUBLANES, tc + 2 * CONV_HALO - SUBLANES, GROUP_W), F32),
                        pltpu.VMEM((tc, GROUP_W), F32)],
        compiler_params=_cparams(("parallel", "parallel")),
        name="conformer_conv",
    )(z, z, z, z, z, z, w_dw, b_dw, ln_g, ln_b)


def _attn_prep_kernel(q_ref, k_ref, v_ref, cos_ref, sin_ref, qg_ref, kg_ref, qo_ref, ko_ref, vo_ref):
    cosf = cos_ref[...]
    sinf = sin_ref[...]

    def norm_rope(x, g, scale):
        y = _rms(x.astype(F32), g)
        return (y * cosf + pltpu.roll(y, HEAD_DIM // 2, 1) * sinf) * scale

    q_scale = HEAD_DIM ** -0.5 * math.log2(math.e)
    for h in range(ATT_Q_HEADS):
        hs = slice(h * HEAD_DIM, (h + 1) * HEAD_DIM)
        qo_ref[:, hs] = norm_rope(q_ref[:, hs], qg_ref[...], q_scale).astype(BF16)
    ones = jnp.ones((v_ref.shape[0], HEAD_DIM), BF16)
    for h in range(ATT_KV_HEADS):
        hs = slice(h * HEAD_DIM, (h + 1) * HEAD_DIM)
        ko_ref[:, hs] = norm_rope(k_ref[:, hs], kg_ref[...], 1.0).astype(BF16)
        vo_ref[:, 2 * h * HEAD_DIM:(2 * h + 1) * HEAD_DIM] = v_ref[:, hs]
        vo_ref[:, (2 * h + 1) * HEAD_DIM:(2 * h + 2) * HEAD_DIM] = ones


def _attn_prep(z, cosf, sinf, qg, kg, tr=256):
    b, tt, _ = z.shape
    qw, kw = ATT_Q_HEADS * HEAD_DIM, ATT_KV_HEADS * HEAD_DIM
    const2 = lambda bb, i: (0, 0)
    return pl.pallas_call(
        _attn_prep_kernel,
        grid=(b, tt // tr),
        in_specs=[pl.BlockSpec((None, tr, qw), lambda bb, i: (bb, i, COL_AQ // qw)),
                  pl.BlockSpec((None, tr, kw), lambda bb, i: (bb, i, COL_AK // kw)),
                  pl.BlockSpec((None, tr, kw), lambda bb, i: (bb, i, COL_AV // kw)),
                  pl.BlockSpec((tr, HEAD_DIM), lambda bb, i: (i, 0)),
                  pl.BlockSpec((tr, HEAD_DIM), lambda bb, i: (i, 0)),
                  pl.BlockSpec((1, HEAD_DIM), const2), pl.BlockSpec((1, HEAD_DIM), const2)],
        out_specs=[pl.BlockSpec((None, tr, qw), lambda bb, i: (bb, i, 0)),
                   pl.BlockSpec((None, tr, kw), lambda bb, i: (bb, i, 0)),
                   pl.BlockSpec((None, tr, 2 * kw), lambda bb, i: (bb, i, 0))],
        out_shape=[jax.ShapeDtypeStruct((b, tt, qw), BF16),
                   jax.ShapeDtypeStruct((b, tt, kw), BF16),
                   jax.ShapeDtypeStruct((b, tt, 2 * kw), BF16)],
        compiler_params=_cparams(("parallel", "parallel")),
        name="attn_prep",
    )(z, z, z, cosf, sinf, qg, kg)


def _attn_kernel(q_ref, k_ref, v_ref, o_ref, *, tq, sub):
    k = k_ref[...]
    vaug = v_ref[...]
    for g in range(ATT_Q_HEADS // ATT_KV_HEADS):
        hs = slice(g * HEAD_DIM, (g + 1) * HEAD_DIM)
        for r in range(tq // sub):
            rs = slice(r * sub, (r + 1) * sub)
            s = lax.dot_general(q_ref[rs, hs], k, (((1,), (1,)), ((), ())), preferred_element_type=F32)
            p = jnp.exp2(s - jnp.max(s, axis=-1, keepdims=True)).astype(BF16)
            res = jnp.dot(p, vaug, preferred_element_type=F32)
            o_ref[rs, hs] = (res[:, :HEAD_DIM] / res[:, HEAD_DIM:]).astype(BF16)


def _attention(qr, kr, vaug, q_row0, n_q, k_row0, n_k, tq):
    b = qr.shape[0]
    gw = (ATT_Q_HEADS // ATT_KV_HEADS) * HEAD_DIM
    assert q_row0 % tq == 0 and n_q % tq == 0 and k_row0 % n_k == 0
    qb, kb = q_row0 // tq, k_row0 // n_k
    return pl.pallas_call(
        functools.partial(_attn_kernel, tq=tq, sub=min(tq, 256)),
        grid=(b, ATT_KV_HEADS, n_q // tq),
        in_specs=[pl.BlockSpec((None, tq, gw), lambda bb, h, i: (bb, i + qb, h)),
                  pl.BlockSpec((None, n_k, HEAD_DIM), lambda bb, h, i: (bb, kb, h)),
                  pl.BlockSpec((None, n_k, 2 * HEAD_DIM), lambda bb, h, i: (bb, kb, h))],
        out_specs=pl.BlockSpec((None, tq, gw), lambda bb, h, i: (bb, i, h)),
        out_shape=jax.ShapeDtypeStruct((b, n_q, ATT_Q_HEADS * HEAD_DIM), BF16),
        compiler_params=_cparams(("parallel", "parallel", "arbitrary")),
        name="attention",
    )(qr, kr, vaug)


def _out_proj_kernel(ya_ref, yb_ref, yc_ref, yd_ref, w_ref, x_ref, gate_ref, pg_ref, o_ref):
    acc = jnp.dot(ya_ref[...], w_ref[0 * GROUP_W:1 * GROUP_W, :], preferred_element_type=F32)
    acc += jnp.dot(yb_ref[...], w_ref[1 * GROUP_W:2 * GROUP_W, :], preferred_element_type=F32)
    acc += jnp.dot(yc_ref[...], w_ref[2 * GROUP_W:3 * GROUP_W, :], preferred_element_type=F32)
    acc += jnp.dot(yd_ref[...], w_ref[3 * GROUP_W:4 * GROUP_W, :], preferred_element_type=F32)
    o_ref[...] = x_ref[...] + gate_ref[...] * _rms(acc, pg_ref[...])


def _out_proj(ys, y_row_offs, w_out, xs, gate, post_g, tm):
    b, t, d = xs.shape
    batched = gate.shape[0] > 1
    mod_map = (lambda bb, i: (bb, 0, 0)) if batched else (lambda bb, i: (0, 0, 0))
    assert all(off % tm == 0 for off in y_row_offs)
    yspecs = [pl.BlockSpec((None, tm, GROUP_W), functools.partial(lambda bb, i, o: (bb, i + o, 0), o=off // tm))
              for off in y_row_offs]
    return pl.pallas_call(
        _out_proj_kernel,
        grid=(b, t // tm),
        in_specs=[*yspecs,
                  pl.BlockSpec(w_out.shape, lambda bb, i: (0, 0)),
                  pl.BlockSpec((None, tm, d), lambda bb, i: (bb, i, 0)),
                  pl.BlockSpec((None, 1, d), mod_map),
                  pl.BlockSpec((1, d), lambda bb, i: (0, 0))],
        out_specs=pl.BlockSpec((None, tm, d), lambda bb, i: (bb, i, 0)),
        out_shape=jax.ShapeDtypeStruct((b, t, d), F32),
        compiler_params=_cparams(("parallel", "parallel")),
        name="out_proj",
    )(*ys, w_out, xs, gate, post_g)


def _ffn_kernel(x_ref, xp_ref, xn_ref, sh_ref, sc_ref, gate_ref, pre_g_ref, post_g_ref, wg_ref, wu_ref,
                cw_ref, cb_ref, wd_ref, o_ref, h_scr, acc_scr, *, tm, n_tiles, n_ff, seq_len):
    i = pl.program_id(1)
    j = pl.program_id(2)
    H = FFN_HALO

    @pl.when(j == 0)
    def _():
        def pre(x):
            return _rms(x, pre_g_ref[...]) * (1.0 + sc_ref[...]) + sh_ref[...]
        h_scr[H:H + tm, :] = pre(x_ref[...]).astype(BF16)
        h_scr[0:H, :] = jnp.where(i > 0, pre(xp_ref[...]), 0.0).astype(BF16)
        h_scr[H + tm:, :] = jnp.where(i < n_tiles - 1, pre(xn_ref[...]), 0.0).astype(BF16)
        acc_scr[...] = jnp.zeros_like(acc_scr)

    n_ext = tm + 2 * H
    g_ext = jnp.dot(h_scr[...], wg_ref[...], preferred_element_type=F32)
    u = jnp.dot(h_scr[H:H + tm, :], wu_ref[...], preferred_element_type=F32)
    g_prev = pltpu.roll(g_ext, 1, 0)[H:H + tm]
    g_next = pltpu.roll(g_ext, n_ext - 1, 0)[H:H + tm]
    if seq_len < tm:
        pos = lax.broadcasted_iota(jnp.int32, (tm, 1), 0) % seq_len
        g_prev = jnp.where(pos == 0, 0.0, g_prev)
        g_next = jnp.where(pos == seq_len - 1, 0.0, g_next)
    conv =g_prev * cw_ref[0:1, :] + g_ext[H:H + tm] * cw_ref[1:2, :] + g_next * cw_ref[2:3, :] + cb_ref[...]
    act = (conv * _sigmoid(conv)) * u
    acc_scr[...] += jnp.dot(act.astype(BF16), wd_ref[...], preferred_element_type=F32)

    @pl.when(j == n_ff - 1)
    def _():
        o_ref[...] = x_ref[...] + gate_ref[...] * _rms(acc_scr[...], post_g_ref[...])


def _ffn(xs, shift, scale, gate, pre_g, post_g, w_up, cv_w, cv_b, w_down, tm, seq_len, tf=512):
    b, t, d = xs.shape
    assert seq_len == t or (tm % seq_len == 0 and t % tm == 0)
    d_ff = w_down.shape[0]
    n_ff = d_ff // tf
    n_tiles = t // tm
    hb = tm // FFN_HALO
    n_halo = t // FFN_HALO
    batched = shift.shape[0] > 1
    mod_map = (lambda bb, i, j: (bb, 0, 0)) if batched else (lambda bb, i, j: (0, 0, 0))
    const2 = lambda bb, i, j: (0, 0)
    return pl.pallas_call(
        functools.partial(_ffn_kernel, tm=tm, n_tiles=n_tiles, n_ff=n_ff, seq_len=seq_len),
        grid=(b, n_tiles, n_ff),
        in_specs=[pl.BlockSpec((None, tm, d), lambda bb, i, j: (bb, i, 0)),
                  pl.BlockSpec((None, FFN_HALO, d), lambda bb, i, j: (bb, jnp.maximum(i * hb - 1, 0), 0)),
                  pl.BlockSpec((None, FFN_HALO, d), lambda bb, i, j: (bb, jnp.minimum((i + 1) * hb, n_halo - 1), 0)),
                  pl.BlockSpec((None, 1, d), mod_map), pl.BlockSpec((None, 1, d), mod_map),
                  pl.BlockSpec((None, 1, d), mod_map),
                  pl.BlockSpec((1, d), const2), pl.BlockSpec((1, d), const2),
                  pl.BlockSpec((d, tf), lambda bb, i, j: (0, j)),
                  pl.BlockSpec((d, tf), lambda bb, i, j: (0, j + n_ff)),
                  pl.BlockSpec((cv_w.shape[0], tf), lambda bb, i, j: (0, j)),
                  pl.BlockSpec((1, tf), lambda bb, i, j: (0, j)),
                  pl.BlockSpec((tf, d), lambda bb, i, j: (j, 0))],
        out_specs=pl.BlockSpec((None, tm, d), lambda bb, i, j: (bb, i, 0)),
        out_shape=jax.ShapeDtypeStruct((b, t, d), F32),
        scratch_shapes=[pltpu.VMEM((tm + 2 * FFN_HALO, d), BF16), pltpu.VMEM((tm, d), F32)],
        compiler_params=_cparams(("parallel", "parallel", "arbitrary")),
        name="conv_glu_ffn",
    )(xs, xs, xs, shift, scale, gate, pre_g, post_g, w_up, w_up, cv_w, cv_b, w_down)


def _pack_w_in(w_in):
    d = w_in.shape[0]
    main = jnp.concatenate([w_in[:, :4 * GROUP_W], w_in[:, 4 * GROUP_W + N_GATE:]], axis=1).astype(BF16)
    gates = w_in[:, 4 * GROUP_W:4 * GROUP_W + N_GATE]
    gates = jnp.concatenate([gates, jnp.zeros((d, LANE - N_GATE), w_in.dtype)], axis=1).astype(BF16)
    return main, gates


def _rope_tables(t_lat, t_ctx):
    t = jnp.arange(t_lat)
    row = (t // GRID_W).astype(F32)
    colp = (t % GRID_W).astype(F32)
    axis_dim = HEAD_DIM // 2
    inv = jnp.power(ROPE_THETA, -jnp.arange(0, axis_dim, 2, dtype=F32) / axis_dim)
    ang = jnp.concatenate([row[:, None] * inv, colp[:, None] * inv], axis=-1)
    cos, sin = jnp.cos(ang), jnp.sin(ang)
    cosf = jnp.concatenate([cos, cos], axis=-1)
    sinf = jnp.concatenate([-sin, sin], axis=-1)
    cosf = jnp.concatenate([cosf, jnp.ones((t_ctx, HEAD_DIM), F32)], axis=0)
    sinf = jnp.concatenate([sinf, jnp.zeros((t_ctx, HEAD_DIM), F32)], axis=0)
    return cosf, sinf


def kernel(x, c, ctx, c_ctx, w_ada, b_ada, pre_mix_g, post_mix_g, w_in, ml_gate_b, ml_norm_g, sg_ln_g, sg_ln_b,
           sg_w, sg_b, cv_w, cv_b, cv_ln_g, cv_ln_b, at_qn_g, at_kn_g, w_out, pre_ffn_g, post_ffn_g, w_up,
           ffn_cv_w, ffn_cv_b, w_down):
    b, t_lat, d = x.shape
    t_ctx = ctx.shape[1]
    tt = t_lat + t_ctx
    depth = w_ada.shape[0]
    tm_in = min(1024, t_lat)
    tm_lat = min(512, t_lat)
    assert t_lat % tm_in == 0 and t_lat % t_ctx == 0 and t_ctx % CHUNK == 0 and t_ctx % 256 == 0

    n_rows = -(-(b + 1) // 8) * 8
    s_rows = jnp.concatenate([c, c_ctx[None, :], jnp.zeros((n_rows - b - 1, d), F32)], axis=0)
    mod = _ada(s_rows, w_ada, b_ada)
    cosf, sinf = _rope_tables(t_lat, t_ctx)
    row2 = lambda a: a.reshape(1, -1)

    xc = ctx
    for i in range(depth):
        need_ctx = i < depth - 1
        mod_l = [m[:, None, :] for m in jnp.split(mod[i, :b], 6, axis=-1)]
        mod_c = [m[:, None, :] for m in jnp.split(mod[i, b:b + 1], 6, axis=-1)]
        sh1, sc1, g1, sh2, sc2, g2 = mod_l
        sh1c, sc1c, g1c, sh2c, sc2c, g2c = mod_c

        w_main, w_gate = _pack_w_in(w_in[i])
        bufs = _in_proj(x, sh1, sc1, row2(pre_mix_g[i]), w_main, w_gate, tt, 0, tm_in)
        z, zg = _in_proj(xc, sh1c, sc1c, row2(pre_mix_g[i]), w_main, w_gate, tt, t_lat, t_ctx, bufs=bufs)

        gate_b = jnp.concatenate([ml_gate_b[i], jnp.zeros((LANE - N_GATE,), F32)]).reshape(1, LANE)
        ya = _mlstm(z, zg, gate_b, row2(ml_norm_g[i]), t_lat, t_ctx)

        mix_rows = tt if need_ctx else t_lat
        sg_tr = 2176 if (need_ctx and tt % 2176 == 0) else (2048 if t_lat % 2048 == 0 else 256)
        sg_tr = sg_tr if mix_rows % sg_tr == 0 else 256
        b_s_full = jnp.broadcast_to(sg_b[i][:, :, None], (SG_GROUPS, CHUNK, LANE))
        yb = _spatial_gating(z, row2(sg_ln_g[i]), row2(sg_ln_b[i]), sg_w[i].astype(BF16), b_s_full, mix_rows, sg_tr)

        yc = _conformer(z, cv_w[i], row2(cv_b[i]), row2(cv_ln_g[i]), row2(cv_ln_b[i]), t_lat, mix_rows, 256)

        qr, kr, vaug = _attn_prep(z, cosf, sinf, row2(at_qn_g[i]), row2(at_kn_g[i]))
        yd = _attention(qr, kr, vaug, 0, t_lat, 0, tt, min(512, t_lat))

        w_out_b = w_out[i].astype(BF16)
        w_up_b = w_up[i].astype(BF16)
        w_down_b = w_down[i].astype(BF16)
        x = _out_proj((ya, yb, yc, yd), (0, 0, 0, 0), w_out_b, x, g1, row2(post_mix_g[i]), tm_lat)
        x = _ffn(x, sh2, sc2, g2, row2(pre_ffn_g[i]), row2(post_ffn_g[i]), w_up_b, ffn_cv_w[i],
                 row2(ffn_cv_b[i]), w_down_b, tm_lat, t_lat)
        if need_ctx:
            yd_c = _attention(qr, kr, vaug, t_lat, t_ctx, t_lat, t_ctx, t_ctx)
            xc = _out_proj((ya, yb, yc, yd_c), (t_lat, t_lat, t_lat, 0), w_out_b, xc, g1c, row2(post_mix_g[i]), t_ctx)
            tm_ctx = t_ctx * max(1, min(b, 512 // t_ctx))
            tm_ctx = tm_ctx if (b * t_ctx) % tm_ctx == 0 else t_ctx
            xc = _ffn(xc.reshape(1, b * t_ctx, d), sh2c, sc2c, g2c, row2(pre_ffn_g[i]), row2(post_ffn_g[i]), w_up_b,
                      ffn_cv_w[i], row2(ffn_cv_b[i]), w_down_b, tm_ctx, t_ctx).reshape(b, t_ctx, d)
    return x
```

```python
import functools
import math

import jax
import jax.numpy as jnp
from jax import lax
from jax.experimental import pallas as pl
from jax.experimental.pallas import tpu as pltpu

F32 = jnp.float32
BF16 = jnp.bfloat16

EPS = 1e-6
LANE = 128
SUBLANES = 8
HEAD_DIM = 128
GROUP_W = 512
ML_HEADS = 4
CHUNK = 128
SG_GROUPS = 4
CONV_W = 31
CONV_HALO = 16
ATT_Q_HEADS = 4
ATT_KV_HEADS = 2
GRID_W = 64
ROPE_THETA = 10000.0
FFN_HALO = 16
N_GATE = 16
GATE_ROWS = 32

COL_MQ, COL_MK, COL_MV, COL_MO = 0, 512, 1024, 1536
COL_SU, COL_SV = 2048, 2560
COL_CA, COL_CG = 3072, 3584
COL_AQ, COL_AK, COL_AV = 4096, 4608, 4864
NZ = 5120

VMEM_LIMIT = 52 * 1024 * 1024


def _cparams(sem):
    return pltpu.CompilerParams(dimension_semantics=sem, vmem_limit_bytes=VMEM_LIMIT)


def _rms(x, g):
    ms = jnp.mean(x * x, axis=-1, keepdims=True)
    return x * lax.rsqrt(ms + EPS) * g


def _layernorm(x, g, b):
    mu = jnp.mean(x, axis=-1, keepdims=True)
    d = x - mu
    var = jnp.mean(d * d, axis=-1, keepdims=True)
    return d * lax.rsqrt(var + EPS) * g + b


def _sigmoid(x):
    return 1.0 / (1.0 + jnp.exp(-x))


def _gelu_tanh(x):
    c = math.sqrt(2.0 / math.pi)
    return x * (0.5 * (1.0 + jnp.tanh(c * (x + 0.044715 * (x * x * x)))))


def _log_sigmoid(x):
    return jnp.minimum(x, 0.0) - jnp.log1p(jnp.exp(-jnp.abs(x)))


def _ada_kernel(s_ref, w_ref, b_ref, o_ref):
    s = s_ref[...]
    s = s * _sigmoid(s)
    o_ref[...] = jnp.dot(s.astype(BF16), w_ref[...].astype(BF16), preferred_element_type=F32) + b_ref[...]


def _ada(s_rows, w_ada, b_ada, tn=1024):
    depth, d, n = w_ada.shape
    r = s_rows.shape[0]
    return pl.pallas_call(
        _ada_kernel,
        grid=(depth, n // tn),
        in_specs=[pl.BlockSpec((r, d), lambda l, j: (0, 0)),
                  pl.BlockSpec((None, d, tn), lambda l, j: (l, 0, j)),
                  pl.BlockSpec((None, 1, tn), lambda l, j: (l, 0, j))],
        out_specs=pl.BlockSpec((None, r, tn), lambda l, j: (l, 0, j)),
        out_shape=jax.ShapeDtypeStruct((depth, r, n), F32),
        compiler_params=_cparams(("parallel", "parallel")),
        name="ada_mod",
    )(s_rows, w_ada, b_ada.reshape(depth, 1, n))


def _in_proj_kernel(x_ref, xc_ref, sh_ref, sc_ref, shc_ref, scc_ref, g_ref, w_ref, wg_ref, o_ref, og_ref, h_scr,
                    *, tm, t_ctx, n_lat_tiles, rows_per_step):
    i = pl.program_id(1)
    first_col = pl.program_id(2) == 0
    is_lat = i < n_lat_tiles

    def pre(x, sh, sc):
        return (_rms(x, g_ref[...]) * (1.0 + sc) + sh).astype(BF16)

    @pl.when(jnp.logical_and(first_col, is_lat))
    def _():
        def body(r, carry):
            sl = pl.ds(pl.multiple_of(r * rows_per_step, rows_per_step), rows_per_step)
            h_scr[sl, :] = pre(x_ref[sl, :], sh_ref[...], sc_ref[...])
            return carry
        lax.fori_loop(0, tm // rows_per_step, body, 0)
        og_ref[...] = jnp.dot(h_scr[...], wg_ref[...], preferred_element_type=F32)

    @pl.when(jnp.logical_and(first_col, jnp.logical_not(is_lat)))
    def _():
        h_scr[0:t_ctx, :] = pre(xc_ref[...], shc_ref[...], scc_ref[...])
        og_ref[0:t_ctx, :] = jnp.dot(h_scr[0:t_ctx, :], wg_ref[...], preferred_element_type=F32)
        if t_ctx < tm:
            og_ref[t_ctx:, :] = jnp.zeros((tm - t_ctx, og_ref.shape[1]), F32)

    @pl.when(is_lat)
    def _():
        o_ref[...] = jnp.dot(h_scr[...], w_ref[...], preferred_element_type=F32).astype(BF16)

    @pl.when(jnp.logical_not(is_lat))
    def _():
        o_ref[0:t_ctx, :] = jnp.dot(h_scr[0:t_ctx, :], w_ref[...], preferred_element_type=F32).astype(BF16)
        if t_ctx < tm:
            o_ref[t_ctx:, :] = jnp.zeros((tm - t_ctx, o_ref.shape[1]), BF16)


def _in_proj(x, xc, mod_l, mod_c, g, w, w_gate, tm, tn=1280):
    b, t_lat, d = x.shape
    t_ctx = xc.shape[1]
    tt = t_lat + t_ctx
    nz = w.shape[1]
    n_lat_tiles = t_lat // tm
    assert t_lat % tm == 0 and t_ctx <= tm
    lat_map = lambda bb, i, j: (bb, 0, 0)
    ctx_map = lambda bb, i, j: (0, 0, 0)
    return pl.pallas_call(
        functools.partial(_in_proj_kernel, tm=tm, t_ctx=t_ctx, n_lat_tiles=n_lat_tiles, rows_per_step=min(tm, 256)),
        grid=(b, n_lat_tiles + 1, nz // tn),
        in_specs=[pl.BlockSpec((None, tm, d), lambda bb, i, j: (bb, jnp.minimum(i, n_lat_tiles - 1), 0)),
                  pl.BlockSpec((None, t_ctx, d), lambda bb, i, j: (bb, 0, 0)),
                  pl.BlockSpec((None, 1, d), lat_map), pl.BlockSpec((None, 1, d), lat_map),
                  pl.BlockSpec((None, 1, d), ctx_map), pl.BlockSpec((None, 1, d), ctx_map),
                  pl.BlockSpec((1, d), lambda bb, i, j: (0, 0)),
                  pl.BlockSpec((d, tn), lambda bb, i, j: (0, j)),
                  pl.BlockSpec((d, LANE), lambda bb, i, j: (0, 0))],
        out_specs=[pl.BlockSpec((None, tm, tn), lambda bb, i, j: (bb, i, j)),
                   pl.BlockSpec((None, tm, LANE), lambda bb, i, j: (bb, i, 0))],
        out_shape=[jax.ShapeDtypeStruct((b, tt, nz), BF16), jax.ShapeDtypeStruct((b, tt, LANE), F32)],
        scratch_shapes=[pltpu.VMEM((tm, d), BF16)],
        compiler_params=_cparams(("parallel", "parallel", "arbitrary")),
        name="in_proj",
    )(x, xc, mod_l[0], mod_l[1], mod_c[0], mod_c[1], g, w, w_gate)


def _prefix_lanes(x, op, fill):
    lane = lax.broadcasted_iota(jnp.int32, x.shape, 1)
    sh = 1
    while sh < x.shape[1]:
        x = op(x, jnp.where(lane >= sh, pltpu.roll(x, sh, 1), fill))
        sh *= 2
    return x


def _suffix_lanes(x, op, fill):
    n = x.shape[1]
    lane = lax.broadcasted_iota(jnp.int32, x.shape, 1)
    sh = 1
    while sh < n:
        x = op(x, jnp.where(lane < n - sh, pltpu.roll(x, n - sh, 1), fill))
        sh *= 2
    return x


def _mlstm_kernel(q_ref, k_ref, v_ref, o_ref, gt_ref, gb_ref, ng_ref, out_ref,
                  rb_scr, gr_scr, vt_scr, u_scr, cp_scr, st_scr, sa_scr, sc_scr, sm_scr, *, n_lat, n_ctx):
    L = CHUNK
    nc = n_lat + n_ctx
    VT_ROWS = vt_scr.shape[1]
    head = pl.program_id(1)
    row = lax.broadcasted_iota(jnp.int32, (L, L), 0)
    col = lax.broadcasted_iota(jnp.int32, (L, L), 1)
    lower = col <= row
    upper = col >= row
    k_scale = HEAD_DIM ** -0.5
    HALF = GATE_ROWS // 2

    def rows_of(c):
        return pl.ds(pl.multiple_of(c * L, L), L)

    def step_of(c, direction):
        if direction == 0:
            return jnp.where(c < n_lat, c + n_ctx, c - n_lat)
        return nc - 1 - c

    @pl.when(head == 0)
    def _():
        lower_f = lower.astype(F32)
        upper_f = upper.astype(F32)
        n_all = nc * HALF
        rg = lax.broadcasted_iota(jnp.int32, (n_all, L), 0) % HALF
        is_cum = (rg % 8) >= 4
        is_bwd = rg >= 8

        def to_rows(c, carry):
            gr_scr[c, 0:HALF, :] = (gt_ref[rows_of(c), :] + gb_ref[...]).T[0:HALF, :]
            return carry

        lax.fori_loop(0, nc, to_rows, 0, unroll=2)

        G = gr_scr[:, 0:HALF, :].reshape(n_all, L)
        Gf = _log_sigmoid(G)
        Bf = jnp.dot(Gf, upper_f, precision=lax.Precision.HIGHEST, preferred_element_type=F32)
        Bb = jnp.dot(Gf, lower_f, precision=lax.Precision.HIGHEST, preferred_element_type=F32)
        Bsel = jnp.where(is_bwd, Bb, Bf)
        R = G - pltpu.roll(Bsel, n_all - 4, 0)
        CM = jnp.where(is_bwd, _suffix_lanes(R, jnp.maximum, -jnp.inf), _prefix_lanes(R, jnp.maximum, -jnp.inf))
        gr_scr[:, 0:HALF, :] = jnp.where(is_cum, Bsel, G).reshape(nc, HALF, L)
        gr_scr[:, HALF:GATE_ROWS, :] = jnp.where(is_cum, pltpu.roll(CM, 4, 0), R).reshape(nc, HALF, L)

    row_g = lax.broadcasted_iota(jnp.int32, (GATE_ROWS, L), 0)
    lane_r = lax.broadcasted_iota(jnp.int32, (1, LANE), 1)

    def gate_row(x, idx):
        return jnp.sum(jnp.where(row_g == idx, x, 0.0), axis=0, keepdims=True)

    st_scr[...] = jnp.zeros_like(st_scr)
    ones_rows = jnp.ones((VT_ROWS - HEAD_DIM, L), F32)

    def summarize(c, carry):
        vt = jnp.concatenate([v_ref[rows_of(c), :].astype(F32).T, ones_rows], axis=0)
        vt_scr[c] = vt.astype(BF16)
        k = k_ref[rows_of(c), :]
        grv = gr_scr[c]
        for direction in (0, 1):
            ci = direction * 8 + head
            b_row = gate_row(grv, ci + 4)
            i_row = gate_row(grv, ci)
            total = b_row[:, L - 1:L] if direction == 0 else b_row[:, 0:1]
            g_row = total - b_row + i_row
            mg = jnp.max(g_row, axis=1, keepdims=True)
            w_row = jnp.exp(g_row - mg) * k_scale
            step = step_of(c, direction)
            u_scr[direction * nc + step] = jnp.dot((vt * w_row).astype(BF16), k, preferred_element_type=F32)
            st_scr[direction, pl.ds(step, 1), :] = jnp.where(lane_r < 64, total, mg)
            rb_scr[direction * nc + c] = jnp.broadcast_to(gate_row(grv, 16 + ci), (L, L)).T
        return carry

    lax.fori_loop(0, nc, summarize, 0, unroll=4)

    lane8 = lax.broadcasted_iota(jnp.int32, (8, LANE), 1)
    for direction in (0, 1):
        xt = st_scr[direction].T
        valid = lane8 < nc
        T = jnp.where(valid, jnp.broadcast_to(xt[0:1, :], (8, LANE)), 0.0)
        G = jnp.where(valid, jnp.broadcast_to(xt[64:65, :], (8, LANE)), -jnp.inf)
        PT = _prefix_lanes(T, jnp.add, 0.0)
        mvec = PT + jnp.maximum(_prefix_lanes(G - PT, jnp.maximum, -jnp.inf), 0.0)
        mprev = jnp.where(lane8 >= 1, pltpu.roll(mvec, 1, 1), 0.0)
        a = jnp.exp(T + mprev - mvec)
        c2 = jnp.exp(G - mvec)
        sa_scr[direction] = jnp.broadcast_to(a[0:1, :], (LANE, LANE)).T
        sc_scr[direction] = jnp.broadcast_to(c2[0:1, :], (LANE, LANE)).T
        sm_scr[direction] = jnp.broadcast_to(mprev[0:1, :], (LANE, LANE)).T

    def scan(s, carry):
        new = []
        for direction in (0, 1):
            ct = carry[direction]
            idx = direction * nc + s
            cp_scr[idx] = ct.astype(BF16)
            a = sa_scr[direction, pl.ds(s, 1), :]
            c2 = sc_scr[direction, pl.ds(s, 1), :]
            new.append(a * ct + c2 * u_scr[idx])
        return tuple(new)

    zero = jnp.zeros((VT_ROWS, HEAD_DIM), F32)
    lax.fori_loop(0, nc, scan, (zero, zero))

    nt_dims = (((1,), (1,)), ((), ()))

    def emit(c, carry):
        q = q_ref[rows_of(c), :]
        vt = vt_scr[c]
        grv = gr_scr[c]
        qk_t = lax.dot_general(k_ref[rows_of(c), :], q, nt_dims, preferred_element_type=F32) * k_scale
        h_t = jnp.zeros((HEAD_DIM, L), F32)
        for direction in (0, 1):
            ci = direction * 8 + head
            step = step_of(c, direction)
            r_col = rb_scr[direction * nc + c]
            cm_row = gate_row(grv, 16 + ci + 4)
            b_row = gate_row(grv, ci + 4)
            p_t = jnp.exp(jnp.where(upper if direction == 0 else lower, r_col - cm_row, -jnp.inf))
            intra = jnp.dot(vt, (qk_t * p_t).astype(BF16), preferred_element_type=F32)
            inter = lax.dot_general(cp_scr[direction * nc + step], q, nt_dims, preferred_element_type=F32)
            mp = sm_scr[direction, pl.ds(step, 1), :]
            mx = jnp.maximum(mp, cm_row)
            e1 = jnp.exp(mp - mx)
            e2 = jnp.exp(cm_row - mx)
            num = e1 * inter[:HEAD_DIM] + e2 * intra[:HEAD_DIM]
            den = e1 * inter[HEAD_DIM:HEAD_DIM + 1] + e2 * intra[HEAD_DIM:HEAD_DIM + 1]
            h_t = h_t + num * (1.0 / jnp.maximum(jnp.abs(den), jnp.exp(-(b_row + mx))))
        mu = jnp.mean(h_t, axis=0, keepdims=True)
        dev = h_t - mu
        var = jnp.mean(dev * dev, axis=0, keepdims=True)
        hn = (dev * lax.rsqrt(var + EPS)).T * ng_ref[...]
        out_ref[rows_of(c), :] = (hn * _sigmoid(o_ref[rows_of(c), :].astype(F32))).astype(BF16)
        return carry

    lax.fori_loop(0, nc, emit, 0, unroll=4)


def _mlstm(z, zg, gate_b, norm_g, t_lat, t_ctx):
    b, tt, _ = z.shape
    nc = tt // CHUNK
    assert nc <= 64
    vt_rows = HEAD_DIM + 16
    blk = lambda off: pl.BlockSpec((None, tt, LANE), lambda bb, h: (bb, 0, off // LANE + h))
    return pl.pallas_call(
        functools.partial(_mlstm_kernel, n_lat=t_lat // CHUNK, n_ctx=t_ctx // CHUNK),
        grid=(b, ML_HEADS),
        in_specs=[blk(COL_MQ), blk(COL_MK), blk(COL_MV), blk(COL_MO),
                  pl.BlockSpec((None, tt, LANE), lambda bb, h: (bb, 0, 0)),
                  pl.BlockSpec((1, LANE), lambda bb, h: (0, 0)),
                  pl.BlockSpec((1, LANE), lambda bb, h: (0, h))],
        out_specs=pl.BlockSpec((None, tt, LANE), lambda bb, h: (bb, 0, h)),
        out_shape=jax.ShapeDtypeStruct((b, tt, GROUP_W), BF16),
        scratch_shapes=[pltpu.VMEM((2 * nc, CHUNK, CHUNK), F32),
                        pltpu.VMEM((nc, GATE_ROWS, LANE), F32),
                        pltpu.VMEM((nc, vt_rows, CHUNK), BF16),
                        pltpu.VMEM((2 * nc, vt_rows, HEAD_DIM), F32),
                        pltpu.VMEM((2 * nc, vt_rows, HEAD_DIM), BF16),
                        pltpu.VMEM((2, LANE, LANE), F32),
                        pltpu.VMEM((2, LANE, LANE), F32),
                        pltpu.VMEM((2, LANE, LANE), F32),
                        pltpu.VMEM((2, LANE, LANE), F32)],
        compiler_params=_cparams(("parallel", "arbitrary")),
        name="mlstm",
    )(z, z, z, z, zg, gate_b, norm_g)


def _sg_kernel(u_ref, v_ref, lng_ref, lnb_ref, w_ref, bs_ref, o_ref, *, n_chunks):
    def body(c, carry):
        sl = pl.ds(pl.multiple_of(c * CHUNK, CHUNK), CHUNK)
        u = _gelu_tanh(u_ref[sl, :].astype(F32))
        v = _layernorm(_gelu_tanh(v_ref[sl, :].astype(F32)), lng_ref[...], lnb_ref[...]).astype(BF16)
        for g in range(SG_GROUPS):
            gs = slice(g * LANE, (g + 1) * LANE)
            mixed = jnp.dot(w_ref[g], v[:, gs], preferred_element_type=F32) + bs_ref[g]
            o_ref[sl, gs] = (u[:, gs] * mixed).astype(BF16)
        return carry

    lax.fori_loop(0, n_chunks, body, 0)


def _spatial_gating(z, ln_g, ln_b, w_s, b_s_full, n_rows, tr):
    b = z.shape[0]
    blk = lambda off: pl.BlockSpec((None, tr, GROUP_W), lambda bb, i: (bb, i, off // GROUP_W))
    const2 = lambda bb, i: (0, 0)
    const3 = lambda bb, i: (0, 0, 0)
    return pl.pallas_call(
        functools.partial(_sg_kernel, n_chunks=tr // CHUNK),
        grid=(b, n_rows // tr),
        in_specs=[blk(COL_SU), blk(COL_SV),
                  pl.BlockSpec((1, GROUP_W), const2), pl.BlockSpec((1, GROUP_W), const2),
                  pl.BlockSpec((SG_GROUPS, CHUNK, CHUNK), const3),
                  pl.BlockSpec((SG_GROUPS, CHUNK, LANE), const3)],
        out_specs=pl.BlockSpec((None, tr, GROUP_W), lambda bb, i: (bb, i, 0)),
        out_shape=jax.ShapeDtypeStruct((b, n_rows, GROUP_W), BF16),
        compiler_params=_cparams(("parallel", "parallel")),
        name="spatial_gating",
    )(z, z, ln_g, ln_b, w_s, b_s_full)


def _conf_kernel(a_ref, g_ref, ap_ref, gp_ref, an_ref, gn_ref, w_ref, b_ref, lng_ref, lnb_ref, o_ref,
                 y_scr, ysh_scr, cv_scr, *, tc, seq_starts, seq_ends):
    i = pl.program_id(1)
    has_prev = functools.reduce(jnp.logical_and, [i != s for s in seq_starts])
    has_next = functools.reduce(jnp.logical_and, [i != e for e in seq_ends])
    H = CONV_HALO

    def glu(a, g):
        return a[...].astype(F32) * _sigmoid(g[...].astype(F32))

    y_scr[H:H + tc, :] = glu(a_ref, g_ref)
    y_scr[0:H, :] = jnp.where(has_prev, glu(ap_ref, gp_ref), 0.0)
    y_scr[H + tc:, :] = jnp.where(has_next, glu(an_ref, gn_ref), 0.0)
    n_sh = ysh_scr.shape[1]
    for j in range(SUBLANES):
        ysh_scr[j] = y_scr[j:j + n_sh, :]
    rb = 64
    base = H - (CONV_W // 2)
    for cb in range(GROUP_W // LANE):
        cs = slice(cb * LANE, (cb + 1) * LANE)
        for r in range(tc // rb):
            acc = jnp.zeros((rb, LANE), F32)
            for k in range(CONV_W):
                a, j = divmod(base + k, SUBLANES)
                acc = acc + ysh_scr[j, r * rb + SUBLANES * a:r * rb + SUBLANES * a + rb, cs] * w_ref[k:k + 1, cs]
            cv_scr[r * rb:(r + 1) * rb, cs] = acc
    y = _layernorm(cv_scr[...] + b_ref[...], lng_ref[...], lnb_ref[...])
    o_ref[...] = (y * _sigmoid(y)).astype(BF16)


def _conformer(z, w_dw, b_dw, ln_g, ln_b, t_lat, n_rows, tc):
    b = z.shape[0]
    hb = tc // CONV_HALO
    n_tiles = n_rows // tc
    n_halo = z.shape[1] // CONV_HALO
    ca, cg = COL_CA // GROUP_W, COL_CG // GROUP_W
    cur = lambda cblk: pl.BlockSpec((None, tc, GROUP_W), lambda bb, i: (bb, i, cblk))
    prev = lambda cblk: pl.BlockSpec((None, CONV_HALO, GROUP_W),
                                     lambda bb, i: (bb, jnp.maximum(i * hb - 1, 0), cblk))
    nxt = lambda cblk: pl.BlockSpec((None, CONV_HALO, GROUP_W),
                                    lambda bb, i: (bb, jnp.minimum((i + 1) * hb, n_halo - 1), cblk))
    const2 = lambda bb, i: (0, 0)
    seq_starts = (0, t_lat // tc)
    seq_ends = (t_lat // tc - 1, z.shape[1] // tc - 1)
    return pl.pallas_call(
        functools.partial(_conf_kernel, tc=tc, seq_starts=seq_starts, seq_ends=seq_ends),
        grid=(b, n_tiles),
        in_specs=[cur(ca), cur(cg), prev(ca), prev(cg), nxt(ca), nxt(cg),
                  pl.BlockSpec((CONV_W, GROUP_W), const2), pl.BlockSpec((1, GROUP_W), const2),
                  pl.BlockSpec((1, GROUP_W), const2), pl.BlockSpec((1, GROUP_W), const2)],
        out_specs=pl.BlockSpec((None, tc, GROUP_W), lambda bb, i: (bb, i, 0)),
        out_shape=jax.ShapeDtypeStruct((b, n_rows, GROUP_W), BF16),
        scratch_shapes=[pltpu.VMEM((tc + 2 * CONV_HALO, GROUP_W), F32),
                        pltpu.VMEM((SUBLANES, tc + 2 * CONV_HALO - SUBLANES, GROUP_W), F32),
                        pltpu.VMEM((tc, GROUP_W), F32)],
        compiler_params=_cparams(("parallel", "parallel")),
        name="conformer_conv",
    )(z, z, z, z, z, z, w_dw, b_dw, ln_g, ln_b)


def _attn_prep_kernel(q_ref, k_ref, v_ref, cos_ref, sin_ref, qg_ref, kg_ref, qo_ref, ko_ref, vo_ref):
    cosf = cos_ref[...]
    sinf = sin_ref[...]

    def norm_rope(x, g, scale):
        y = _rms(x.astype(F32), g)
        return (y * cosf + pltpu.roll(y, HEAD_DIM // 2, 1) * sinf) * scale

    q_scale = HEAD_DIM ** -0.5 * math.log2(math.e)
    for h in range(ATT_Q_HEADS):
        hs = slice(h * HEAD_DIM, (h + 1) * HEAD_DIM)
        qo_ref[:, hs] = norm_rope(q_ref[:, hs], qg_ref[...], q_scale).astype(BF16)
    ones = jnp.ones((v_ref.shape[0], HEAD_DIM), BF16)
    for h in range(ATT_KV_HEADS):
        hs = slice(h * HEAD_DIM, (h + 1) * HEAD_DIM)
        ko_ref[:, hs] = norm_rope(k_ref[:, hs], kg_ref[...], 1.0).astype(BF16)
        vo_ref[:, 2 * h * HEAD_DIM:(2 * h + 1) * HEAD_DIM] = v_ref[:, hs]
        vo_ref[:, (2 * h + 1) * HEAD_DIM:(2 * h + 2) * HEAD_DIM] = ones


def _attn_prep(z, cosf, sinf, qg, kg, tr=256):
    b, tt, _ = z.shape
    qw, kw = ATT_Q_HEADS * HEAD_DIM, ATT_KV_HEADS * HEAD_DIM
    const2 = lambda bb, i: (0, 0)
    return pl.pallas_call(
        _attn_prep_kernel,
        grid=(b, tt // tr),
        in_specs=[pl.BlockSpec((None, tr, qw), lambda bb, i: (bb, i, COL_AQ // qw)),
                  pl.BlockSpec((None, tr, kw), lambda bb, i: (bb, i, COL_AK // kw)),
                  pl.BlockSpec((None, tr, kw), lambda bb, i: (bb, i, COL_AV // kw)),
                  pl.BlockSpec((tr, HEAD_DIM), lambda bb, i: (i, 0)),
                  pl.BlockSpec((tr, HEAD_DIM), lambda bb, i: (i, 0)),
                  pl.BlockSpec((1, HEAD_DIM), const2), pl.BlockSpec((1, HEAD_DIM), const2)],
        out_specs=[pl.BlockSpec((None, tr, qw), lambda bb, i: (bb, i, 0)),
                   pl.BlockSpec((None, tr, kw), lambda bb, i: (bb, i, 0)),
                   pl.BlockSpec((None, tr, 2 * kw), lambda bb, i: (bb, i, 0))],
        out_shape=[jax.ShapeDtypeStruct((b, tt, qw), BF16),
                   jax.ShapeDtypeStruct((b, tt, kw), BF16),
                   jax.ShapeDtypeStruct((b, tt, 2 * kw), BF16)],
        compiler_params=_cparams(("parallel", "parallel")),
        name="attn_prep",
    )(z, z, z, cosf, sinf, qg, kg)


def _attn_kernel(q_ref, k_ref, v_ref, o_ref, *, tq, sub):
    k = k_ref[...]
    vaug = v_ref[...]
    for g in range(ATT_Q_HEADS // ATT_KV_HEADS):
        hs = slice(g * HEAD_DIM, (g + 1) * HEAD_DIM)
        for r in range(tq // sub):
            rs = slice(r * sub, (r + 1) * sub)
            s = lax.dot_general(q_ref[rs, hs], k, (((1,), (1,)), ((), ())), preferred_element_type=F32)
            p = jnp.exp2(s - jnp.max(s, axis=-1, keepdims=True)).astype(BF16)
            res = jnp.dot(p, vaug, preferred_element_type=F32)
            o_ref[rs, hs] = (res[:, :HEAD_DIM] / res[:, HEAD_DIM:]).astype(BF16)


def _attention(qr, kr, vaug, q_row0, n_q, k_row0, n_k, tq):
    b = qr.shape[0]
    gw = (ATT_Q_HEADS // ATT_KV_HEADS) * HEAD_DIM
    assert q_row0 % tq == 0 and n_q % tq == 0 and k_row0 % n_k == 0
    qb, kb = q_row0 // tq, k_row0 // n_k
    return pl.pallas_call(
        functools.partial(_attn_kernel, tq=tq, sub=min(tq, 256)),
        grid=(b, ATT_KV_HEADS, n_q // tq),
        in_specs=[pl.BlockSpec((None, tq, gw), lambda bb, h, i: (bb, i + qb, h)),
                  pl.BlockSpec((None, n_k, HEAD_DIM), lambda bb, h, i: (bb, kb, h)),
                  pl.BlockSpec((None, n_k, 2 * HEAD_DIM), lambda bb, h, i: (bb, kb, h))],
        out_specs=pl.BlockSpec((None, tq, gw), lambda bb, h, i: (bb, i, h)),
        out_shape=jax.ShapeDtypeStruct((b, n_q, ATT_Q_HEADS * HEAD_DIM), BF16),
        compiler_params=_cparams(("parallel", "parallel", "arbitrary")),
        name="attention",
    )(qr, kr, vaug)


def _out_proj_kernel(ya_ref, yb_ref, yc_ref, yd_ref, w_ref, x_ref, gate_ref, pg_ref, o_ref):
    acc = jnp.dot(ya_ref[...], w_ref[0 * GROUP_W:1 * GROUP_W, :], preferred_element_type=F32)
    acc += jnp.dot(yb_ref[...], w_ref[1 * GROUP_W:2 * GROUP_W, :], preferred_element_type=F32)
    acc += jnp.dot(yc_ref[...], w_ref[2 * GROUP_W:3 * GROUP_W, :], preferred_element_type=F32)
    acc += jnp.dot(yd_ref[...], w_ref[3 * GROUP_W:4 * GROUP_W, :], preferred_element_type=F32)
    o_ref[...] = x_ref[...] + gate_ref[...] * _rms(acc, pg_ref[...])


def _out_proj(ys, y_row_offs, w_out, xs, gate, post_g, tm):
    b, t, d = xs.shape
    batched = gate.shape[0] > 1
    mod_map = (lambda bb, i: (bb, 0, 0)) if batched else (lambda bb, i: (0, 0, 0))
    assert all(off % tm == 0 for off in y_row_offs)
    yspecs = [pl.BlockSpec((None, tm, GROUP_W), functools.partial(lambda bb, i, o: (bb, i + o, 0), o=off // tm))
              for off in y_row_offs]
    return pl.pallas_call(
        _out_proj_kernel,
        grid=(b, t // tm),
        in_specs=[*yspecs,
                  pl.BlockSpec(w_out.shape, lambda bb, i: (0, 0)),
                  pl.BlockSpec((None, tm, d), lambda bb, i: (bb, i, 0)),
                  pl.BlockSpec((None, 1, d), mod_map),
                  pl.BlockSpec((1, d), lambda bb, i: (0, 0))],
        out_specs=pl.BlockSpec((None, tm, d), lambda bb, i: (bb, i, 0)),
        out_shape=jax.ShapeDtypeStruct((b, t, d), F32),
        compiler_params=_cparams(("parallel", "parallel")),
        name="out_proj",
    )(*ys, w_out, xs, gate, post_g)


def _ffn_kernel(x_ref, xp_ref, xn_ref, sh_ref, sc_ref, gate_ref, pre_g_ref, post_g_ref, wg_ref, wu_ref,
                cw_ref, cb_ref, wd_ref, o_ref, h_scr, acc_scr, *, tm, n_tiles, n_ff, seq_len):
    i = pl.program_id(1)
    j = pl.program_id(2)
    H = FFN_HALO

    @pl.when(j == 0)
    def _():
        def pre(x):
            return _rms(x, pre_g_ref[...]) * (1.0 + sc_ref[...]) + sh_ref[...]
        h_scr[H:H + tm, :] = pre(x_ref[...]).astype(BF16)
        h_scr[0:H, :] = jnp.where(i > 0, pre(xp_ref[...]), 0.0).astype(BF16)
        h_scr[H + tm:, :] = jnp.where(i < n_tiles - 1, pre(xn_ref[...]), 0.0).astype(BF16)
        acc_scr[...] = jnp.zeros_like(acc_scr)

    n_ext = tm + 2 * H
    g_ext = jnp.dot(h_scr[...], wg_ref[...], preferred_element_type=F32)
    u = jnp.dot(h_scr[H:H + tm, :], wu_ref[...], preferred_element_type=F32)
    g_prev = pltpu.roll(g_ext, 1, 0)[H:H + tm]
    g_next = pltpu.roll(g_ext, n_ext - 1, 0)[H:H + tm]
    if seq_len < tm:
        pos = lax.broadcasted_iota(jnp.int32, (tm, 1), 0) % seq_len
        g_prev = jnp.where(pos == 0, 0.0, g_prev)
        g_next = jnp.where(pos == seq_len - 1, 0.0, g_next)
    conv =g_prev * cw_ref[0:1, :] + g_ext[H:H + tm] * cw_ref[1:2, :] + g_next * cw_ref[2:3, :] + cb_ref[...]
    act = (conv * _sigmoid(conv)) * u
    acc_scr[...] += jnp.dot(act.astype(BF16), wd_ref[...], preferred_element_type=F32)

    @pl.when(j == n_ff - 1)
    def _():
        o_ref[...] = x_ref[...] + gate_ref[...] * _rms(acc_scr[...], post_g_ref[...])


def _ffn(xs, shift, scale, gate, pre_g, post_g, w_up, cv_w, cv_b, w_down, tm, seq_len, tf=512):
    b, t, d = xs.shape
    assert seq_len == t or (tm % seq_len == 0 and t % tm == 0)
    d_ff = w_down.shape[0]
    n_ff = d_ff // tf
    n_tiles = t // tm
    hb = tm // FFN_HALO
    n_halo = t // FFN_HALO
    batched = shift.shape[0] > 1
    mod_map = (lambda bb, i, j: (bb, 0, 0)) if batched else (lambda bb, i, j: (0, 0, 0))
    const2 = lambda bb, i, j: (0, 0)
    return pl.pallas_call(
        functools.partial(_ffn_kernel, tm=tm, n_tiles=n_tiles, n_ff=n_ff, seq_len=seq_len),
        grid=(b, n_tiles, n_ff),
        in_specs=[pl.BlockSpec((None, tm, d), lambda bb, i, j: (bb, i, 0)),
                  pl.BlockSpec((None, FFN_HALO, d), lambda bb, i, j: (bb, jnp.maximum(i * hb - 1, 0), 0)),
                  pl.BlockSpec((None, FFN_HALO, d), lambda bb, i, j: (bb, jnp.minimum((i + 1) * hb, n_halo - 1), 0)),
                  pl.BlockSpec((None, 1, d), mod_map), pl.BlockSpec((None, 1, d), mod_map),
                  pl.BlockSpec((None, 1, d), mod_map),
                  pl.BlockSpec((1, d), const2), pl.BlockSpec((1, d), const2),
                  pl.BlockSpec((d, tf), lambda bb, i, j: (0, j)),
                  pl.BlockSpec((d, tf), lambda bb, i, j: (0, j + n_ff)),
                  pl.BlockSpec((cv_w.shape[0], tf), lambda bb, i, j: (0, j)),
                  pl.BlockSpec((1, tf), lambda bb, i, j: (0, j)),
                  pl.BlockSpec((tf, d), lambda bb, i, j: (j, 0))],
        out_specs=pl.BlockSpec((None, tm, d), lambda bb, i, j: (bb, i, 0)),
        out_shape=jax.ShapeDtypeStruct((b, t, d), F32),
        scratch_shapes=[pltpu.VMEM((tm + 2 * FFN_HALO, d), BF16), pltpu.VMEM((tm, d), F32)],
        compiler_params=_cparams(("parallel", "parallel", "arbitrary")),
        name="conv_glu_ffn",
    )(xs, xs, xs, shift, scale, gate, pre_g, post_g, w_up, w_up, cv_w, cv_b, w_down)


def _pack_w_in(w_in):
    d = w_in.shape[0]
    main = jnp.concatenate([w_in[:, :4 * GROUP_W], w_in[:, 4 * GROUP_W + N_GATE:]], axis=1).astype(BF16)
    gates = w_in[:, 4 * GROUP_W:4 * GROUP_W + N_GATE]
    gates = jnp.concatenate([gates, jnp.zeros((d, LANE - N_GATE), w_in.dtype)], axis=1).astype(BF16)
    return main, gates


def _rope_tables(t_lat, t_ctx):
    t = jnp.arange(t_lat)
    row = (t // GRID_W).astype(F32)
    colp = (t % GRID_W).astype(F32)
    axis_dim = HEAD_DIM // 2
    inv = jnp.power(ROPE_THETA, -jnp.arange(0, axis_dim, 2, dtype=F32) / axis_dim)
    ang = jnp.concatenate([row[:, None] * inv, colp[:, None] * inv], axis=-1)
    cos, sin = jnp.cos(ang), jnp.sin(ang)
    cosf = jnp.concatenate([cos, cos], axis=-1)
    sinf = jnp.concatenate([-sin, sin], axis=-1)
    cosf = jnp.concatenate([cosf, jnp.ones((t_ctx, HEAD_DIM), F32)], axis=0)
    sinf = jnp.concatenate([sinf, jnp.zeros((t_ctx, HEAD_DIM), F32)], axis=0)
    return cosf, sinf


def kernel(x, c, ctx, c_ctx, w_ada, b_ada, pre_mix_g, post_mix_g, w_in, ml_gate_b, ml_norm_g, sg_ln_g, sg_ln_b,
           sg_w, sg_b, cv_w, cv_b, cv_ln_g, cv_ln_b, at_qn_g, at_kn_g, w_out, pre_ffn_g, post_ffn_g, w_up,
           ffn_cv_w, ffn_cv_b, w_down):
    b, t_lat, d = x.shape
    t_ctx = ctx.shape[1]
    tt = t_lat + t_ctx
    depth = w_ada.shape[0]
    tm_in = min(1024, t_lat)
    tm_lat = min(512, t_lat)
    assert t_lat % tm_in == 0 and t_lat % t_ctx == 0 and t_ctx % CHUNK == 0 and t_ctx % 256 == 0

    n_rows = -(-(b + 1) // 8) * 8
    s_rows = jnp.concatenate([c, c_ctx[None, :], jnp.zeros((n_rows - b - 1, d), F32)], axis=0)
    mod = _ada(s_rows, w_ada, b_ada)
    cosf, sinf = _rope_tables(t_lat, t_ctx)
    row2 = lambda a: a.reshape(1, -1)

    xc = ctx
    for i in range(depth):
        need_ctx = i < depth - 1
        mod_l = [m[:, None, :] for m in jnp.split(mod[i, :b], 6, axis=-1)]
        mod_c = [m[:, None, :] for m in jnp.split(mod[i, b:b + 1], 6, axis=-1)]
        sh1, sc1, g1, sh2, sc2, g2 = mod_l
        sh1c, sc1c, g1c, sh2c, sc2c, g2c = mod_c

        w_main, w_gate = _pack_w_in(w_in[i])
        z, zg = _in_proj(x, xc, (sh1, sc1), (sh1c, sc1c), row2(pre_mix_g[i]), w_main, w_gate, tm_in)

        gate_b = jnp.concatenate([ml_gate_b[i], jnp.zeros((LANE - N_GATE,), F32)]).reshape(1, LANE)
        ya = _mlstm(z, zg, gate_b, row2(ml_norm_g[i]), t_lat, t_ctx)

        mix_rows = tt if need_ctx else t_lat
        sg_tr = 2176 if (need_ctx and tt % 2176 == 0) else (2048 if t_lat % 2048 == 0 else 256)
        sg_tr = sg_tr if mix_rows % sg_tr == 0 else 256
        b_s_full = jnp.broadcast_to(sg_b[i][:, :, None], (SG_GROUPS, CHUNK, LANE))
        yb = _spatial_gating(z, row2(sg_ln_g[i]), row2(sg_ln_b[i]), sg_w[i].astype(BF16), b_s_full, mix_rows, sg_tr)

        yc = _conformer(z, cv_w[i], row2(cv_b[i]), row2(cv_ln_g[i]), row2(cv_ln_b[i]), t_lat, mix_rows, 256)

        prep_tr = 1088 if tt % 1088 == 0 else 256
        qr, kr, vaug = _attn_prep(z, cosf, sinf, row2(at_qn_g[i]), row2(at_kn_g[i]), prep_tr)
        yd = _attention(qr, kr, vaug, 0, t_lat, 0, tt, min(512, t_lat))

        w_out_b = w_out[i].astype(BF16)
        w_up_b = w_up[i].astype(BF16)
        w_down_b = w_down[i].astype(BF16)
        x = _out_proj((ya, yb, yc, yd), (0, 0, 0, 0), w_out_b, x, g1, row2(post_mix_g[i]), tm_lat)
        x = _ffn(x, sh2, sc2, g2, row2(pre_ffn_g[i]), row2(post_ffn_g[i]), w_up_b, ffn_cv_w[i],
                 row2(ffn_cv_b[i]), w_down_b, tm_lat, t_lat)
        if need_ctx:
            yd_c = _attention(qr, kr, vaug, t_lat, t_ctx, t_lat, t_ctx, t_ctx)
            xc = _out_proj((ya, yb, yc, yd_c), (t_lat, t_lat, t_lat, 0), w_out_b, xc, g1c, row2(post_mix_g[i]), t_ctx)
            tm_ctx = t_ctx * max(1, min(b, 512 // t_ctx))
            tm_ctx = tm_ctx if (b * t_ctx) % tm_ctx == 0 else t_ctx
            xc = _ffn(xc.reshape(1, b * t_ctx, d), sh2c, sc2c, g2c, row2(pre_ffn_g[i]), row2(post_ffn_g[i]), w_up_b,
                      ffn_cv_w[i], row2(ffn_cv_b[i]), w_down_b, tm_ctx, t_ctx).reshape(b, t_ctx, d)
    return x
```

```python
import functools
import math

import jax
import jax.numpy as jnp
from jax import lax
from jax.experimental import pallas as pl
from jax.experimental.pallas import tpu as pltpu

F32 = jnp.float32
BF16 = jnp.bfloat16

EPS = 1e-6
LANE = 128
SUBLANES = 8
HEAD_DIM = 128
GROUP_W = 512
ML_HEADS = 4
CHUNK = 128
SG_GROUPS = 4
CONV_W = 31
CONV_HALO = 16
ATT_Q_HEADS = 4
ATT_KV_HEADS = 2
GRID_W = 64
ROPE_THETA = 10000.0
FFN_HALO = 16
N_GATE = 16
GATE_ROWS = 32

COL_MQ, COL_MK, COL_MV, COL_MO = 0, 512, 1024, 1536
COL_SU, COL_SV = 2048, 2560
COL_CA, COL_CG = 3072, 3584
COL_AQ, COL_AK, COL_AV = 4096, 4608, 4864
NZ = 5120

VMEM_LIMIT = 52 * 1024 * 1024


def _cparams(sem):
    return pltpu.CompilerParams(dimension_semantics=sem, vmem_limit_bytes=VMEM_LIMIT)


def _layer_row(layer, n):
    return pl.BlockSpec((None, 1, n), lambda *_: (layer, 0, 0))


def _mod_spec(layer, k, d, row=None):
    if row is None:
        return pl.BlockSpec((None, None, 1, d), lambda bb, *_: (layer, bb, 0, k))
    return pl.BlockSpec((None, None, 1, d), lambda *_: (layer, row, 0, k))


def _rms(x, g):
    ms = jnp.mean(x * x, axis=-1, keepdims=True)
    return x * lax.rsqrt(ms + EPS) * g


def _layernorm(x, g, b):
    mu = jnp.mean(x, axis=-1, keepdims=True)
    d = x - mu
    var = jnp.mean(d * d, axis=-1, keepdims=True)
    return d * lax.rsqrt(var + EPS) * g + b


def _sigmoid(x):
    return 1.0 / (1.0 + jnp.exp(-x))


def _gelu_tanh(x):
    c = math.sqrt(2.0 / math.pi)
    return x * (0.5 * (1.0 + jnp.tanh(c * (x + 0.044715 * (x * x * x)))))


def _log_sigmoid(x):
    return jnp.minimum(x, 0.0) - jnp.log1p(jnp.exp(-jnp.abs(x)))


def _ada_kernel(s_ref, w_ref, b_ref, o_ref):
    s = s_ref[...]
    s = s * _sigmoid(s)
    o_ref[...] = jnp.dot(s.astype(BF16), w_ref[...].astype(BF16), preferred_element_type=F32) + b_ref[...]


def _ada(s_rows, w_ada, b_ada, tn=1024):
    depth, d, n = w_ada.shape
    r = s_rows.shape[0]
    return pl.pallas_call(
        _ada_kernel,
        grid=(depth, n // tn),
        in_specs=[pl.BlockSpec((r, d), lambda l, j: (0, 0)),
                  pl.BlockSpec((None, d, tn), lambda l, j: (l, 0, j)),
                  pl.BlockSpec((None, 1, tn), lambda l, j: (l, 0, j))],
        out_specs=pl.BlockSpec((None, r, tn), lambda l, j: (l, 0, j)),
        out_shape=jax.ShapeDtypeStruct((depth, r, n), F32),
        compiler_params=_cparams(("parallel", "parallel")),
        name="ada_mod",
    )(s_rows, w_ada, b_ada.reshape(depth, 1, n))


def _in_proj_kernel(x_ref, xc_ref, sh_ref, sc_ref, shc_ref, scc_ref, g_ref, w_ref, wg_ref, o_ref, og_ref, h_scr,
                    *, tm, t_ctx, n_lat_tiles, rows_per_step):
    i = pl.program_id(1)
    first_col = pl.program_id(2) == 0
    is_lat = i < n_lat_tiles

    def pre(x, sh, sc):
        return (_rms(x, g_ref[...]) * (1.0 + sc) + sh).astype(BF16)

    @pl.when(jnp.logical_and(first_col, is_lat))
    def _():
        def body(r, carry):
            sl = pl.ds(pl.multiple_of(r * rows_per_step, rows_per_step), rows_per_step)
            h_scr[sl, :] = pre(x_ref[sl, :], sh_ref[...], sc_ref[...])
            return carry
        lax.fori_loop(0, tm // rows_per_step, body, 0)
        og_ref[...] = jnp.dot(h_scr[...], wg_ref[...], preferred_element_type=F32)

    @pl.when(jnp.logical_and(first_col, jnp.logical_not(is_lat)))
    def _():
        h_scr[0:t_ctx, :] = pre(xc_ref[...], shc_ref[...], scc_ref[...])
        og_ref[0:t_ctx, :] = jnp.dot(h_scr[0:t_ctx, :], wg_ref[...], preferred_element_type=F32)
        if t_ctx < tm:
            og_ref[t_ctx:, :] = jnp.zeros((tm - t_ctx, og_ref.shape[1]), F32)

    @pl.when(is_lat)
    def _():
        o_ref[...] = jnp.dot(h_scr[...], w_ref[...], preferred_element_type=F32).astype(BF16)

    @pl.when(jnp.logical_not(is_lat))
    def _():
        o_ref[0:t_ctx, :] = jnp.dot(h_scr[0:t_ctx, :], w_ref[...], preferred_element_type=F32).astype(BF16)
        if t_ctx < tm:
            o_ref[t_ctx:, :] = jnp.zeros((tm - t_ctx, o_ref.shape[1]), BF16)


def _in_proj(x, xc, mod, g_all, w_all, wg_all, layer, tm, tn=1280):
    b, t_lat, d = x.shape
    t_ctx = xc.shape[1]
    tt = t_lat + t_ctx
    nz = w_all.shape[2]
    n_lat_tiles = t_lat // tm
    assert t_lat % tm == 0 and t_ctx <= tm
    return pl.pallas_call(
        functools.partial(_in_proj_kernel, tm=tm, t_ctx=t_ctx, n_lat_tiles=n_lat_tiles, rows_per_step=min(tm, 256)),
        grid=(b, n_lat_tiles + 1, nz // tn),
        in_specs=[pl.BlockSpec((None, tm, d), lambda bb, i, j: (bb, jnp.minimum(i, n_lat_tiles - 1), 0)),
                  pl.BlockSpec((None, t_ctx, d), lambda bb, i, j: (bb, 0, 0)),
                  _mod_spec(layer, 0, d), _mod_spec(layer, 1, d),
                  _mod_spec(layer, 0, d, row=b), _mod_spec(layer, 1, d, row=b),
                  _layer_row(layer, d),
                  pl.BlockSpec((None, d, tn), lambda bb, i, j: (layer, 0, j)),
                  pl.BlockSpec((None, d, LANE), lambda *_: (layer, 0, 0))],
        out_specs=[pl.BlockSpec((None, tm, tn), lambda bb, i, j: (bb, i, j)),
                   pl.BlockSpec((None, tm, LANE), lambda bb, i, j: (bb, i, 0))],
        out_shape=[jax.ShapeDtypeStruct((b, tt, nz), BF16), jax.ShapeDtypeStruct((b, tt, LANE), F32)],
        scratch_shapes=[pltpu.VMEM((tm, d), BF16)],
        compiler_params=_cparams(("parallel", "parallel", "arbitrary")),
        name="in_proj",
    )(x, xc, mod, mod, mod, mod, g_all, w_all, wg_all)


def _prefix_lanes(x, op, fill):
    lane = lax.broadcasted_iota(jnp.int32, x.shape, 1)
    sh = 1
    while sh < x.shape[1]:
        x = op(x, jnp.where(lane >= sh, pltpu.roll(x, sh, 1), fill))
        sh *= 2
    return x


def _suffix_lanes(x, op, fill):
    n = x.shape[1]
    lane = lax.broadcasted_iota(jnp.int32, x.shape, 1)
    sh = 1
    while sh < n:
        x = op(x, jnp.where(lane < n - sh, pltpu.roll(x, n - sh, 1), fill))
        sh *= 2
    return x


def _mlstm_kernel(q_ref, k_ref, v_ref, o_ref, gt_ref, gb_ref, ng_ref, out_ref,
                  rb_scr, gr_scr, vt_scr, u_scr, cp_scr, st_scr, sa_scr, sc_scr, sm_scr, *, n_lat, n_ctx):
    L = CHUNK
    nc = n_lat + n_ctx
    VT_ROWS = vt_scr.shape[1]
    head = pl.program_id(1)
    row = lax.broadcasted_iota(jnp.int32, (L, L), 0)
    col = lax.broadcasted_iota(jnp.int32, (L, L), 1)
    lower = col <= row
    upper = col >= row
    k_scale = HEAD_DIM ** -0.5
    HALF = GATE_ROWS // 2

    def rows_of(c):
        return pl.ds(pl.multiple_of(c * L, L), L)

    def step_of(c, direction):
        if direction == 0:
            return jnp.where(c < n_lat, c + n_ctx, c - n_lat)
        return nc - 1 - c

    @pl.when(head == 0)
    def _():
        lower_f = lower.astype(F32)
        upper_f = upper.astype(F32)
        n_all = nc * HALF
        rg = lax.broadcasted_iota(jnp.int32, (n_all, L), 0) % HALF
        is_cum = (rg % 8) >= 4
        is_bwd = rg >= 8

        def to_rows(c, carry):
            gr_scr[c, 0:HALF, :] = (gt_ref[rows_of(c), :] + gb_ref[...]).T[0:HALF, :]
            return carry

        lax.fori_loop(0, nc, to_rows, 0, unroll=2)

        G = gr_scr[:, 0:HALF, :].reshape(n_all, L)
        Gf = _log_sigmoid(G)
        Bf = jnp.dot(Gf, upper_f, precision=lax.Precision.HIGHEST, preferred_element_type=F32)
        Bb = jnp.dot(Gf, lower_f, precision=lax.Precision.HIGHEST, preferred_element_type=F32)
        Bsel = jnp.where(is_bwd, Bb, Bf)
        R = G - pltpu.roll(Bsel, n_all - 4, 0)
        CM = jnp.where(is_bwd, _suffix_lanes(R, jnp.maximum, -jnp.inf), _prefix_lanes(R, jnp.maximum, -jnp.inf))
        gr_scr[:, 0:HALF, :] = jnp.where(is_cum, Bsel, G).reshape(nc, HALF, L)
        gr_scr[:, HALF:GATE_ROWS, :] = jnp.where(is_cum, pltpu.roll(CM, 4, 0), R).reshape(nc, HALF, L)

    row_g = lax.broadcasted_iota(jnp.int32, (GATE_ROWS, L), 0)
    lane_r = lax.broadcasted_iota(jnp.int32, (1, LANE), 1)

    def gate_row(x, idx):
        return jnp.sum(jnp.where(row_g == idx, x, 0.0), axis=0, keepdims=True)

    st_scr[...] = jnp.zeros_like(st_scr)
    ones_rows = jnp.ones((VT_ROWS - HEAD_DIM, L), F32)

    def summarize(c, carry):
        vt = jnp.concatenate([v_ref[rows_of(c), :].astype(F32).T, ones_rows], axis=0)
        vt_scr[c] = vt.astype(BF16)
        k = k_ref[rows_of(c), :]
        grv = gr_scr[c]
        for direction in (0, 1):
            ci = direction * 8 + head
            b_row = gate_row(grv, ci + 4)
            i_row = gate_row(grv, ci)
            total = b_row[:, L - 1:L] if direction == 0 else b_row[:, 0:1]
            g_row = total - b_row + i_row
            mg = jnp.max(g_row, axis=1, keepdims=True)
            w_row = jnp.exp(g_row - mg) * k_scale
            step = step_of(c, direction)
            u_scr[direction * nc + step] = jnp.dot((vt * w_row).astype(BF16), k, preferred_element_type=F32)
            st_scr[direction, pl.ds(step, 1), :] = jnp.where(lane_r < 64, total, mg)
            rb_scr[direction * nc + c] = jnp.broadcast_to(gate_row(grv, 16 + ci), (L, L)).T
        return carry

    lax.fori_loop(0, nc, summarize, 0, unroll=4)

    lane8 = lax.broadcasted_iota(jnp.int32, (8, LANE), 1)
    for direction in (0, 1):
        xt = st_scr[direction].T
        valid = lane8 < nc
        T = jnp.where(valid, jnp.broadcast_to(xt[0:1, :], (8, LANE)), 0.0)
        G = jnp.where(valid, jnp.broadcast_to(xt[64:65, :], (8, LANE)), -jnp.inf)
        PT = _prefix_lanes(T, jnp.add, 0.0)
        mvec = PT + jnp.maximum(_prefix_lanes(G - PT, jnp.maximum, -jnp.inf), 0.0)
        mprev = jnp.where(lane8 >= 1, pltpu.roll(mvec, 1, 1), 0.0)
        a = jnp.exp(T + mprev - mvec)
        c2 = jnp.exp(G - mvec)
        sa_scr[direction] = jnp.broadcast_to(a[0:1, :], (LANE, LANE)).T
        sc_scr[direction] = jnp.broadcast_to(c2[0:1, :], (LANE, LANE)).T
        sm_scr[direction] = jnp.broadcast_to(mprev[0:1, :], (LANE, LANE)).T

    def scan(s, carry):
        new = []
        for direction in (0, 1):
            ct = carry[direction]
            idx = direction * nc + s
            cp_scr[idx] = ct.astype(BF16)
            a = sa_scr[direction, pl.ds(s, 1), :]
            c2 = sc_scr[direction, pl.ds(s, 1), :]
            new.append(a * ct + c2 * u_scr[idx])
        return tuple(new)

    zero = jnp.zeros((VT_ROWS, HEAD_DIM), F32)
    lax.fori_loop(0, nc, scan, (zero, zero))

    nt_dims = (((1,), (1,)), ((), ()))

    def emit(c, carry):
        q = q_ref[rows_of(c), :]
        vt = vt_scr[c]
        grv = gr_scr[c]
        qk_t = lax.dot_general(k_ref[rows_of(c), :], q, nt_dims, preferred_element_type=F32) * k_scale
        h_t = jnp.zeros((HEAD_DIM, L), F32)
        for direction in (0, 1):
            ci = direction * 8 + head
            step = step_of(c, direction)
            r_col = rb_scr[direction * nc + c]
            cm_row = gate_row(grv, 16 + ci + 4)
            b_row = gate_row(grv, ci + 4)
            p_t = jnp.exp(jnp.where(upper if direction == 0 else lower, r_col - cm_row, -jnp.inf))
            intra = jnp.dot(vt, (qk_t * p_t).astype(BF16), preferred_element_type=F32)
            inter = lax.dot_general(cp_scr[direction * nc + step], q, nt_dims, preferred_element_type=F32)
            mp = sm_scr[direction, pl.ds(step, 1), :]
            mx = jnp.maximum(mp, cm_row)
            e1 = jnp.exp(mp - mx)
            e2 = jnp.exp(cm_row - mx)
            num = e1 * inter[:HEAD_DIM] + e2 * intra[:HEAD_DIM]
            den = e1 * inter[HEAD_DIM:HEAD_DIM + 1] + e2 * intra[HEAD_DIM:HEAD_DIM + 1]
            h_t = h_t + num * (1.0 / jnp.maximum(jnp.abs(den), jnp.exp(-(b_row + mx))))
        mu = jnp.mean(h_t, axis=0, keepdims=True)
        dev = h_t - mu
        var = jnp.mean(dev * dev, axis=0, keepdims=True)
        hn = (dev * lax.rsqrt(var + EPS)).T * ng_ref[...]
        out_ref[rows_of(c), :] = (hn * _sigmoid(o_ref[rows_of(c), :].astype(F32))).astype(BF16)
        return carry

    lax.fori_loop(0, nc, emit, 0, unroll=4)


def _mlstm(z, zg, gate_b_all, norm_g_all, layer, t_lat, t_ctx):
    b, tt, _ = z.shape
    nc = tt // CHUNK
    assert nc <= 64
    vt_rows = HEAD_DIM + 16
    blk = lambda off: pl.BlockSpec((None, tt, LANE), lambda bb, h: (bb, 0, off // LANE + h))
    return pl.pallas_call(
        functools.partial(_mlstm_kernel, n_lat=t_lat // CHUNK, n_ctx=t_ctx // CHUNK),
        grid=(b, ML_HEADS),
        in_specs=[blk(COL_MQ), blk(COL_MK), blk(COL_MV), blk(COL_MO),
                  pl.BlockSpec((None, tt, LANE), lambda bb, h: (bb, 0, 0)),
                  _layer_row(layer, LANE),
                  pl.BlockSpec((None, 1, LANE), lambda bb, h: (layer, 0, h))],
        out_specs=pl.BlockSpec((None, tt, LANE), lambda bb, h: (bb, 0, h)),
        out_shape=jax.ShapeDtypeStruct((b, tt, GROUP_W), BF16),
        scratch_shapes=[pltpu.VMEM((2 * nc, CHUNK, CHUNK), F32),
                        pltpu.VMEM((nc, GATE_ROWS, LANE), F32),
                        pltpu.VMEM((nc, vt_rows, CHUNK), BF16),
                        pltpu.VMEM((2 * nc, vt_rows, HEAD_DIM), F32),
                        pltpu.VMEM((2 * nc, vt_rows, HEAD_DIM), BF16),
                        pltpu.VMEM((2, LANE, LANE), F32),
                        pltpu.VMEM((2, LANE, LANE), F32),
                        pltpu.VMEM((2, LANE, LANE), F32),
                        pltpu.VMEM((2, LANE, LANE), F32)],
        compiler_params=_cparams(("parallel", "arbitrary")),
        name="mlstm",
    )(z, z, z, z, zg, gate_b_all, norm_g_all)


def _sg_kernel(u_ref, v_ref, lng_ref, lnb_ref, w_ref, bs_ref, o_ref, *, n_chunks):
    def body(c, carry):
        sl = pl.ds(pl.multiple_of(c * CHUNK, CHUNK), CHUNK)
        u = _gelu_tanh(u_ref[sl, :].astype(F32))
        v = _layernorm(_gelu_tanh(v_ref[sl, :].astype(F32)), lng_ref[...], lnb_ref[...]).astype(BF16)
        for g in range(SG_GROUPS):
            gs = slice(g * LANE, (g + 1) * LANE)
            mixed = jnp.dot(w_ref[g], v[:, gs], preferred_element_type=F32) + bs_ref[g]
            o_ref[sl, gs] = (u[:, gs] * mixed).astype(BF16)
        return carry

    lax.fori_loop(0, n_chunks, body, 0)


def _spatial_gating(z, ln_g_all, ln_b_all, w_s_all, b_s_all, layer, n_rows, tr):
    b = z.shape[0]
    blk = lambda off: pl.BlockSpec((None, tr, GROUP_W), lambda bb, i: (bb, i, off // GROUP_W))
    per_layer = lambda *_: (layer, 0, 0, 0)
    return pl.pallas_call(
        functools.partial(_sg_kernel, n_chunks=tr // CHUNK),
        grid=(b, n_rows // tr),
        in_specs=[blk(COL_SU), blk(COL_SV),
                  _layer_row(layer, GROUP_W), _layer_row(layer, GROUP_W),
                  pl.BlockSpec((None, SG_GROUPS, CHUNK, CHUNK), per_layer),
                  pl.BlockSpec((None, SG_GROUPS, CHUNK, LANE), per_layer)],
        out_specs=pl.BlockSpec((None, tr, GROUP_W), lambda bb, i: (bb, i, 0)),
        out_shape=jax.ShapeDtypeStruct((b, n_rows, GROUP_W), BF16),
        compiler_params=_cparams(("parallel", "parallel")),
        name="spatial_gating",
    )(z, z, ln_g_all, ln_b_all, w_s_all, b_s_all)


def _conf_kernel(a_ref, g_ref, ap_ref, gp_ref, an_ref, gn_ref, w_ref, b_ref, lng_ref, lnb_ref, o_ref,
                 y_scr, ysh_scr, cv_scr, *, tc, seq_starts, seq_ends):
    i = pl.program_id(1)
    has_prev = functools.reduce(jnp.logical_and, [i != s for s in seq_starts])
    has_next = functools.reduce(jnp.logical_and, [i != e for e in seq_ends])
    H = CONV_HALO

    def glu(a, g):
        return a[...].astype(F32) * _sigmoid(g[...].astype(F32))

    y_scr[H:H + tc, :] = glu(a_ref, g_ref)
    y_scr[0:H, :] = jnp.where(has_prev, glu(ap_ref, gp_ref), 0.0)
    y_scr[H + tc:, :] = jnp.where(has_next, glu(an_ref, gn_ref), 0.0)
    n_sh = ysh_scr.shape[1]
    for j in range(SUBLANES):
        ysh_scr[j] = y_scr[j:j + n_sh, :]
    rb = 64
    base = H - (CONV_W // 2)
    for cb in range(GROUP_W // LANE):
        cs = slice(cb * LANE, (cb + 1) * LANE)
        for r in range(tc // rb):
            acc = jnp.zeros((rb, LANE), F32)
            for k in range(CONV_W):
                a, j = divmod(base + k, SUBLANES)
                acc = acc + ysh_scr[j, r * rb + SUBLANES * a:r * rb + SUBLANES * a + rb, cs] * w_ref[k:k + 1, cs]
            cv_scr[r * rb:(r + 1) * rb, cs] = acc
    y = _layernorm(cv_scr[...] + b_ref[...], lng_ref[...], lnb_ref[...])
    o_ref[...] = (y * _sigmoid(y)).astype(BF16)


def _conformer(z, w_dw_all, b_dw_all, ln_g_all, ln_b_all, layer, t_lat, n_rows, tc):
    b = z.shape[0]
    hb = tc // CONV_HALO
    n_tiles = n_rows // tc
    n_halo = z.shape[1] // CONV_HALO
    ca, cg = COL_CA // GROUP_W, COL_CG // GROUP_W
    cur = lambda cblk: pl.BlockSpec((None, tc, GROUP_W), lambda bb, i: (bb, i, cblk))
    prev = lambda cblk: pl.BlockSpec((None, CONV_HALO, GROUP_W),
                                     lambda bb, i: (bb, jnp.maximum(i * hb - 1, 0), cblk))
    nxt = lambda cblk: pl.BlockSpec((None, CONV_HALO, GROUP_W),
                                    lambda bb, i: (bb, jnp.minimum((i + 1) * hb, n_halo - 1), cblk))
    seq_starts = (0, t_lat // tc)
    seq_ends = (t_lat // tc - 1, z.shape[1] // tc - 1)
    return pl.pallas_call(
        functools.partial(_conf_kernel, tc=tc, seq_starts=seq_starts, seq_ends=seq_ends),
        grid=(b, n_tiles),
        in_specs=[cur(ca), cur(cg), prev(ca), prev(cg), nxt(ca), nxt(cg),
                  pl.BlockSpec((None, CONV_W, GROUP_W), lambda *_: (layer, 0, 0)), _layer_row(layer, GROUP_W),
                  _layer_row(layer, GROUP_W), _layer_row(layer, GROUP_W)],
        out_specs=pl.BlockSpec((None, tc, GROUP_W), lambda bb, i: (bb, i, 0)),
        out_shape=jax.ShapeDtypeStruct((b, n_rows, GROUP_W), BF16),
        scratch_shapes=[pltpu.VMEM((tc + 2 * CONV_HALO, GROUP_W), F32),
                        pltpu.VMEM((SUBLANES, tc + 2 * CONV_HALO - SUBLANES, GROUP_W), F32),
                        pltpu.VMEM((tc, GROUP_W), F32)],
        compiler_params=_cparams(("parallel", "parallel")),
        name="conformer_conv",
    )(z, z, z, z, z, z, w_dw_all, b_dw_all, ln_g_all, ln_b_all)


def _attn_prep_kernel(q_ref, k_ref, v_ref, cos_ref, sin_ref, qg_ref, kg_ref, qo_ref, ko_ref, vo_ref):
    cosf = cos_ref[...]
    sinf = sin_ref[...]

    def norm_rope(x, g, scale):
        y = _rms(x.astype(F32), g)
        return (y * cosf + pltpu.roll(y, HEAD_DIM // 2, 1) * sinf) * scale

    q_scale = HEAD_DIM ** -0.5 * math.log2(math.e)
    for h in range(ATT_Q_HEADS):
        hs = slice(h * HEAD_DIM, (h + 1) * HEAD_DIM)
        qo_ref[:, hs] = norm_rope(q_ref[:, hs], qg_ref[...], q_scale).astype(BF16)
    ones = jnp.ones((v_ref.shape[0], HEAD_DIM), BF16)
    for h in range(ATT_KV_HEADS):
        hs = slice(h * HEAD_DIM, (h + 1) * HEAD_DIM)
        ko_ref[:, hs] = norm_rope(k_ref[:, hs], kg_ref[...], 1.0).astype(BF16)
        vo_ref[:, 2 * h * HEAD_DIM:(2 * h + 1) * HEAD_DIM] = v_ref[:, hs]
        vo_ref[:, (2 * h + 1) * HEAD_DIM:(2 * h + 2) * HEAD_DIM] = ones


def _attn_prep(z, cosf, sinf, qg_all, kg_all, layer, tr):
    b, tt, _ = z.shape
    qw, kw = ATT_Q_HEADS * HEAD_DIM, ATT_KV_HEADS * HEAD_DIM
    return pl.pallas_call(
        _attn_prep_kernel,
        grid=(b, tt // tr),
        in_specs=[pl.BlockSpec((None, tr, qw), lambda bb, i: (bb, i, COL_AQ // qw)),
                  pl.BlockSpec((None, tr, kw), lambda bb, i: (bb, i, COL_AK // kw)),
                  pl.BlockSpec((None, tr, kw), lambda bb, i: (bb, i, COL_AV // kw)),
                  pl.BlockSpec((tr, HEAD_DIM), lambda bb, i: (i, 0)),
                  pl.BlockSpec((tr, HEAD_DIM), lambda bb, i: (i, 0)),
                  _layer_row(layer, HEAD_DIM), _layer_row(layer, HEAD_DIM)],
        out_specs=[pl.BlockSpec((None, tr, qw), lambda bb, i: (bb, i, 0)),
                   pl.BlockSpec((None, tr, kw), lambda bb, i: (bb, i, 0)),
                   pl.BlockSpec((None, tr, 2 * kw), lambda bb, i: (bb, i, 0))],
        out_shape=[jax.ShapeDtypeStruct((b, tt, qw), BF16),
                   jax.ShapeDtypeStruct((b, tt, kw), BF16),
                   jax.ShapeDtypeStruct((b, tt, 2 * kw), BF16)],
        compiler_params=_cparams(("parallel", "parallel")),
        name="attn_prep",
    )(z, z, z, cosf, sinf, qg_all, kg_all)


def _attn_kernel(q_ref, k_ref, v_ref, o_ref, *, tq, sub):
    k = k_ref[...]
    vaug = v_ref[...]
    for g in range(ATT_Q_HEADS // ATT_KV_HEADS):
        hs = slice(g * HEAD_DIM, (g + 1) * HEAD_DIM)
        for r in range(tq // sub):
            rs = slice(r * sub, (r + 1) * sub)
            s = lax.dot_general(q_ref[rs, hs], k, (((1,), (1,)), ((), ())), preferred_element_type=F32)
            p = jnp.exp2(s - jnp.max(s, axis=-1, keepdims=True)).astype(BF16)
            res = jnp.dot(p, vaug, preferred_element_type=F32)
            o_ref[rs, hs] = (res[:, :HEAD_DIM] / res[:, HEAD_DIM:]).astype(BF16)


def _attention(qr, kr, vaug, q_row0, n_q, k_row0, n_k, tq):
    b = qr.shape[0]
    gw = (ATT_Q_HEADS // ATT_KV_HEADS) * HEAD_DIM
    assert q_row0 % tq == 0 and n_q % tq == 0 and k_row0 % n_k == 0
    qb, kb = q_row0 // tq, k_row0 // n_k
    return pl.pallas_call(
        functools.partial(_attn_kernel, tq=tq, sub=min(tq, 256)),
        grid=(b, ATT_KV_HEADS, n_q // tq),
        in_specs=[pl.BlockSpec((None, tq, gw), lambda bb, h, i: (bb, i + qb, h)),
                  pl.BlockSpec((None, n_k, HEAD_DIM), lambda bb, h, i: (bb, kb, h)),
                  pl.BlockSpec((None, n_k, 2 * HEAD_DIM), lambda bb, h, i: (bb, kb, h))],
        out_specs=pl.BlockSpec((None, tq, gw), lambda bb, h, i: (bb, i, h)),
        out_shape=jax.ShapeDtypeStruct((b, n_q, ATT_Q_HEADS * HEAD_DIM), BF16),
        compiler_params=_cparams(("parallel", "parallel", "arbitrary")),
        name="attention",
    )(qr, kr, vaug)


def _out_proj_kernel(ya_ref, yb_ref, yc_ref, yd_ref, w_ref, x_ref, gate_ref, pg_ref, o_ref):
    acc = jnp.dot(ya_ref[...], w_ref[0 * GROUP_W:1 * GROUP_W, :], preferred_element_type=F32)
    acc += jnp.dot(yb_ref[...], w_ref[1 * GROUP_W:2 * GROUP_W, :], preferred_element_type=F32)
    acc += jnp.dot(yc_ref[...], w_ref[2 * GROUP_W:3 * GROUP_W, :], preferred_element_type=F32)
    acc += jnp.dot(yd_ref[...], w_ref[3 * GROUP_W:4 * GROUP_W, :], preferred_element_type=F32)
    o_ref[...] = x_ref[...] + gate_ref[...] * _rms(acc, pg_ref[...])


def _out_proj(ys, y_row_offs, w_out_all, xs, mod, mod_row, post_g_all, layer, tm):
    b, t, d = xs.shape
    assert all(off % tm == 0 for off in y_row_offs)
    yspecs = [pl.BlockSpec((None, tm, GROUP_W), functools.partial(lambda bb, i, o: (bb, i + o, 0), o=off // tm))
              for off in y_row_offs]
    return pl.pallas_call(
        _out_proj_kernel,
        grid=(b, t // tm),
        in_specs=[*yspecs,
                  pl.BlockSpec((None,) + w_out_all.shape[1:], lambda *_: (layer, 0, 0)),
                  pl.BlockSpec((None, tm, d), lambda bb, i: (bb, i, 0)),
                  _mod_spec(layer, 2, d, mod_row),
                  _layer_row(layer, d)],
        out_specs=pl.BlockSpec((None, tm, d), lambda bb, i: (bb, i, 0)),
        out_shape=jax.ShapeDtypeStruct((b, t, d), F32),
        compiler_params=_cparams(("parallel", "parallel")),
        name="out_proj",
    )(*ys, w_out_all, xs, mod, post_g_all)


def _ffn_kernel(x_ref, xp_ref, xn_ref, sh_ref, sc_ref, gate_ref, pre_g_ref, post_g_ref, wg_ref, wu_ref,
                cw_ref, cb_ref, wd_ref, o_ref, h_scr, acc_scr, *, tm, n_tiles, n_ff, seq_len):
    i = pl.program_id(1)
    j = pl.program_id(2)
    H = FFN_HALO

    @pl.when(j == 0)
    def _():
        def pre(x):
            return _rms(x, pre_g_ref[...]) * (1.0 + sc_ref[...]) + sh_ref[...]
        h_scr[H:H + tm, :] = pre(x_ref[...]).astype(BF16)
        h_scr[0:H, :] = jnp.where(i > 0, pre(xp_ref[...]), 0.0).astype(BF16)
        h_scr[H + tm:, :] = jnp.where(i < n_tiles - 1, pre(xn_ref[...]), 0.0).astype(BF16)
        acc_scr[...] = jnp.zeros_like(acc_scr)

    n_ext = tm + 2 * H
    g_ext = jnp.dot(h_scr[...], wg_ref[...], preferred_element_type=F32)
    u = jnp.dot(h_scr[H:H + tm, :], wu_ref[...], preferred_element_type=F32)
    g_prev = pltpu.roll(g_ext, 1, 0)[H:H + tm]
    g_next = pltpu.roll(g_ext, n_ext - 1, 0)[H:H + tm]
    if seq_len < tm:
        pos = lax.broadcasted_iota(jnp.int32, (tm, 1), 0) % seq_len
        g_prev = jnp.where(pos == 0, 0.0, g_prev)
        g_next = jnp.where(pos == seq_len - 1, 0.0, g_next)
    conv = g_prev * cw_ref[0:1, :] + g_ext[H:H + tm] * cw_ref[1:2, :] + g_next * cw_ref[2:3, :] + cb_ref[...]
    act = (conv * _sigmoid(conv)) * u
    acc_scr[...] += jnp.dot(act.astype(BF16), wd_ref[...], preferred_element_type=F32)

    @pl.when(j == n_ff - 1)
    def _():
        o_ref[...] = x_ref[...] + gate_ref[...] * _rms(acc_scr[...], post_g_ref[...])


def _ffn(xs, mod, mod_row, pre_g_all, post_g_all, w_up_all, cv_w_all, cv_b_all, w_down_all, layer, tm, seq_len, tf=512):
    b, t, d = xs.shape
    assert seq_len == t or (tm % seq_len == 0 and t % tm == 0)
    d_ff = w_down_all.shape[1]
    n_ff = d_ff // tf
    n_tiles = t // tm
    hb = tm // FFN_HALO
    n_halo = t // FFN_HALO
    return pl.pallas_call(
        functools.partial(_ffn_kernel, tm=tm, n_tiles=n_tiles, n_ff=n_ff, seq_len=seq_len),
        grid=(b, n_tiles, n_ff),
        in_specs=[pl.BlockSpec((None, tm, d), lambda bb, i, j: (bb, i, 0)),
                  pl.BlockSpec((None, FFN_HALO, d), lambda bb, i, j: (bb, jnp.maximum(i * hb - 1, 0), 0)),
                  pl.BlockSpec((None, FFN_HALO, d), lambda bb, i, j: (bb, jnp.minimum((i + 1) * hb, n_halo - 1), 0)),
                  _mod_spec(layer, 3, d, mod_row), _mod_spec(layer, 4, d, mod_row), _mod_spec(layer, 5, d, mod_row),
                  _layer_row(layer, d), _layer_row(layer, d),
                  pl.BlockSpec((None, d, tf), lambda bb, i, j: (layer, 0, j)),
                  pl.BlockSpec((None, d, tf), lambda bb, i, j: (layer, 0, j + n_ff)),
                  pl.BlockSpec((None, cv_w_all.shape[1], tf), lambda bb, i, j: (layer, 0, j)),
                  pl.BlockSpec((None, 1, tf), lambda bb, i, j: (layer, 0, j)),
                  pl.BlockSpec((None, tf, d), lambda bb, i, j: (layer, j, 0))],
        out_specs=pl.BlockSpec((None, tm, d), lambda bb, i, j: (bb, i, 0)),
        out_shape=jax.ShapeDtypeStruct((b, t, d), F32),
        scratch_shapes=[pltpu.VMEM((tm + 2 * FFN_HALO, d), BF16), pltpu.VMEM((tm, d), F32)],
        compiler_params=_cparams(("parallel", "parallel", "arbitrary")),
        name="conv_glu_ffn",
    )(xs, xs, xs, mod, mod, mod, pre_g_all, post_g_all, w_up_all, w_up_all, cv_w_all, cv_b_all, w_down_all)


def _pack_w_in(w_in):
    depth, d, _ = w_in.shape
    main = jnp.concatenate([w_in[..., :4 * GROUP_W], w_in[..., 4 * GROUP_W + N_GATE:]], axis=-1).astype(BF16)
    gates = w_in[..., 4 * GROUP_W:4 * GROUP_W + N_GATE]
    gates = jnp.concatenate([gates, jnp.zeros((depth, d, LANE - N_GATE), w_in.dtype)], axis=-1).astype(BF16)
    return main, gates


def _rope_tables(t_lat, t_ctx):
    t = jnp.arange(t_lat)
    row = (t // GRID_W).astype(F32)
    colp = (t % GRID_W).astype(F32)
    axis_dim = HEAD_DIM // 2
    inv = jnp.power(ROPE_THETA, -jnp.arange(0, axis_dim, 2, dtype=F32) / axis_dim)
    ang = jnp.concatenate([row[:, None] * inv, colp[:, None] * inv], axis=-1)
    cos, sin = jnp.cos(ang), jnp.sin(ang)
    cosf = jnp.concatenate([cos, cos], axis=-1)
    sinf = jnp.concatenate([-sin, sin], axis=-1)
    cosf = jnp.concatenate([cosf, jnp.ones((t_ctx, HEAD_DIM), F32)], axis=0)
    sinf = jnp.concatenate([sinf, jnp.zeros((t_ctx, HEAD_DIM), F32)], axis=0)
    return cosf, sinf


def kernel(x, c, ctx, c_ctx, w_ada, b_ada, pre_mix_g, post_mix_g, w_in, ml_gate_b, ml_norm_g, sg_ln_g, sg_ln_b,
           sg_w, sg_b, cv_w, cv_b, cv_ln_g, cv_ln_b, at_qn_g, at_kn_g, w_out, pre_ffn_g, post_ffn_g, w_up,
           ffn_cv_w, ffn_cv_b, w_down):
    b, t_lat, d = x.shape
    t_ctx = ctx.shape[1]
    tt = t_lat + t_ctx
    depth = w_ada.shape[0]
    tm_in = min(1024, t_lat)
    tm_lat = min(512, t_lat)
    assert t_lat % tm_in == 0 and t_lat % t_ctx == 0 and t_ctx % CHUNK == 0 and t_ctx % 256 == 0

    n_rows = -(-(b + 1) // 8) * 8
    s_rows = jnp.concatenate([c, c_ctx[None, :], jnp.zeros((n_rows - b - 1, d), F32)], axis=0)
    mod = _ada(s_rows, w_ada, b_ada).reshape(depth, n_rows, 1, 6 * d)
    cosf, sinf = _rope_tables(t_lat, t_ctx)

    rows = lambda a: a.reshape(depth, 1, -1)
    w_main, w_gate = _pack_w_in(w_in)
    w_out_b, w_up_b, w_down_b = w_out.astype(BF16), w_up.astype(BF16), w_down.astype(BF16)
    gate_b = rows(jnp.concatenate([ml_gate_b, jnp.zeros((depth, LANE - N_GATE), F32)], axis=1))
    sg_w_b = sg_w.astype(BF16)
    sg_b_full = jnp.broadcast_to(sg_b[..., None], (depth, SG_GROUPS, CHUNK, LANE))
    pre_mix, post_mix, pre_ffn, post_ffn = rows(pre_mix_g), rows(post_mix_g), rows(pre_ffn_g), rows(post_ffn_g)
    ml_norm, sg_g, sg_bb, cv_bias, cv_g, cv_bb = (rows(ml_norm_g), rows(sg_ln_g), rows(sg_ln_b), rows(cv_b),
                                                  rows(cv_ln_g), rows(cv_ln_b))
    qn_g, kn_g, ffn_bias = rows(at_qn_g), rows(at_kn_g), rows(ffn_cv_b)

    xc = ctx
    for i in range(depth):
        need_ctx = i < depth - 1
        z, zg = _in_proj(x, xc, mod, pre_mix, w_main, w_gate, i, tm_in)
        ya = _mlstm(z, zg, gate_b, ml_norm, i, t_lat, t_ctx)

        mix_rows = tt if need_ctx else t_lat
        sg_tr = 2176 if (need_ctx and tt % 2176 == 0) else (2048 if t_lat % 2048 == 0 else 256)
        sg_tr = sg_tr if mix_rows % sg_tr == 0 else 256
        yb = _spatial_gating(z, sg_g, sg_bb, sg_w_b, sg_b_full, i, mix_rows, sg_tr)
        yc = _conformer(z, cv_w, cv_bias, cv_g, cv_bb, i, t_lat, mix_rows, 256)

        prep_tr = 1088 if tt % 1088 == 0 else 256
        qr, kr, vaug = _attn_prep(z, cosf, sinf, qn_g, kn_g, i, prep_tr)
        yd = _attention(qr, kr, vaug, 0, t_lat, 0, tt, min(512, t_lat))

        x = _out_proj((ya, yb, yc, yd), (0, 0, 0, 0), w_out_b, x, mod, None, post_mix, i, tm_lat)
        x = _ffn(x, mod, None, pre_ffn, post_ffn, w_up_b, ffn_cv_w, ffn_bias, w_down_b, i, tm_lat, t_lat)
        if need_ctx:
            yd_c = _attention(qr, kr, vaug, t_lat, t_ctx, t_lat, t_ctx, t_ctx)
            xc = _out_proj((ya, yb, yc, yd_c), (t_lat, t_lat, t_lat, 0), w_out_b, xc, mod, b, post_mix, i, t_ctx)
            tm_ctx = t_ctx * max(1, min(b, 512 // t_ctx))
            tm_ctx = tm_ctx if (b * t_ctx) % tm_ctx == 0 else t_ctx
            xc = _ffn(xc.reshape(1, b * t_ctx, d), mod, b, pre_ffn, post_ffn, w_up_b, ffn_cv_w, ffn_bias, w_down_b,
                      i, tm_ctx, t_ctx).reshape(b, t_ctx, d)
    return x
```

```python
import functools
import math

import jax
import jax.numpy as jnp
from jax import lax
from jax.experimental import pallas as pl
from jax.experimental.pallas import tpu as pltpu

F32 = jnp.float32
BF16 = jnp.bfloat16

EPS = 1e-6
LANE = 128
SUBLANES = 8
HEAD_DIM = 128
GROUP_W = 512
ML_HEADS = 4
CHUNK = 128
SG_GROUPS = 4
CONV_W = 31
CONV_HALO = 16
ATT_Q_HEADS = 4
ATT_KV_HEADS = 2
GRID_W = 64
ROPE_THETA = 10000.0
FFN_HALO = 16
N_GATE = 16
GATE_ROWS = 32

COL_MQ, COL_MK, COL_MV, COL_MO = 0, 512, 1024, 1536
COL_SU, COL_SV = 2048, 2560
COL_CA, COL_CG = 3072, 3584
COL_AQ, COL_AK, COL_AV = 4096, 4608, 4864
NZ = 5120

VMEM_LIMIT = 52 * 1024 * 1024


def _cparams(sem):
    return pltpu.CompilerParams(dimension_semantics=sem, vmem_limit_bytes=VMEM_LIMIT)


def _layer_row(layer, n):
    return pl.BlockSpec((None, 1, n), lambda *_: (layer, 0, 0))


def _mod_spec(layer, k, d, row=None):
    if row is None:
        return pl.BlockSpec((None, None, 1, d), lambda bb, *_: (layer, bb, 0, k))
    return pl.BlockSpec((None, None, 1, d), lambda *_: (layer, row, 0, k))


def _rms(x, g):
    ms = jnp.mean(x * x, axis=-1, keepdims=True)
    return x * lax.rsqrt(ms + EPS) * g


def _layernorm(x, g, b):
    mu = jnp.mean(x, axis=-1, keepdims=True)
    d = x - mu
    var = jnp.mean(d * d, axis=-1, keepdims=True)
    return d * lax.rsqrt(var + EPS) * g + b


def _sigmoid(x):
    return 1.0 / (1.0 + jnp.exp(-x))


def _gelu_tanh(x):
    c = math.sqrt(2.0 / math.pi)
    return x * (0.5 * (1.0 + jnp.tanh(c * (x + 0.044715 * (x * x * x)))))


def _log_sigmoid(x):
    return jnp.minimum(x, 0.0) - jnp.log1p(jnp.exp(-jnp.abs(x)))


def _ada_kernel(s_ref, w_ref, b_ref, o_ref):
    s = s_ref[...]
    s = s * _sigmoid(s)
    o_ref[...] = jnp.dot(s.astype(BF16), w_ref[...].astype(BF16), preferred_element_type=F32) + b_ref[...]


def _ada(s_rows, w_ada, b_ada, tn=1024):
    depth, d, n = w_ada.shape
    r = s_rows.shape[0]
    return pl.pallas_call(
        _ada_kernel,
        grid=(depth, n // tn),
        in_specs=[pl.BlockSpec((r, d), lambda l, j: (0, 0)),
                  pl.BlockSpec((None, d, tn), lambda l, j: (l, 0, j)),
                  pl.BlockSpec((None, 1, tn), lambda l, j: (l, 0, j))],
        out_specs=pl.BlockSpec((None, r, tn), lambda l, j: (l, 0, j)),
        out_shape=jax.ShapeDtypeStruct((depth, r, n), F32),
        compiler_params=_cparams(("parallel", "parallel")),
        name="ada_mod",
    )(s_rows, w_ada, b_ada.reshape(depth, 1, n))


def _in_proj_kernel(x_ref, xc_ref, sh_ref, sc_ref, shc_ref, scc_ref, g_ref, w_ref, wg_ref, o_ref, og_ref, h_scr,
                    *, tm, t_ctx, n_lat_tiles, rows_per_step):
    i = pl.program_id(1)
    first_col = pl.program_id(2) == 0
    is_lat = i < n_lat_tiles

    def pre(x, sh, sc):
        return (_rms(x, g_ref[...]) * (1.0 + sc) + sh).astype(BF16)

    def first_step(x_rows, sh, sc, n_rows):
        for r in range(n_rows // rows_per_step):
            sl = slice(r * rows_per_step, (r + 1) * rows_per_step)
            h = pre(x_rows[sl, :], sh[...], sc[...])
            h_scr[sl, :] = h
            o_ref[sl, :] = jnp.dot(h, w_ref[...], preferred_element_type=F32).astype(BF16)
            og_ref[sl, :] = jnp.dot(h, wg_ref[...], preferred_element_type=F32)

    @pl.when(jnp.logical_and(first_col, is_lat))
    def _():
        first_step(x_ref, sh_ref, sc_ref, tm)

    @pl.when(jnp.logical_and(jnp.logical_not(first_col), is_lat))
    def _():
        o_ref[...] = jnp.dot(h_scr[...], w_ref[...], preferred_element_type=F32).astype(BF16)

    @pl.when(jnp.logical_and(first_col, jnp.logical_not(is_lat)))
    def _():
        first_step(xc_ref, shc_ref, scc_ref, t_ctx)
        if t_ctx < tm:
            og_ref[t_ctx:, :] = jnp.zeros((tm - t_ctx, og_ref.shape[1]), F32)

    @pl.when(jnp.logical_and(jnp.logical_not(first_col), jnp.logical_not(is_lat)))
    def _():
        o_ref[0:t_ctx, :] = jnp.dot(h_scr[0:t_ctx, :], w_ref[...], preferred_element_type=F32).astype(BF16)

    if t_ctx < tm:
        @pl.when(jnp.logical_not(is_lat))
        def _():
            o_ref[t_ctx:, :] = jnp.zeros((tm - t_ctx, o_ref.shape[1]), BF16)


def _in_proj(x, xc, mod, g_all, w_all, wg_all, layer, tm, tn=1280):
    b, t_lat, d = x.shape
    t_ctx = xc.shape[1]
    tt = t_lat + t_ctx
    nz = w_all.shape[2]
    n_lat_tiles = t_lat // tm
    assert t_lat % tm == 0 and t_ctx <= tm
    return pl.pallas_call(
        functools.partial(_in_proj_kernel, tm=tm, t_ctx=t_ctx, n_lat_tiles=n_lat_tiles, rows_per_step=min(tm, 256)),
        grid=(b, n_lat_tiles + 1, nz // tn),
        in_specs=[pl.BlockSpec((None, tm, d), lambda bb, i, j: (bb, jnp.minimum(i, n_lat_tiles - 1), 0)),
                  pl.BlockSpec((None, t_ctx, d), lambda bb, i, j: (bb, 0, 0)),
                  _mod_spec(layer, 0, d), _mod_spec(layer, 1, d),
                  _mod_spec(layer, 0, d, row=b), _mod_spec(layer, 1, d, row=b),
                  _layer_row(layer, d),
                  pl.BlockSpec((None, d, tn), lambda bb, i, j: (layer, 0, j)),
                  pl.BlockSpec((None, d, LANE), lambda *_: (layer, 0, 0))],
        out_specs=[pl.BlockSpec((None, tm, tn), lambda bb, i, j: (bb, i, j)),
                   pl.BlockSpec((None, tm, LANE), lambda bb, i, j: (bb, i, 0))],
        out_shape=[jax.ShapeDtypeStruct((b, tt, nz), BF16), jax.ShapeDtypeStruct((b, tt, LANE), F32)],
        scratch_shapes=[pltpu.VMEM((tm, d), BF16)],
        compiler_params=_cparams(("parallel", "parallel", "arbitrary")),
        name="in_proj",
    )(x, xc, mod, mod, mod, mod, g_all, w_all, wg_all)


def _prefix_lanes(x, op, fill):
    lane = lax.broadcasted_iota(jnp.int32, x.shape, 1)
    sh = 1
    while sh < x.shape[1]:
        x = op(x, jnp.where(lane >= sh, pltpu.roll(x, sh, 1), fill))
        sh *= 2
    return x


def _suffix_lanes(x, op, fill):
    n = x.shape[1]
    lane = lax.broadcasted_iota(jnp.int32, x.shape, 1)
    sh = 1
    while sh < n:
        x = op(x, jnp.where(lane < n - sh, pltpu.roll(x, n - sh, 1), fill))
        sh *= 2
    return x


def _mlstm_kernel(q_ref, k_ref, v_ref, o_ref, gt_ref, gb_ref, ng_ref, out_ref,
                  rb_scr, gr_scr, vt_scr, u_scr, cp_scr, st_scr, sa_scr, sc_scr, sm_scr, *, n_lat, n_ctx):
    L = CHUNK
    nc = n_lat + n_ctx
    VT_ROWS = vt_scr.shape[1]
    head = pl.program_id(1)
    row = lax.broadcasted_iota(jnp.int32, (L, L), 0)
    col = lax.broadcasted_iota(jnp.int32, (L, L), 1)
    lower = col <= row
    upper = col >= row
    k_scale = HEAD_DIM ** -0.5
    HALF = GATE_ROWS // 2

    def rows_of(c):
        return pl.ds(pl.multiple_of(c * L, L), L)

    def step_of(c, direction):
        if direction == 0:
            return jnp.where(c < n_lat, c + n_ctx, c - n_lat)
        return nc - 1 - c

    @pl.when(head == 0)
    def _():
        lower_f = lower.astype(F32)
        upper_f = upper.astype(F32)
        n_all = nc * HALF
        rg = lax.broadcasted_iota(jnp.int32, (n_all, L), 0) % HALF
        is_cum = (rg % 8) >= 4
        is_bwd = rg >= 8

        def to_rows(c, carry):
            gr_scr[c, 0:HALF, :] = (gt_ref[rows_of(c), :] + gb_ref[...]).T[0:HALF, :]
            return carry

        lax.fori_loop(0, nc, to_rows, 0, unroll=2)

        G = gr_scr[:, 0:HALF, :].reshape(n_all, L)
        Gf = _log_sigmoid(G)
        Bf = jnp.dot(Gf, upper_f, precision=lax.Precision.HIGHEST, preferred_element_type=F32)
        Bb = jnp.dot(Gf, lower_f, precision=lax.Precision.HIGHEST, preferred_element_type=F32)
        Bsel = jnp.where(is_bwd, Bb, Bf)
        R = G - pltpu.roll(Bsel, n_all - 4, 0)
        CM = jnp.where(is_bwd, _suffix_lanes(R, jnp.maximum, -jnp.inf), _prefix_lanes(R, jnp.maximum, -jnp.inf))
        gr_scr[:, 0:HALF, :] = jnp.where(is_cum, Bsel, G).reshape(nc, HALF, L)
        gr_scr[:, HALF:GATE_ROWS, :] = jnp.where(is_cum, pltpu.roll(CM, 4, 0), R).reshape(nc, HALF, L)

    row_g = lax.broadcasted_iota(jnp.int32, (GATE_ROWS, L), 0)
    lane_r = lax.broadcasted_iota(jnp.int32, (1, LANE), 1)

    def gate_row(x, idx):
        return jnp.sum(jnp.where(row_g == idx, x, 0.0), axis=0, keepdims=True)

    st_scr[...] = jnp.zeros_like(st_scr)
    ones_rows = jnp.ones((VT_ROWS - HEAD_DIM, L), F32)

    def summarize(c, carry):
        vt = jnp.concatenate([v_ref[rows_of(c), :].astype(F32).T, ones_rows], axis=0)
        vt_scr[c] = vt.astype(BF16)
        k = k_ref[rows_of(c), :]
        grv = gr_scr[c]
        for direction in (0, 1):
            ci = direction * 8 + head
            b_row = gate_row(grv, ci + 4)
            i_row = gate_row(grv, ci)
            total = b_row[:, L - 1:L] if direction == 0 else b_row[:, 0:1]
            g_row = total - b_row + i_row
            mg = jnp.max(g_row, axis=1, keepdims=True)
            w_row = jnp.exp(g_row - mg) * k_scale
            step = step_of(c, direction)
            u_scr[direction * nc + step] = jnp.dot((vt * w_row).astype(BF16), k, preferred_element_type=F32)
            st_scr[direction, pl.ds(step, 1), :] = jnp.where(lane_r < 64, total, mg)
            rb_scr[direction * nc + c] = jnp.broadcast_to(gate_row(grv, 16 + ci), (L, L)).T
        return carry

    lax.fori_loop(0, nc, summarize, 0, unroll=4)

    lane8 = lax.broadcasted_iota(jnp.int32, (8, LANE), 1)
    for direction in (0, 1):
        xt = st_scr[direction].T
        valid = lane8 < nc
        T = jnp.where(valid, jnp.broadcast_to(xt[0:1, :], (8, LANE)), 0.0)
        G = jnp.where(valid, jnp.broadcast_to(xt[64:65, :], (8, LANE)), -jnp.inf)
        PT = _prefix_lanes(T, jnp.add, 0.0)
        mvec = PT + jnp.maximum(_prefix_lanes(G - PT, jnp.maximum, -jnp.inf), 0.0)
        mprev = jnp.where(lane8 >= 1, pltpu.roll(mvec, 1, 1), 0.0)
        a = jnp.exp(T + mprev - mvec)
        c2 = jnp.exp(G - mvec)
        sa_scr[direction] = jnp.broadcast_to(a[0:1, :], (LANE, LANE)).T
        sc_scr[direction] = jnp.broadcast_to(c2[0:1, :], (LANE, LANE)).T
        sm_scr[direction] = jnp.broadcast_to(mprev[0:1, :], (LANE, LANE)).T

    def scan(s, carry):
        new = []
        for direction in (0, 1):
            ct = carry[direction]
            idx = direction * nc + s
            cp_scr[idx] = ct.astype(BF16)
            a = sa_scr[direction, pl.ds(s, 1), :]
            c2 = sc_scr[direction, pl.ds(s, 1), :]
            new.append(a * ct + c2 * u_scr[idx])
        return tuple(new)

    zero = jnp.zeros((VT_ROWS, HEAD_DIM), F32)
    lax.fori_loop(0, nc, scan, (zero, zero))

    nt_dims = (((1,), (1,)), ((), ()))

    def emit(c, carry):
        q = q_ref[rows_of(c), :]
        vt = vt_scr[c]
        grv = gr_scr[c]
        qk_t = lax.dot_general(k_ref[rows_of(c), :], q, nt_dims, preferred_element_type=F32) * k_scale
        h_t = jnp.zeros((HEAD_DIM, L), F32)
        for direction in (0, 1):
            ci = direction * 8 + head
            step = step_of(c, direction)
            r_col = rb_scr[direction * nc + c]
            cm_row = gate_row(grv, 16 + ci + 4)
            b_row = gate_row(grv, ci + 4)
            p_t = jnp.exp(jnp.where(upper if direction == 0 else lower, r_col - cm_row, -jnp.inf))
            intra = jnp.dot(vt, (qk_t * p_t).astype(BF16), preferred_element_type=F32)
            inter = lax.dot_general(cp_scr[direction * nc + step], q, nt_dims, preferred_element_type=F32)
            mp = sm_scr[direction, pl.ds(step, 1), :]
            mx = jnp.maximum(mp, cm_row)
            e1 = jnp.exp(mp - mx)
            e2 = jnp.exp(cm_row - mx)
            num = e1 * inter[:HEAD_DIM] + e2 * intra[:HEAD_DIM]
            den = e1 * inter[HEAD_DIM:HEAD_DIM + 1] + e2 * intra[HEAD_DIM:HEAD_DIM + 1]
            h_t = h_t + num * (1.0 / jnp.maximum(jnp.abs(den), jnp.exp(-(b_row + mx))))
        mu = jnp.mean(h_t, axis=0, keepdims=True)
        dev = h_t - mu
        var = jnp.mean(dev * dev, axis=0, keepdims=True)
        hn = (dev * lax.rsqrt(var + EPS)).T * ng_ref[...]
        out_ref[rows_of(c), :] = (hn * _sigmoid(o_ref[rows_of(c), :].astype(F32))).astype(BF16)
        return carry

    lax.fori_loop(0, nc, emit, 0, unroll=8)


def _mlstm(z, zg, gate_b_all, norm_g_all, layer, t_lat, t_ctx):
    b, tt, _ = z.shape
    nc = tt // CHUNK
    assert nc <= 64
    vt_rows = HEAD_DIM + 16
    blk = lambda off: pl.BlockSpec((None, tt, LANE), lambda bb, h: (bb, 0, off // LANE + h))
    return pl.pallas_call(
        functools.partial(_mlstm_kernel, n_lat=t_lat // CHUNK, n_ctx=t_ctx // CHUNK),
        grid=(b, ML_HEADS),
        in_specs=[blk(COL_MQ), blk(COL_MK), blk(COL_MV), blk(COL_MO),
                  pl.BlockSpec((None, tt, LANE), lambda bb, h: (bb, 0, 0)),
                  _layer_row(layer, LANE),
                  pl.BlockSpec((None, 1, LANE), lambda bb, h: (layer, 0, h))],
        out_specs=pl.BlockSpec((None, tt, LANE), lambda bb, h: (bb, 0, h)),
        out_shape=jax.ShapeDtypeStruct((b, tt, GROUP_W), BF16),
        scratch_shapes=[pltpu.VMEM((2 * nc, CHUNK, CHUNK), F32),
                        pltpu.VMEM((nc, GATE_ROWS, LANE), F32),
                        pltpu.VMEM((nc, vt_rows, CHUNK), BF16),
                        pltpu.VMEM((2 * nc, vt_rows, HEAD_DIM), F32),
                        pltpu.VMEM((2 * nc, vt_rows, HEAD_DIM), BF16),
                        pltpu.VMEM((2, LANE, LANE), F32),
                        pltpu.VMEM((2, LANE, LANE), F32),
                        pltpu.VMEM((2, LANE, LANE), F32),
                        pltpu.VMEM((2, LANE, LANE), F32)],
        compiler_params=_cparams(("parallel", "arbitrary")),
        name="mlstm",
    )(z, z, z, z, zg, gate_b_all, norm_g_all)


def _sg_kernel(u_ref, v_ref, lng_ref, lnb_ref, w_ref, bs_ref, o_ref, *, n_chunks):
    def body(c, carry):
        sl = pl.ds(pl.multiple_of(c * CHUNK, CHUNK), CHUNK)
        u = _gelu_tanh(u_ref[sl, :].astype(F32))
        v = _layernorm(_gelu_tanh(v_ref[sl, :].astype(F32)), lng_ref[...], lnb_ref[...]).astype(BF16)
        for g in range(SG_GROUPS):
            gs = slice(g * LANE, (g + 1) * LANE)
            mixed = jnp.dot(w_ref[g], v[:, gs], preferred_element_type=F32) + bs_ref[g]
            o_ref[sl, gs] = (u[:, gs] * mixed).astype(BF16)
        return carry

    lax.fori_loop(0, n_chunks, body, 0)


def _spatial_gating(z, ln_g_all, ln_b_all, w_s_all, b_s_all, layer, n_rows, tr):
    b = z.shape[0]
    blk = lambda off: pl.BlockSpec((None, tr, GROUP_W), lambda bb, i: (bb, i, off // GROUP_W))
    per_layer = lambda *_: (layer, 0, 0, 0)
    return pl.pallas_call(
        functools.partial(_sg_kernel, n_chunks=tr // CHUNK),
        grid=(b, n_rows // tr),
        in_specs=[blk(COL_SU), blk(COL_SV),
                  _layer_row(layer, GROUP_W), _layer_row(layer, GROUP_W),
                  pl.BlockSpec((None, SG_GROUPS, CHUNK, CHUNK), per_layer),
                  pl.BlockSpec((None, SG_GROUPS, CHUNK, LANE), per_layer)],
        out_specs=pl.BlockSpec((None, tr, GROUP_W), lambda bb, i: (bb, i, 0)),
        out_shape=jax.ShapeDtypeStruct((b, n_rows, GROUP_W), BF16),
        compiler_params=_cparams(("parallel", "parallel")),
        name="spatial_gating",
    )(z, z, ln_g_all, ln_b_all, w_s_all, b_s_all)


def _conf_kernel(a_ref, g_ref, ap_ref, gp_ref, an_ref, gn_ref, w_ref, b_ref, lng_ref, lnb_ref, o_ref,
                 y_scr, ysh_scr, cv_scr, *, tc, seq_starts, seq_ends):
    i = pl.program_id(1)
    has_prev = functools.reduce(jnp.logical_and, [i != s for s in seq_starts])
    has_next = functools.reduce(jnp.logical_and, [i != e for e in seq_ends])
    H = CONV_HALO

    def glu(a, g):
        return a[...].astype(F32) * _sigmoid(g[...].astype(F32))

    y_scr[H:H + tc, :] = glu(a_ref, g_ref)
    y_scr[0:H, :] = jnp.where(has_prev, glu(ap_ref, gp_ref), 0.0)
    y_scr[H + tc:, :] = jnp.where(has_next, glu(an_ref, gn_ref), 0.0)
    n_sh = ysh_scr.shape[1]
    for j in range(SUBLANES):
        ysh_scr[j] = y_scr[j:j + n_sh, :]
    rb = 64
    base = H - (CONV_W // 2)
    for cb in range(GROUP_W // LANE):
        cs = slice(cb * LANE, (cb + 1) * LANE)
        for r in range(tc // rb):
            acc = jnp.zeros((rb, LANE), F32)
            for k in range(CONV_W):
                a, j = divmod(base + k, SUBLANES)
                acc = acc + ysh_scr[j, r * rb + SUBLANES * a:r * rb + SUBLANES * a + rb, cs] * w_ref[k:k + 1, cs]
            cv_scr[r * rb:(r + 1) * rb, cs] = acc
    y = _layernorm(cv_scr[...] + b_ref[...], lng_ref[...], lnb_ref[...])
    o_ref[...] = (y * _sigmoid(y)).astype(BF16)


def _conformer(z, w_dw_all, b_dw_all, ln_g_all, ln_b_all, layer, t_lat, n_rows, tc):
    b = z.shape[0]
    hb = tc // CONV_HALO
    n_tiles = n_rows // tc
    n_halo = z.shape[1] // CONV_HALO
    ca, cg = COL_CA // GROUP_W, COL_CG // GROUP_W
    cur = lambda cblk: pl.BlockSpec((None, tc, GROUP_W), lambda bb, i: (bb, i, cblk))
    prev = lambda cblk: pl.BlockSpec((None, CONV_HALO, GROUP_W),
                                     lambda bb, i: (bb, jnp.maximum(i * hb - 1, 0), cblk))
    nxt = lambda cblk: pl.BlockSpec((None, CONV_HALO, GROUP_W),
                                    lambda bb, i: (bb, jnp.minimum((i + 1) * hb, n_halo - 1), cblk))
    seq_starts = (0, t_lat // tc)
    seq_ends = (t_lat // tc - 1, z.shape[1] // tc - 1)
    return pl.pallas_call(
        functools.partial(_conf_kernel, tc=tc, seq_starts=seq_starts, seq_ends=seq_ends),
        grid=(b, n_tiles),
        in_specs=[cur(ca), cur(cg), prev(ca), prev(cg), nxt(ca), nxt(cg),
                  pl.BlockSpec((None, CONV_W, GROUP_W), lambda *_: (layer, 0, 0)), _layer_row(layer, GROUP_W),
                  _layer_row(layer, GROUP_W), _layer_row(layer, GROUP_W)],
        out_specs=pl.BlockSpec((None, tc, GROUP_W), lambda bb, i: (bb, i, 0)),
        out_shape=jax.ShapeDtypeStruct((b, n_rows, GROUP_W), BF16),
        scratch_shapes=[pltpu.VMEM((tc + 2 * CONV_HALO, GROUP_W), F32),
                        pltpu.VMEM((SUBLANES, tc + 2 * CONV_HALO - SUBLANES, GROUP_W), F32),
                        pltpu.VMEM((tc, GROUP_W), F32)],
        compiler_params=_cparams(("parallel", "parallel")),
        name="conformer_conv",
    )(z, z, z, z, z, z, w_dw_all, b_dw_all, ln_g_all, ln_b_all)


def _attn_prep_kernel(q_ref, k_ref, v_ref, cos_ref, sin_ref, qg_ref, kg_ref, qo_ref, ko_ref, vo_ref):
    cosf = cos_ref[...]
    sinf = sin_ref[...]

    def norm_rope(x, g, scale):
        y = _rms(x.astype(F32), g)
        return (y * cosf + pltpu.roll(y, HEAD_DIM // 2, 1) * sinf) * scale

    q_scale = HEAD_DIM ** -0.5 * math.log2(math.e)
    for h in range(ATT_Q_HEADS):
        hs = slice(h * HEAD_DIM, (h + 1) * HEAD_DIM)
        qo_ref[:, hs] = norm_rope(q_ref[:, hs], qg_ref[...], q_scale).astype(BF16)
    ones = jnp.ones((v_ref.shape[0], HEAD_DIM), BF16)
    for h in range(ATT_KV_HEADS):
        hs = slice(h * HEAD_DIM, (h + 1) * HEAD_DIM)
        ko_ref[:, hs] = norm_rope(k_ref[:, hs], kg_ref[...], 1.0).astype(BF16)
        vo_ref[:, 2 * h * HEAD_DIM:(2 * h + 1) * HEAD_DIM] = v_ref[:, hs]
        vo_ref[:, (2 * h + 1) * HEAD_DIM:(2 * h + 2) * HEAD_DIM] = ones


def _attn_prep(z, cosf, sinf, qg_all, kg_all, layer, tr):
    b, tt, _ = z.shape
    qw, kw = ATT_Q_HEADS * HEAD_DIM, ATT_KV_HEADS * HEAD_DIM
    return pl.pallas_call(
        _attn_prep_kernel,
        grid=(b, tt // tr),
        in_specs=[pl.BlockSpec((None, tr, qw), lambda bb, i: (bb, i, COL_AQ // qw)),
                  pl.BlockSpec((None, tr, kw), lambda bb, i: (bb, i, COL_AK // kw)),
                  pl.BlockSpec((None, tr, kw), lambda bb, i: (bb, i, COL_AV // kw)),
                  pl.BlockSpec((tr, HEAD_DIM), lambda bb, i: (i, 0)),
                  pl.BlockSpec((tr, HEAD_DIM), lambda bb, i: (i, 0)),
                  _layer_row(layer, HEAD_DIM), _layer_row(layer, HEAD_DIM)],
        out_specs=[pl.BlockSpec((None, tr, qw), lambda bb, i: (bb, i, 0)),
                   pl.BlockSpec((None, tr, kw), lambda bb, i: (bb, i, 0)),
                   pl.BlockSpec((None, tr, 2 * kw), lambda bb, i: (bb, i, 0))],
        out_shape=[jax.ShapeDtypeStruct((b, tt, qw), BF16),
                   jax.ShapeDtypeStruct((b, tt, kw), BF16),
                   jax.ShapeDtypeStruct((b, tt, 2 * kw), BF16)],
        compiler_params=_cparams(("parallel", "parallel")),
        name="attn_prep",
    )(z, z, z, cosf, sinf, qg_all, kg_all)


def _attn_kernel(q_ref, k_ref, v_ref, o_ref, *, tq, sub):
    k = k_ref[...]
    vaug = v_ref[...]
    for g in range(ATT_Q_HEADS // ATT_KV_HEADS):
        hs = slice(g * HEAD_DIM, (g + 1) * HEAD_DIM)
        for r in range(tq // sub):
            rs = slice(r * sub, (r + 1) * sub)
            s = lax.dot_general(q_ref[rs, hs], k, (((1,), (1,)), ((), ())), preferred_element_type=F32)
            p = jnp.exp2(s - jnp.max(s, axis=-1, keepdims=True)).astype(BF16)
            res = jnp.dot(p, vaug, preferred_element_type=F32)
            o_ref[rs, hs] = (res[:, :HEAD_DIM] / res[:, HEAD_DIM:]).astype(BF16)


def _attention(qr, kr, vaug, q_row0, n_q, k_row0, n_k, tq):
    b = qr.shape[0]
    gw = (ATT_Q_HEADS // ATT_KV_HEADS) * HEAD_DIM
    assert q_row0 % tq == 0 and n_q % tq == 0 and k_row0 % n_k == 0
    qb, kb = q_row0 // tq, k_row0 // n_k
    return pl.pallas_call(
        functools.partial(_attn_kernel, tq=tq, sub=min(tq, 128)),
        grid=(b, ATT_KV_HEADS, n_q // tq),
        in_specs=[pl.BlockSpec((None, tq, gw), lambda bb, h, i: (bb, i + qb, h)),
                  pl.BlockSpec((None, n_k, HEAD_DIM), lambda bb, h, i: (bb, kb, h)),
                  pl.BlockSpec((None, n_k, 2 * HEAD_DIM), lambda bb, h, i: (bb, kb, h))],
        out_specs=pl.BlockSpec((None, tq, gw), lambda bb, h, i: (bb, i, h)),
        out_shape=jax.ShapeDtypeStruct((b, n_q, ATT_Q_HEADS * HEAD_DIM), BF16),
        compiler_params=_cparams(("parallel", "parallel", "arbitrary")),
        name="attention",
    )(qr, kr, vaug)


def _out_proj_kernel(ya_ref, yb_ref, yc_ref, yd_ref, w_ref, x_ref, gate_ref, pg_ref, o_ref):
    acc = jnp.dot(ya_ref[...], w_ref[0 * GROUP_W:1 * GROUP_W, :], preferred_element_type=F32)
    acc += jnp.dot(yb_ref[...], w_ref[1 * GROUP_W:2 * GROUP_W, :], preferred_element_type=F32)
    acc += jnp.dot(yc_ref[...], w_ref[2 * GROUP_W:3 * GROUP_W, :], preferred_element_type=F32)
    acc += jnp.dot(yd_ref[...], w_ref[3 * GROUP_W:4 * GROUP_W, :], preferred_element_type=F32)
    o_ref[...] = x_ref[...] + gate_ref[...] * _rms(acc, pg_ref[...])


def _out_proj(ys, y_row_offs, w_out_all, xs, mod, mod_row, post_g_all, layer, tm):
    b, t, d = xs.shape
    assert all(off % tm == 0 for off in y_row_offs)
    yspecs = [pl.BlockSpec((None, tm, GROUP_W), functools.partial(lambda bb, i, o: (bb, i + o, 0), o=off // tm))
              for off in y_row_offs]
    return pl.pallas_call(
        _out_proj_kernel,
        grid=(b, t // tm),
        in_specs=[*yspecs,
                  pl.BlockSpec((None,) + w_out_all.shape[1:], lambda *_: (layer, 0, 0)),
                  pl.BlockSpec((None, tm, d), lambda bb, i: (bb, i, 0)),
                  _mod_spec(layer, 2, d, mod_row),
                  _layer_row(layer, d)],
        out_specs=pl.BlockSpec((None, tm, d), lambda bb, i: (bb, i, 0)),
        out_shape=jax.ShapeDtypeStruct((b, t, d), F32),
        compiler_params=_cparams(("parallel", "parallel")),
        name="out_proj",
    )(*ys, w_out_all, xs, mod, post_g_all)


def _ffn_kernel(x_ref, xp_ref, xn_ref, sh_ref, sc_ref, gate_ref, pre_g_ref, post_g_ref, wg_ref, wu_ref,
                cw_ref, cb_ref, wd_ref, o_ref, h_scr, acc_scr, *, tm, n_tiles, n_ff, seq_len):
    i = pl.program_id(1)
    j = pl.program_id(2)
    H = FFN_HALO
    n_ext = tm + 2 * H
    n_blk = 4 if tm % 64 == 0 else 1
    rb = tm // n_blk
    ext_bounds = [0] + [H + r * rb for r in range(1, n_blk)] + [n_ext]

    def pre(x):
        return _rms(x, pre_g_ref[...]) * (1.0 + sc_ref[...]) + sh_ref[...]

    def up_first():
        gs, us = [], []
        for r in range(n_blk):
            lo, hi = ext_bounds[r], ext_bounds[r + 1]
            m_lo, m_hi = max(lo, H), min(hi, H + tm)
            pieces = []
            if r == 0:
                pieces.append(jnp.where(i > 0, pre(xp_ref[...]), 0.0))
            pieces.append(pre(x_ref[m_lo - H:m_hi - H, :]))
            if r == n_blk - 1:
                pieces.append(jnp.where(i < n_tiles - 1, pre(xn_ref[...]), 0.0))
            hc = (pieces[0] if len(pieces) == 1 else jnp.concatenate(pieces, axis=0)).astype(BF16)
            h_scr[lo:hi, :] = hc
            gs.append(jnp.dot(hc, wg_ref[...], preferred_element_type=F32))
            us.append(jnp.dot(hc[m_lo - lo:m_hi - lo, :], wu_ref[...], preferred_element_type=F32))
        return jnp.concatenate(gs, axis=0), jnp.concatenate(us, axis=0)

    def step(first, last):
        if first:
            g_ext, u = up_first()
        else:
            g_ext = jnp.dot(h_scr[...], wg_ref[...], preferred_element_type=F32)
            u = jnp.dot(h_scr[H:H + tm, :], wu_ref[...], preferred_element_type=F32)
        g_prev = pltpu.roll(g_ext, 1, 0)[H:H + tm]
        g_next = pltpu.roll(g_ext, n_ext - 1, 0)[H:H + tm]
        if seq_len < tm:
            pos = lax.broadcasted_iota(jnp.int32, (tm, 1), 0) % seq_len
            g_prev = jnp.where(pos == 0, 0.0, g_prev)
            g_next = jnp.where(pos == seq_len - 1, 0.0, g_next)
        conv = g_prev * cw_ref[0:1, :] + g_ext[H:H + tm] * cw_ref[1:2, :] + g_next * cw_ref[2:3, :] + cb_ref[...]
        act = ((conv * _sigmoid(conv)) * u).astype(BF16)
        if not last:
            down = jnp.dot(act, wd_ref[...], preferred_element_type=F32)
            acc_scr[...] = down if first else acc_scr[...] + down
            return
        for r in range(n_blk):
            rs = slice(r * rb, (r + 1) * rb)
            down = jnp.dot(act[rs, :], wd_ref[...], preferred_element_type=F32)
            acc = down if first else acc_scr[rs, :] + down
            o_ref[rs, :] = x_ref[rs, :] + gate_ref[...] * _rms(acc, post_g_ref[...])

    @pl.when(j == 0)
    def _():
        step(True, n_ff == 1)

    if n_ff > 2:
        @pl.when(jnp.logical_and(j > 0, j < n_ff - 1))
        def _():
            step(False, False)

    if n_ff > 1:
        @pl.when(j == n_ff - 1)
        def _():
            step(False, True)


def _ffn(xs, mod, mod_row, pre_g_all, post_g_all, w_up_all, cv_w_all, cv_b_all, w_down_all, layer, tm, seq_len, tf=512):
    b, t, d = xs.shape
    assert seq_len == t or (tm % seq_len == 0 and t % tm == 0)
    d_ff = w_down_all.shape[1]
    n_ff = d_ff // tf
    n_tiles = t // tm
    hb = tm // FFN_HALO
    n_halo = t // FFN_HALO
    return pl.pallas_call(
        functools.partial(_ffn_kernel, tm=tm, n_tiles=n_tiles, n_ff=n_ff, seq_len=seq_len),
        grid=(b, n_tiles, n_ff),
        in_specs=[pl.BlockSpec((None, tm, d), lambda bb, i, j: (bb, i, 0)),
                  pl.BlockSpec((None, FFN_HALO, d), lambda bb, i, j: (bb, jnp.maximum(i * hb - 1, 0), 0)),
                  pl.BlockSpec((None, FFN_HALO, d), lambda bb, i, j: (bb, jnp.minimum((i + 1) * hb, n_halo - 1), 0)),
                  _mod_spec(layer, 3, d, mod_row), _mod_spec(layer, 4, d, mod_row), _mod_spec(layer, 5, d, mod_row),
                  _layer_row(layer, d), _layer_row(layer, d),
                  pl.BlockSpec((None, d, tf), lambda bb, i, j: (layer, 0, j)),
                  pl.BlockSpec((None, d, tf), lambda bb, i, j: (layer, 0, j + n_ff)),
                  pl.BlockSpec((None, cv_w_all.shape[1], tf), lambda bb, i, j: (layer, 0, j)),
                  pl.BlockSpec((None, 1, tf), lambda bb, i, j: (layer, 0, j)),
                  pl.BlockSpec((None, tf, d), lambda bb, i, j: (layer, j, 0))],
        out_specs=pl.BlockSpec((None, tm, d), lambda bb, i, j: (bb, i, 0)),
        out_shape=jax.ShapeDtypeStruct((b, t, d), F32),
        scratch_shapes=[pltpu.VMEM((tm + 2 * FFN_HALO, d), BF16), pltpu.VMEM((tm, d), F32)],
        compiler_params=_cparams(("parallel", "parallel", "arbitrary")),
        name="conv_glu_ffn",
    )(xs, xs, xs, mod, mod, mod, pre_g_all, post_g_all, w_up_all, w_up_all, cv_w_all, cv_b_all, w_down_all)


def _pack_w_in(w_in):
    depth, d, _ = w_in.shape
    main = jnp.concatenate([w_in[..., :4 * GROUP_W], w_in[..., 4 * GROUP_W + N_GATE:]], axis=-1).astype(BF16)
    gates = w_in[..., 4 * GROUP_W:4 * GROUP_W + N_GATE]
    gates = jnp.concatenate([gates, jnp.zeros((depth, d, LANE - N_GATE), w_in.dtype)], axis=-1).astype(BF16)
    return main, gates


def _rope_tables(t_lat, t_ctx):
    t = jnp.arange(t_lat)
    row = (t // GRID_W).astype(F32)
    colp = (t % GRID_W).astype(F32)
    axis_dim = HEAD_DIM // 2
    inv = jnp.power(ROPE_THETA, -jnp.arange(0, axis_dim, 2, dtype=F32) / axis_dim)
    ang = jnp.concatenate([row[:, None] * inv, colp[:, None] * inv], axis=-1)
    cos, sin = jnp.cos(ang), jnp.sin(ang)
    cosf = jnp.concatenate([cos, cos], axis=-1)
    sinf = jnp.concatenate([-sin, sin], axis=-1)
    cosf = jnp.concatenate([cosf, jnp.ones((t_ctx, HEAD_DIM), F32)], axis=0)
    sinf = jnp.concatenate([sinf, jnp.zeros((t_ctx, HEAD_DIM), F32)], axis=0)
    return cosf, sinf


def kernel(x, c, ctx, c_ctx, w_ada, b_ada, pre_mix_g, post_mix_g, w_in, ml_gate_b, ml_norm_g, sg_ln_g, sg_ln_b,
           sg_w, sg_b, cv_w, cv_b, cv_ln_g, cv_ln_b, at_qn_g, at_kn_g, w_out, pre_ffn_g, post_ffn_g, w_up,
           ffn_cv_w, ffn_cv_b, w_down):
    b, t_lat, d = x.shape
    t_ctx = ctx.shape[1]
    tt = t_lat + t_ctx
    depth = w_ada.shape[0]
    tm_in = min(1024, t_lat)
    tm_lat = min(512, t_lat)
    assert t_lat % tm_in == 0 and t_lat % t_ctx == 0 and t_ctx % CHUNK == 0 and t_ctx % 256 == 0

    n_rows = -(-(b + 1) // 8) * 8
    s_rows = jnp.concatenate([c, c_ctx[None, :], jnp.zeros((n_rows - b - 1, d), F32)], axis=0)
    mod = _ada(s_rows, w_ada, b_ada).reshape(depth, n_rows, 1, 6 * d)
    cosf, sinf = _rope_tables(t_lat, t_ctx)

    rows = lambda a: a.reshape(depth, 1, -1)
    w_main, w_gate = _pack_w_in(w_in)
    w_out_b, w_up_b, w_down_b = w_out.astype(BF16), w_up.astype(BF16), w_down.astype(BF16)
    gate_b = rows(jnp.concatenate([ml_gate_b, jnp.zeros((depth, LANE - N_GATE), F32)], axis=1))
    sg_w_b = sg_w.astype(BF16)
    sg_b_full = jnp.broadcast_to(sg_b[..., None], (depth, SG_GROUPS, CHUNK, LANE))
    pre_mix, post_mix, pre_ffn, post_ffn = rows(pre_mix_g), rows(post_mix_g), rows(pre_ffn_g), rows(post_ffn_g)
    ml_norm, sg_g, sg_bb, cv_bias, cv_g, cv_bb = (rows(ml_norm_g), rows(sg_ln_g), rows(sg_ln_b), rows(cv_b),
                                                  rows(cv_ln_g), rows(cv_ln_b))
    qn_g, kn_g, ffn_bias = rows(at_qn_g), rows(at_kn_g), rows(ffn_cv_b)

    xc = ctx
    for i in range(depth):
        need_ctx = i < depth - 1
        z, zg = _in_proj(x, xc, mod, pre_mix, w_main, w_gate, i, tm_in)
        ya = _mlstm(z, zg, gate_b, ml_norm, i, t_lat, t_ctx)

        mix_rows = tt if need_ctx else t_lat
        sg_tr = 2176 if (need_ctx and tt % 2176 == 0) else (2048 if t_lat % 2048 == 0 else 256)
        sg_tr = sg_tr if mix_rows % sg_tr == 0 else 256
        yb = _spatial_gating(z, sg_g, sg_bb, sg_w_b, sg_b_full, i, mix_rows, sg_tr)
        yc = _conformer(z, cv_w, cv_bias, cv_g, cv_bb, i, t_lat, mix_rows, 256)

        prep_tr = 1088 if tt % 1088 == 0 else 256
        qr, kr, vaug = _attn_prep(z, cosf, sinf, qn_g, kn_g, i, prep_tr)
        yd = _attention(qr, kr, vaug, 0, t_lat, 0, tt, min(512, t_lat))

        x = _out_proj((ya, yb, yc, yd), (0, 0, 0, 0), w_out_b, x, mod, None, post_mix, i, tm_lat)
        x = _ffn(x, mod, None, pre_ffn, post_ffn, w_up_b, ffn_cv_w, ffn_bias, w_down_b, i, tm_lat, t_lat)
        if need_ctx:
            yd_c = _attention(qr, kr, vaug, t_lat, t_ctx, t_lat, t_ctx, t_ctx)
            xc = _out_proj((ya, yb, yc, yd_c), (t_lat, t_lat, t_lat, 0), w_out_b, xc, mod, b, post_mix, i, t_ctx)
            tm_ctx = t_ctx * max(1, min(b, 512 // t_ctx))
            tm_ctx = tm_ctx if (b * t_ctx) % tm_ctx == 0 else t_ctx
            xc = _ffn(xc.reshape(1, b * t_ctx, d), mod, b, pre_ffn, post_ffn, w_up_b, ffn_cv_w, ffn_bias, w_down_b,
                      i, tm_ctx, t_ctx).reshape(b, t_ctx, d)
    return x
```

```python
import functools
import math

import jax
import jax.numpy as jnp
from jax import lax
from jax.experimental import pallas as pl
from jax.experimental.pallas import tpu as pltpu

F32 = jnp.float32
BF16 = jnp.bfloat16

EPS = 1e-6
LANE = 128
SUBLANES = 8
HEAD_DIM = 128
GROUP_W = 512
ML_HEADS = 4
CHUNK = 128
SG_GROUPS = 4
CONV_W = 31
CONV_HALO = 16
ATT_Q_HEADS = 4
ATT_KV_HEADS = 2
GRID_W = 64
ROPE_THETA = 10000.0
FFN_HALO = 16
N_GATE = 16
FFN_TF = 512
GATE_ROWS = 32

COL_MQ, COL_MK, COL_MV, COL_MO = 0, 512, 1024, 1536
COL_SU, COL_SV = 2048, 2560
COL_CA, COL_CG = 3072, 3584
COL_AQ, COL_AK, COL_AV = 4096, 4608, 4864
NZ = 5120

VMEM_LIMIT = 52 * 1024 * 1024


def _cparams(sem):
    return pltpu.CompilerParams(dimension_semantics=sem, vmem_limit_bytes=VMEM_LIMIT)


def _layer_row(layer, n):
    return pl.BlockSpec((None, 1, n), lambda *_: (layer, 0, 0))


def _mod_spec(layer, k, d, row=None):
    if row is None:
        return pl.BlockSpec((None, None, 1, d), lambda bb, *_: (layer, bb, 0, k))
    return pl.BlockSpec((None, None, 1, d), lambda *_: (layer, row, 0, k))


def _rms(x, g):
    ms = jnp.mean(x * x, axis=-1, keepdims=True)
    return x * lax.rsqrt(ms + EPS) * g


def _layernorm(x, g, b):
    mu = jnp.mean(x, axis=-1, keepdims=True)
    d = x - mu
    var = jnp.mean(d * d, axis=-1, keepdims=True)
    return d * lax.rsqrt(var + EPS) * g + b


def _sigmoid(x):
    return 1.0 / (1.0 + jnp.exp(-x))


def _gelu_tanh(x):
    c = math.sqrt(2.0 / math.pi)
    return x * (0.5 * (1.0 + jnp.tanh(c * (x + 0.044715 * (x * x * x)))))


def _log_sigmoid(x):
    return jnp.minimum(x, 0.0) - jnp.log1p(jnp.exp(-jnp.abs(x)))


def _ada_kernel(s_ref, w_ref, b_ref, o_ref):
    s = s_ref[...]
    s = s * _sigmoid(s)
    o_ref[...] = jnp.dot(s.astype(BF16), w_ref[...].astype(BF16), preferred_element_type=F32) + b_ref[...]


def _ada(s_rows, w_ada, b_ada, tn=1024):
    depth, d, n = w_ada.shape
    r = s_rows.shape[0]
    return pl.pallas_call(
        _ada_kernel,
        grid=(depth, n // tn),
        in_specs=[pl.BlockSpec((r, d), lambda l, j: (0, 0)),
                  pl.BlockSpec((None, d, tn), lambda l, j: (l, 0, j)),
                  pl.BlockSpec((None, 1, tn), lambda l, j: (l, 0, j))],
        out_specs=pl.BlockSpec((None, r, tn), lambda l, j: (l, 0, j)),
        out_shape=jax.ShapeDtypeStruct((depth, r, n), F32),
        compiler_params=_cparams(("parallel", "parallel")),
        name="ada_mod",
    )(s_rows, w_ada, b_ada.reshape(depth, 1, n))


def _in_proj_kernel(x_ref, xc_ref, sh_ref, sc_ref, shc_ref, scc_ref, g_ref, w_ref, wg_ref, o_ref, og_ref, h_scr,
                    *, tm, t_ctx, n_lat_tiles, rows_per_step):
    i = pl.program_id(1)
    first_col = pl.program_id(2) == 0
    is_lat = i < n_lat_tiles

    def pre(x, sh, sc):
        return (_rms(x, g_ref[...]) * (1.0 + sc) + sh).astype(BF16)

    def first_step(x_rows, sh, sc, n_rows):
        for r in range(n_rows // rows_per_step):
            sl = slice(r * rows_per_step, (r + 1) * rows_per_step)
            h = pre(x_rows[sl, :], sh[...], sc[...])
            h_scr[sl, :] = h
            o_ref[sl, :] = jnp.dot(h, w_ref[...], preferred_element_type=F32).astype(BF16)
            og_ref[sl, :] = jnp.dot(h, wg_ref[...], preferred_element_type=F32)

    @pl.when(jnp.logical_and(first_col, is_lat))
    def _():
        first_step(x_ref, sh_ref, sc_ref, tm)

    @pl.when(jnp.logical_and(jnp.logical_not(first_col), is_lat))
    def _():
        o_ref[...] = jnp.dot(h_scr[...], w_ref[...], preferred_element_type=F32).astype(BF16)

    @pl.when(jnp.logical_and(first_col, jnp.logical_not(is_lat)))
    def _():
        first_step(xc_ref, shc_ref, scc_ref, t_ctx)
        if t_ctx < tm:
            og_ref[t_ctx:, :] = jnp.zeros((tm - t_ctx, og_ref.shape[1]), F32)

    @pl.when(jnp.logical_and(jnp.logical_not(first_col), jnp.logical_not(is_lat)))
    def _():
        o_ref[0:t_ctx, :] = jnp.dot(h_scr[0:t_ctx, :], w_ref[...], preferred_element_type=F32).astype(BF16)

    if t_ctx < tm:
        @pl.when(jnp.logical_not(is_lat))
        def _():
            o_ref[t_ctx:, :] = jnp.zeros((tm - t_ctx, o_ref.shape[1]), BF16)


def _in_proj(x, xc, mod, g_all, w_all, wg_all, layer, tm, tn=1280):
    b, t_lat, d = x.shape
    t_ctx = xc.shape[1]
    tt = t_lat + t_ctx
    nz = w_all.shape[2]
    n_lat_tiles = t_lat // tm
    assert t_lat % tm == 0 and t_ctx <= tm
    return pl.pallas_call(
        functools.partial(_in_proj_kernel, tm=tm, t_ctx=t_ctx, n_lat_tiles=n_lat_tiles, rows_per_step=min(tm, 256)),
        grid=(b, n_lat_tiles + 1, nz // tn),
        in_specs=[pl.BlockSpec((None, tm, d), lambda bb, i, j: (bb, jnp.minimum(i, n_lat_tiles - 1), 0)),
                  pl.BlockSpec((None, t_ctx, d), lambda bb, i, j: (bb, 0, 0)),
                  _mod_spec(layer, 0, d), _mod_spec(layer, 1, d),
                  _mod_spec(layer, 0, d, row=b), _mod_spec(layer, 1, d, row=b),
                  _layer_row(layer, d),
                  pl.BlockSpec((None, d, tn), lambda bb, i, j: (layer, 0, j)),
                  pl.BlockSpec((None, d, LANE), lambda *_: (layer, 0, 0))],
        out_specs=[pl.BlockSpec((None, tm, tn), lambda bb, i, j: (bb, i, j)),
                   pl.BlockSpec((None, tm, LANE), lambda bb, i, j: (bb, i, 0))],
        out_shape=[jax.ShapeDtypeStruct((b, tt, nz), BF16), jax.ShapeDtypeStruct((b, tt, LANE), F32)],
        scratch_shapes=[pltpu.VMEM((tm, d), BF16)],
        compiler_params=_cparams(("parallel", "parallel", "arbitrary")),
        name="in_proj",
    )(x, xc, mod, mod, mod, mod, g_all, w_all, wg_all)


def _prefix_lanes(x, op, fill):
    lane = lax.broadcasted_iota(jnp.int32, x.shape, 1)
    sh = 1
    while sh < x.shape[1]:
        x = op(x, jnp.where(lane >= sh, pltpu.roll(x, sh, 1), fill))
        sh *= 2
    return x


def _suffix_lanes(x, op, fill):
    n = x.shape[1]
    lane = lax.broadcasted_iota(jnp.int32, x.shape, 1)
    sh = 1
    while sh < n:
        x = op(x, jnp.where(lane < n - sh, pltpu.roll(x, n - sh, 1), fill))
        sh *= 2
    return x


def _mlstm_kernel(q_ref, k_ref, v_ref, o_ref, gt_ref, gb_ref, ng_ref, out_ref,
                  rb_scr, gr_scr, vt_scr, u_scr, cp_scr, st_scr, sa_scr, sc_scr, sm_scr, *, n_lat, n_ctx):
    L = CHUNK
    nc = n_lat + n_ctx
    VT_ROWS = vt_scr.shape[1]
    head = pl.program_id(1)
    row = lax.broadcasted_iota(jnp.int32, (L, L), 0)
    col = lax.broadcasted_iota(jnp.int32, (L, L), 1)
    lower = col <= row
    upper = col >= row
    k_scale = HEAD_DIM ** -0.5
    HALF = GATE_ROWS // 2

    def rows_of(c):
        return pl.ds(pl.multiple_of(c * L, L), L)

    def step_of(c, direction):
        if direction == 0:
            return jnp.where(c < n_lat, c + n_ctx, c - n_lat)
        return nc - 1 - c

    @pl.when(head == 0)
    def _():
        lower_f = lower.astype(F32)
        upper_f = upper.astype(F32)
        n_all = nc * HALF
        rg = lax.broadcasted_iota(jnp.int32, (n_all, L), 0) % HALF
        is_cum = (rg % 8) >= 4
        is_bwd = rg >= 8

        def to_rows(c, carry):
            gr_scr[c, 0:HALF, :] = (gt_ref[rows_of(c), :] + gb_ref[...]).T[0:HALF, :]
            return carry

        lax.fori_loop(0, nc, to_rows, 0, unroll=2)

        G = gr_scr[:, 0:HALF, :].reshape(n_all, L)
        Gf = _log_sigmoid(G)
        Bf = jnp.dot(Gf, upper_f, precision=lax.Precision.HIGHEST, preferred_element_type=F32)
        Bb = jnp.dot(Gf, lower_f, precision=lax.Precision.HIGHEST, preferred_element_type=F32)
        Bsel = jnp.where(is_bwd, Bb, Bf)
        R = G - pltpu.roll(Bsel, n_all - 4, 0)
        CM = jnp.where(is_bwd, _suffix_lanes(R, jnp.maximum, -jnp.inf), _prefix_lanes(R, jnp.maximum, -jnp.inf))
        gr_scr[:, 0:HALF, :] = jnp.where(is_cum, Bsel, G).reshape(nc, HALF, L)
        gr_scr[:, HALF:GATE_ROWS, :] = jnp.where(is_cum, pltpu.roll(CM, 4, 0), R).reshape(nc, HALF, L)

    row_g = lax.broadcasted_iota(jnp.int32, (GATE_ROWS, L), 0)
    lane_r = lax.broadcasted_iota(jnp.int32, (1, LANE), 1)

    def gate_row(x, idx):
        return jnp.sum(jnp.where(row_g == idx, x, 0.0), axis=0, keepdims=True)

    st_scr[...] = jnp.zeros_like(st_scr)
    ones_rows = jnp.ones((VT_ROWS - HEAD_DIM, L), F32)

    def summarize(c, carry):
        vt = jnp.concatenate([v_ref[rows_of(c), :].astype(F32).T, ones_rows], axis=0)
        vt_scr[c] = vt.astype(BF16)
        k = k_ref[rows_of(c), :]
        grv = gr_scr[c]
        for direction in (0, 1):
            ci = direction * 8 + head
            b_row = gate_row(grv, ci + 4)
            i_row = gate_row(grv, ci)
            total = b_row[:, L - 1:L] if direction == 0 else b_row[:, 0:1]
            g_row = total - b_row + i_row
            mg = jnp.max(g_row, axis=1, keepdims=True)
            w_row = jnp.exp(g_row - mg) * k_scale
            step = step_of(c, direction)
            u_scr[direction * nc + step] = jnp.dot((vt * w_row).astype(BF16), k, preferred_element_type=F32)
            st_scr[direction, pl.ds(step, 1), :] = jnp.where(lane_r < 64, total, mg)
            rb_scr[direction * nc + c] = jnp.broadcast_to(gate_row(grv, 16 + ci), (L, L)).T
        return carry

    lax.fori_loop(0, nc, summarize, 0, unroll=4)

    lane8 = lax.broadcasted_iota(jnp.int32, (8, LANE), 1)
    for direction in (0, 1):
        xt = st_scr[direction].T
        valid = lane8 < nc
        T = jnp.where(valid, jnp.broadcast_to(xt[0:1, :], (8, LANE)), 0.0)
        G = jnp.where(valid, jnp.broadcast_to(xt[64:65, :], (8, LANE)), -jnp.inf)
        PT = _prefix_lanes(T, jnp.add, 0.0)
        mvec = PT + jnp.maximum(_prefix_lanes(G - PT, jnp.maximum, -jnp.inf), 0.0)
        mprev = jnp.where(lane8 >= 1, pltpu.roll(mvec, 1, 1), 0.0)
        a = jnp.exp(T + mprev - mvec)
        c2 = jnp.exp(G - mvec)
        sa_scr[direction] = jnp.broadcast_to(a[0:1, :], (LANE, LANE)).T
        sc_scr[direction] = jnp.broadcast_to(c2[0:1, :], (LANE, LANE)).T
        sm_scr[direction] = jnp.broadcast_to(mprev[0:1, :], (LANE, LANE)).T

    def scan(s, carry):
        new = []
        for direction in (0, 1):
            ct = carry[direction]
            idx = direction * nc + s
            cp_scr[idx] = ct.astype(BF16)
            a = sa_scr[direction, pl.ds(s, 1), :]
            c2 = sc_scr[direction, pl.ds(s, 1), :]
            new.append(a * ct + c2 * u_scr[idx])
        return tuple(new)

    zero = jnp.zeros((VT_ROWS, HEAD_DIM), F32)
    lax.fori_loop(0, nc, scan, (zero, zero))

    nt_dims = (((1,), (1,)), ((), ()))

    def emit(c, carry):
        q = q_ref[rows_of(c), :]
        vt = vt_scr[c]
        grv = gr_scr[c]
        qk_t = lax.dot_general(k_ref[rows_of(c), :], q, nt_dims, preferred_element_type=F32) * k_scale
        h_t = jnp.zeros((HEAD_DIM, L), F32)
        for direction in (0, 1):
            ci = direction * 8 + head
            step = step_of(c, direction)
            r_col = rb_scr[direction * nc + c]
            cm_row = gate_row(grv, 16 + ci + 4)
            b_row = gate_row(grv, ci + 4)
            p_t = jnp.exp(jnp.where(upper if direction == 0 else lower, r_col - cm_row, -jnp.inf))
            intra = jnp.dot(vt, (qk_t * p_t).astype(BF16), preferred_element_type=F32)
            inter = lax.dot_general(cp_scr[direction * nc + step], q, nt_dims, preferred_element_type=F32)
            mp = sm_scr[direction, pl.ds(step, 1), :]
            mx = jnp.maximum(mp, cm_row)
            e1 = jnp.exp(mp - mx)
            e2 = jnp.exp(cm_row - mx)
            num = e1 * inter[:HEAD_DIM] + e2 * intra[:HEAD_DIM]
            den = e1 * inter[HEAD_DIM:HEAD_DIM + 1] + e2 * intra[HEAD_DIM:HEAD_DIM + 1]
            h_t = h_t + num * (1.0 / jnp.maximum(jnp.abs(den), jnp.exp(-(b_row + mx))))
        mu = jnp.mean(h_t, axis=0, keepdims=True)
        dev = h_t - mu
        var = jnp.mean(dev * dev, axis=0, keepdims=True)
        hn = (dev * lax.rsqrt(var + EPS)).T * ng_ref[...]
        out_ref[rows_of(c), :] = (hn * _sigmoid(o_ref[rows_of(c), :].astype(F32))).astype(BF16)
        return carry

    lax.fori_loop(0, nc, emit, 0, unroll=8)


def _mlstm(z, zg, gate_b_all, norm_g_all, layer, t_lat, t_ctx):
    b, tt, _ = z.shape
    nc = tt // CHUNK
    assert nc <= 64
    vt_rows = HEAD_DIM + 16
    blk = lambda off: pl.BlockSpec((None, tt, LANE), lambda bb, h: (bb, 0, off // LANE + h))
    return pl.pallas_call(
        functools.partial(_mlstm_kernel, n_lat=t_lat // CHUNK, n_ctx=t_ctx // CHUNK),
        grid=(b, ML_HEADS),
        in_specs=[blk(COL_MQ), blk(COL_MK), blk(COL_MV), blk(COL_MO),
                  pl.BlockSpec((None, tt, LANE), lambda bb, h: (bb, 0, 0)),
                  _layer_row(layer, LANE),
                  pl.BlockSpec((None, 1, LANE), lambda bb, h: (layer, 0, h))],
        out_specs=pl.BlockSpec((None, tt, LANE), lambda bb, h: (bb, 0, h)),
        out_shape=jax.ShapeDtypeStruct((b, tt, GROUP_W), BF16),
        scratch_shapes=[pltpu.VMEM((2 * nc, CHUNK, CHUNK), F32),
                        pltpu.VMEM((nc, GATE_ROWS, LANE), F32),
                        pltpu.VMEM((nc, vt_rows, CHUNK), BF16),
                        pltpu.VMEM((2 * nc, vt_rows, HEAD_DIM), F32),
                        pltpu.VMEM((2 * nc, vt_rows, HEAD_DIM), BF16),
                        pltpu.VMEM((2, LANE, LANE), F32),
                        pltpu.VMEM((2, LANE, LANE), F32),
                        pltpu.VMEM((2, LANE, LANE), F32),
                        pltpu.VMEM((2, LANE, LANE), F32)],
        compiler_params=_cparams(("parallel", "arbitrary")),
        name="mlstm",
    )(z, z, z, z, zg, gate_b_all, norm_g_all)


def _sg_kernel(u_ref, v_ref, lng_ref, lnb_ref, w_ref, bs_ref, o_ref, *, n_chunks):
    def body(c, carry):
        sl = pl.ds(pl.multiple_of(c * CHUNK, CHUNK), CHUNK)
        u = _gelu_tanh(u_ref[sl, :].astype(F32))
        v = _layernorm(_gelu_tanh(v_ref[sl, :].astype(F32)), lng_ref[...], lnb_ref[...]).astype(BF16)
        for g in range(SG_GROUPS):
            gs = slice(g * LANE, (g + 1) * LANE)
            mixed = jnp.dot(w_ref[g], v[:, gs], preferred_element_type=F32) + bs_ref[g]
            o_ref[sl, gs] = (u[:, gs] * mixed).astype(BF16)
        return carry

    lax.fori_loop(0, n_chunks, body, 0)


def _spatial_gating(z, ln_g_all, ln_b_all, w_s_all, b_s_all, layer, n_rows, tr):
    b = z.shape[0]
    blk = lambda off: pl.BlockSpec((None, tr, GROUP_W), lambda bb, i: (bb, i, off // GROUP_W))
    per_layer = lambda *_: (layer, 0, 0, 0)
    return pl.pallas_call(
        functools.partial(_sg_kernel, n_chunks=tr // CHUNK),
        grid=(b, n_rows // tr),
        in_specs=[blk(COL_SU), blk(COL_SV),
                  _layer_row(layer, GROUP_W), _layer_row(layer, GROUP_W),
                  pl.BlockSpec((None, SG_GROUPS, CHUNK, CHUNK), per_layer),
                  pl.BlockSpec((None, SG_GROUPS, CHUNK, LANE), per_layer)],
        out_specs=pl.BlockSpec((None, tr, GROUP_W), lambda bb, i: (bb, i, 0)),
        out_shape=jax.ShapeDtypeStruct((b, n_rows, GROUP_W), BF16),
        compiler_params=_cparams(("parallel", "parallel")),
        name="spatial_gating",
    )(z, z, ln_g_all, ln_b_all, w_s_all, b_s_all)


def _conf_kernel(a_ref, g_ref, ap_ref, gp_ref, an_ref, gn_ref, w_ref, b_ref, lng_ref, lnb_ref, o_ref,
                 y_scr, ysh_scr, cv_scr, *, tc, seq_starts, seq_ends):
    i = pl.program_id(1)
    has_prev = functools.reduce(jnp.logical_and, [i != s for s in seq_starts])
    has_next = functools.reduce(jnp.logical_and, [i != e for e in seq_ends])
    H = CONV_HALO

    def glu(a, g):
        return a[...].astype(F32) * _sigmoid(g[...].astype(F32))

    y_scr[H:H + tc, :] = glu(a_ref, g_ref)
    y_scr[0:H, :] = jnp.where(has_prev, glu(ap_ref, gp_ref), 0.0)
    y_scr[H + tc:, :] = jnp.where(has_next, glu(an_ref, gn_ref), 0.0)
    n_sh = ysh_scr.shape[1]
    for j in range(SUBLANES):
        ysh_scr[j] = y_scr[j:j + n_sh, :]
    rb = 64
    base = H - (CONV_W // 2)
    for cb in range(GROUP_W // LANE):
        cs = slice(cb * LANE, (cb + 1) * LANE)
        for r in range(tc // rb):
            acc = jnp.zeros((rb, LANE), F32)
            for k in range(CONV_W):
                a, j = divmod(base + k, SUBLANES)
                acc = acc + ysh_scr[j, r * rb + SUBLANES * a:r * rb + SUBLANES * a + rb, cs] * w_ref[k:k + 1, cs]
            cv_scr[r * rb:(r + 1) * rb, cs] = acc
    y = _layernorm(cv_scr[...] + b_ref[...], lng_ref[...], lnb_ref[...])
    o_ref[...] = (y * _sigmoid(y)).astype(BF16)


def _conformer(z, w_dw_all, b_dw_all, ln_g_all, ln_b_all, layer, t_lat, n_rows, tc):
    b = z.shape[0]
    hb = tc // CONV_HALO
    n_tiles = n_rows // tc
    n_halo = z.shape[1] // CONV_HALO
    ca, cg = COL_CA // GROUP_W, COL_CG // GROUP_W
    cur = lambda cblk: pl.BlockSpec((None, tc, GROUP_W), lambda bb, i: (bb, i, cblk))
    prev = lambda cblk: pl.BlockSpec((None, CONV_HALO, GROUP_W),
                                     lambda bb, i: (bb, jnp.maximum(i * hb - 1, 0), cblk))
    nxt = lambda cblk: pl.BlockSpec((None, CONV_HALO, GROUP_W),
                                    lambda bb, i: (bb, jnp.minimum((i + 1) * hb, n_halo - 1), cblk))
    seq_starts = (0, t_lat // tc)
    seq_ends = (t_lat // tc - 1, z.shape[1] // tc - 1)
    return pl.pallas_call(
        functools.partial(_conf_kernel, tc=tc, seq_starts=seq_starts, seq_ends=seq_ends),
        grid=(b, n_tiles),
        in_specs=[cur(ca), cur(cg), prev(ca), prev(cg), nxt(ca), nxt(cg),
                  pl.BlockSpec((None, CONV_W, GROUP_W), lambda *_: (layer, 0, 0)), _layer_row(layer, GROUP_W),
                  _layer_row(layer, GROUP_W), _layer_row(layer, GROUP_W)],
        out_specs=pl.BlockSpec((None, tc, GROUP_W), lambda bb, i: (bb, i, 0)),
        out_shape=jax.ShapeDtypeStruct((b, n_rows, GROUP_W), BF16),
        scratch_shapes=[pltpu.VMEM((tc + 2 * CONV_HALO, GROUP_W), F32),
                        pltpu.VMEM((SUBLANES, tc + 2 * CONV_HALO - SUBLANES, GROUP_W), F32),
                        pltpu.VMEM((tc, GROUP_W), F32)],
        compiler_params=_cparams(("parallel", "parallel")),
        name="conformer_conv",
    )(z, z, z, z, z, z, w_dw_all, b_dw_all, ln_g_all, ln_b_all)


def _attn_prep_kernel(q_ref, k_ref, v_ref, cos_ref, sin_ref, qg_ref, kg_ref, qo_ref, ko_ref, vo_ref):
    cosf = cos_ref[...]
    sinf = sin_ref[...]

    def norm_rope(x, g, scale):
        y = _rms(x.astype(F32), g)
        return (y * cosf + pltpu.roll(y, HEAD_DIM // 2, 1) * sinf) * scale

    q_scale = HEAD_DIM ** -0.5 * math.log2(math.e)
    for h in range(ATT_Q_HEADS):
        hs = slice(h * HEAD_DIM, (h + 1) * HEAD_DIM)
        qo_ref[:, hs] = norm_rope(q_ref[:, hs], qg_ref[...], q_scale).astype(BF16)
    ones = jnp.ones((v_ref.shape[0], HEAD_DIM), BF16)
    for h in range(ATT_KV_HEADS):
        hs = slice(h * HEAD_DIM, (h + 1) * HEAD_DIM)
        ko_ref[:, hs] = norm_rope(k_ref[:, hs], kg_ref[...], 1.0).astype(BF16)
        vo_ref[:, 2 * h * HEAD_DIM:(2 * h + 1) * HEAD_DIM] = v_ref[:, hs]
        vo_ref[:, (2 * h + 1) * HEAD_DIM:(2 * h + 2) * HEAD_DIM] = ones


def _attn_prep(z, cosf, sinf, qg_all, kg_all, layer, tr):
    b, tt, _ = z.shape
    qw, kw = ATT_Q_HEADS * HEAD_DIM, ATT_KV_HEADS * HEAD_DIM
    return pl.pallas_call(
        _attn_prep_kernel,
        grid=(b, tt // tr),
        in_specs=[pl.BlockSpec((None, tr, qw), lambda bb, i: (bb, i, COL_AQ // qw)),
                  pl.BlockSpec((None, tr, kw), lambda bb, i: (bb, i, COL_AK // kw)),
                  pl.BlockSpec((None, tr, kw), lambda bb, i: (bb, i, COL_AV // kw)),
                  pl.BlockSpec((tr, HEAD_DIM), lambda bb, i: (i, 0)),
                  pl.BlockSpec((tr, HEAD_DIM), lambda bb, i: (i, 0)),
                  _layer_row(layer, HEAD_DIM), _layer_row(layer, HEAD_DIM)],
        out_specs=[pl.BlockSpec((None, tr, qw), lambda bb, i: (bb, i, 0)),
                   pl.BlockSpec((None, tr, kw), lambda bb, i: (bb, i, 0)),
                   pl.BlockSpec((None, tr, 2 * kw), lambda bb, i: (bb, i, 0))],
        out_shape=[jax.ShapeDtypeStruct((b, tt, qw), BF16),
                   jax.ShapeDtypeStruct((b, tt, kw), BF16),
                   jax.ShapeDtypeStruct((b, tt, 2 * kw), BF16)],
        compiler_params=_cparams(("parallel", "parallel")),
        name="attn_prep",
    )(z, z, z, cosf, sinf, qg_all, kg_all)


def _attn_kernel(q_ref, k_ref, v_ref, o_ref, *, tq, sub):
    k = k_ref[...]
    vaug = v_ref[...]
    for g in range(ATT_Q_HEADS // ATT_KV_HEADS):
        hs = slice(g * HEAD_DIM, (g + 1) * HEAD_DIM)
        for r in range(tq // sub):
            rs = slice(r * sub, (r + 1) * sub)
            s = lax.dot_general(q_ref[rs, hs], k, (((1,), (1,)), ((), ())), preferred_element_type=F32)
            p = jnp.exp2(s - jnp.max(s, axis=-1, keepdims=True)).astype(BF16)
            res = jnp.dot(p, vaug, preferred_element_type=F32)
            o_ref[rs, hs] = (res[:, :HEAD_DIM] / res[:, HEAD_DIM:]).astype(BF16)


def _attention(qr, kr, vaug, q_row0, n_q, k_row0, n_k, tq):
    b = qr.shape[0]
    gw = (ATT_Q_HEADS // ATT_KV_HEADS) * HEAD_DIM
    assert q_row0 % tq == 0 and n_q % tq == 0 and k_row0 % n_k == 0
    qb, kb = q_row0 // tq, k_row0 // n_k
    return pl.pallas_call(
        functools.partial(_attn_kernel, tq=tq, sub=min(tq, 128)),
        grid=(b, ATT_KV_HEADS, n_q // tq),
        in_specs=[pl.BlockSpec((None, tq, gw), lambda bb, h, i: (bb, i + qb, h)),
                  pl.BlockSpec((None, n_k, HEAD_DIM), lambda bb, h, i: (bb, kb, h)),
                  pl.BlockSpec((None, n_k, 2 * HEAD_DIM), lambda bb, h, i: (bb, kb, h))],
        out_specs=pl.BlockSpec((None, tq, gw), lambda bb, h, i: (bb, i, h)),
        out_shape=jax.ShapeDtypeStruct((b, n_q, ATT_Q_HEADS * HEAD_DIM), BF16),
        compiler_params=_cparams(("parallel", "parallel", "arbitrary")),
        name="attention",
    )(qr, kr, vaug)


def _out_proj_kernel(ya_ref, yb_ref, yc_ref, yd_ref, w_ref, x_ref, gate_ref, pg_ref, o_ref):
    acc = jnp.dot(ya_ref[...], w_ref[0 * GROUP_W:1 * GROUP_W, :], preferred_element_type=F32)
    acc += jnp.dot(yb_ref[...], w_ref[1 * GROUP_W:2 * GROUP_W, :], preferred_element_type=F32)
    acc += jnp.dot(yc_ref[...], w_ref[2 * GROUP_W:3 * GROUP_W, :], preferred_element_type=F32)
    acc += jnp.dot(yd_ref[...], w_ref[3 * GROUP_W:4 * GROUP_W, :], preferred_element_type=F32)
    o_ref[...] = x_ref[...] + gate_ref[...] * _rms(acc, pg_ref[...])


def _out_proj(ys, y_row_offs, w_out_all, xs, mod, mod_row, post_g_all, layer, tm):
    b, t, d = xs.shape
    assert all(off % tm == 0 for off in y_row_offs)
    yspecs = [pl.BlockSpec((None, tm, GROUP_W), functools.partial(lambda bb, i, o: (bb, i + o, 0), o=off // tm))
              for off in y_row_offs]
    return pl.pallas_call(
        _out_proj_kernel,
        grid=(b, t // tm),
        in_specs=[*yspecs,
                  pl.BlockSpec((None,) + w_out_all.shape[1:], lambda *_: (layer, 0, 0)),
                  pl.BlockSpec((None, tm, d), lambda bb, i: (bb, i, 0)),
                  _mod_spec(layer, 2, d, mod_row),
                  _layer_row(layer, d)],
        out_specs=pl.BlockSpec((None, tm, d), lambda bb, i: (bb, i, 0)),
        out_shape=jax.ShapeDtypeStruct((b, t, d), F32),
        compiler_params=_cparams(("parallel", "parallel")),
        name="out_proj",
    )(*ys, w_out_all, xs, mod, post_g_all)


def _ffn_kernel(x_ref, xp_ref, xn_ref, sh_ref, sc_ref, gate_ref, pre_g_ref, post_g_ref, wgu_ref, cwb_ref, wd_ref,
                o_ref, h_scr, acc_scr, *, tm, n_tiles, n_ff, seq_len):
    i = pl.program_id(1)
    j = pl.program_id(2)
    H = FFN_HALO
    n_ext = tm + 2 * H
    tf = wd_ref.shape[0]
    wg_ref = wgu_ref.at[:, 0:tf]
    wu_ref = wgu_ref.at[:, tf:2 * tf]
    n_blk = 4 if tm % 64 == 0 else 1
    rb = tm // n_blk
    ext_bounds = [0] + [H + r * rb for r in range(1, n_blk)] + [n_ext]

    def pre(x):
        return _rms(x, pre_g_ref[...]) * (1.0 + sc_ref[...]) + sh_ref[...]

    def up_first():
        gs, us = [], []
        for r in range(n_blk):
            lo, hi = ext_bounds[r], ext_bounds[r + 1]
            m_lo, m_hi = max(lo, H), min(hi, H + tm)
            pieces = []
            if r == 0:
                pieces.append(jnp.where(i > 0, pre(xp_ref[...]), 0.0))
            pieces.append(pre(x_ref[m_lo - H:m_hi - H, :]))
            if r == n_blk - 1:
                pieces.append(jnp.where(i < n_tiles - 1, pre(xn_ref[...]), 0.0))
            hc = (pieces[0] if len(pieces) == 1 else jnp.concatenate(pieces, axis=0)).astype(BF16)
            h_scr[lo:hi, :] = hc
            gs.append(jnp.dot(hc, wg_ref[...], preferred_element_type=F32))
            us.append(jnp.dot(hc[m_lo - lo:m_hi - lo, :], wu_ref[...], preferred_element_type=F32))
        return jnp.concatenate(gs, axis=0), jnp.concatenate(us, axis=0)

    def step(first, last):
        if first:
            g_ext, u = up_first()
        else:
            g_ext = jnp.dot(h_scr[...], wg_ref[...], preferred_element_type=F32)
            u = jnp.dot(h_scr[H:H + tm, :], wu_ref[...], preferred_element_type=F32)
        g_prev = pltpu.roll(g_ext, 1, 0)[H:H + tm]
        g_next = pltpu.roll(g_ext, n_ext - 1, 0)[H:H + tm]
        if seq_len < tm:
            pos = lax.broadcasted_iota(jnp.int32, (tm, 1), 0) % seq_len
            g_prev = jnp.where(pos == 0, 0.0, g_prev)
            g_next = jnp.where(pos == seq_len - 1, 0.0, g_next)
        conv = (g_prev * cwb_ref[0:1, :] + g_ext[H:H + tm] * cwb_ref[1:2, :] + g_next * cwb_ref[2:3, :]
                + cwb_ref[3:4, :])
        act = ((conv * _sigmoid(conv)) * u).astype(BF16)
        if not last:
            down = jnp.dot(act, wd_ref[...], preferred_element_type=F32)
            acc_scr[...] = down if first else acc_scr[...] + down
            return
        for r in range(n_blk):
            rs = slice(r * rb, (r + 1) * rb)
            down = jnp.dot(act[rs, :], wd_ref[...], preferred_element_type=F32)
            acc = down if first else acc_scr[rs, :] + down
            o_ref[rs, :] = x_ref[rs, :] + gate_ref[...] * _rms(acc, post_g_ref[...])

    @pl.when(j == 0)
    def _():
        step(True, n_ff == 1)

    if n_ff > 2:
        @pl.when(jnp.logical_and(j > 0, j < n_ff - 1))
        def _():
            step(False, False)

    if n_ff > 1:
        @pl.when(j == n_ff - 1)
        def _():
            step(False, True)


def _ffn(xs, mod, mod_row, pre_g_all, post_g_all, w_gu_all, cv_wb_all, w_down_all, layer, tm, seq_len, tf):
    b, t, d = xs.shape
    assert seq_len == t or (tm % seq_len == 0 and t % tm == 0)
    d_ff = w_down_all.shape[1]
    n_ff = d_ff // tf
    n_tiles = t // tm
    hb = tm // FFN_HALO
    n_halo = t // FFN_HALO
    return pl.pallas_call(
        functools.partial(_ffn_kernel, tm=tm, n_tiles=n_tiles, n_ff=n_ff, seq_len=seq_len),
        grid=(b, n_tiles, n_ff),
        in_specs=[pl.BlockSpec((None, tm, d), lambda bb, i, j: (bb, i, 0)),
                  pl.BlockSpec((None, FFN_HALO, d), lambda bb, i, j: (bb, jnp.maximum(i * hb - 1, 0), 0)),
                  pl.BlockSpec((None, FFN_HALO, d), lambda bb, i, j: (bb, jnp.minimum((i + 1) * hb, n_halo - 1), 0)),
                  _mod_spec(layer, 3, d, mod_row), _mod_spec(layer, 4, d, mod_row), _mod_spec(layer, 5, d, mod_row),
                  _layer_row(layer, d), _layer_row(layer, d),
                  pl.BlockSpec((None, d, 2 * tf), lambda bb, i, j: (layer, 0, j)),
                  pl.BlockSpec((None, cv_wb_all.shape[1], tf), lambda bb, i, j: (layer, 0, j)),
                  pl.BlockSpec((None, tf, d), lambda bb, i, j: (layer, j, 0))],
        out_specs=pl.BlockSpec((None, tm, d), lambda bb, i, j: (bb, i, 0)),
        out_shape=jax.ShapeDtypeStruct((b, t, d), F32),
        scratch_shapes=[pltpu.VMEM((tm + 2 * FFN_HALO, d), BF16), pltpu.VMEM((tm, d), F32)],
        compiler_params=_cparams(("parallel", "parallel", "arbitrary")),
        name="conv_glu_ffn",
    )(xs, xs, xs, mod, mod, mod, pre_g_all, post_g_all, w_gu_all, cv_wb_all, w_down_all)


def _pack_w_in(w_in):
    depth, d, _ = w_in.shape
    main = jnp.concatenate([w_in[..., :4 * GROUP_W], w_in[..., 4 * GROUP_W + N_GATE:]], axis=-1).astype(BF16)
    gates = w_in[..., 4 * GROUP_W:4 * GROUP_W + N_GATE]
    gates = jnp.concatenate([gates, jnp.zeros((depth, d, LANE - N_GATE), w_in.dtype)], axis=-1).astype(BF16)
    return main, gates


def _rope_tables(t_lat, t_ctx):
    t = jnp.arange(t_lat)
    row = (t // GRID_W).astype(F32)
    colp = (t % GRID_W).astype(F32)
    axis_dim = HEAD_DIM // 2
    inv = jnp.power(ROPE_THETA, -jnp.arange(0, axis_dim, 2, dtype=F32) / axis_dim)
    ang = jnp.concatenate([row[:, None] * inv, colp[:, None] * inv], axis=-1)
    cos, sin = jnp.cos(ang), jnp.sin(ang)
    cosf = jnp.concatenate([cos, cos], axis=-1)
    sinf = jnp.concatenate([-sin, sin], axis=-1)
    cosf = jnp.concatenate([cosf, jnp.ones((t_ctx, HEAD_DIM), F32)], axis=0)
    sinf = jnp.concatenate([sinf, jnp.zeros((t_ctx, HEAD_DIM), F32)], axis=0)
    return cosf, sinf


def kernel(x, c, ctx, c_ctx, w_ada, b_ada, pre_mix_g, post_mix_g, w_in, ml_gate_b, ml_norm_g, sg_ln_g, sg_ln_b,
           sg_w, sg_b, cv_w, cv_b, cv_ln_g, cv_ln_b, at_qn_g, at_kn_g, w_out, pre_ffn_g, post_ffn_g, w_up,
           ffn_cv_w, ffn_cv_b, w_down):
    b, t_lat, d = x.shape
    t_ctx = ctx.shape[1]
    tt = t_lat + t_ctx
    depth = w_ada.shape[0]
    tm_in = min(1024, t_lat)
    tm_lat = min(512, t_lat)
    assert t_lat % tm_in == 0 and t_lat % t_ctx == 0 and t_ctx % CHUNK == 0 and t_ctx % 256 == 0

    n_rows = -(-(b + 1) // 8) * 8
    s_rows = jnp.concatenate([c, c_ctx[None, :], jnp.zeros((n_rows - b - 1, d), F32)], axis=0)
    mod = _ada(s_rows, w_ada, b_ada).reshape(depth, n_rows, 1, 6 * d)
    cosf, sinf = _rope_tables(t_lat, t_ctx)

    rows = lambda a: a.reshape(depth, 1, -1)
    w_main, w_gate = _pack_w_in(w_in)
    w_out_b, w_down_b = w_out.astype(BF16), w_down.astype(BF16)
    d_ff = w_down.shape[1]
    w_gu = (w_up.reshape(depth, d, 2, d_ff // FFN_TF, FFN_TF).transpose(0, 1, 3, 2, 4)
            .reshape(depth, d, 2 * d_ff).astype(BF16))
    cv_wb = jnp.concatenate([ffn_cv_w, ffn_cv_b[:, None, :]], axis=1)
    gate_b = rows(jnp.concatenate([ml_gate_b, jnp.zeros((depth, LANE - N_GATE), F32)], axis=1))
    sg_w_b = sg_w.astype(BF16)
    sg_b_full = jnp.broadcast_to(sg_b[..., None], (depth, SG_GROUPS, CHUNK, LANE))
    pre_mix, post_mix, pre_ffn, post_ffn = rows(pre_mix_g), rows(post_mix_g), rows(pre_ffn_g), rows(post_ffn_g)
    ml_norm, sg_g, sg_bb, cv_bias, cv_g, cv_bb = (rows(ml_norm_g), rows(sg_ln_g), rows(sg_ln_b), rows(cv_b),
                                                  rows(cv_ln_g), rows(cv_ln_b))
    qn_g, kn_g = rows(at_qn_g), rows(at_kn_g)

    xc = ctx
    for i in range(depth):
        need_ctx = i < depth - 1
        z, zg = _in_proj(x, xc, mod, pre_mix, w_main, w_gate, i, tm_in)
        ya = _mlstm(z, zg, gate_b, ml_norm, i, t_lat, t_ctx)

        mix_rows = tt if need_ctx else t_lat
        sg_tr = 2176 if (need_ctx and tt % 2176 == 0) else (2048 if t_lat % 2048 == 0 else 256)
        sg_tr = sg_tr if mix_rows % sg_tr == 0 else 256
        yb = _spatial_gating(z, sg_g, sg_bb, sg_w_b, sg_b_full, i, mix_rows, sg_tr)
        yc = _conformer(z, cv_w, cv_bias, cv_g, cv_bb, i, t_lat, mix_rows, 256)

        prep_tr = 1088 if tt % 1088 == 0 else 256
        qr, kr, vaug = _attn_prep(z, cosf, sinf, qn_g, kn_g, i, prep_tr)
        yd = _attention(qr, kr, vaug, 0, t_lat, 0, tt, min(512, t_lat))

        x = _out_proj((ya, yb, yc, yd), (0, 0, 0, 0), w_out_b, x, mod, None, post_mix, i, tm_lat)
        x = _ffn(x, mod, None, pre_ffn, post_ffn, w_gu, cv_wb, w_down_b, i, tm_lat, t_lat, FFN_TF)
        if need_ctx:
            yd_c = _attention(qr, kr, vaug, t_lat, t_ctx, t_lat, t_ctx, t_ctx)
            xc = _out_proj((ya, yb, yc, yd_c), (t_lat, t_lat, t_lat, 0), w_out_b, xc, mod, b, post_mix, i, t_ctx)
            tm_ctx = t_ctx * max(1, min(b, 512 // t_ctx))
            tm_ctx = tm_ctx if (b * t_ctx) % tm_ctx == 0 else t_ctx
            xc = _ffn(xc.reshape(1, b * t_ctx, d), mod, b, pre_ffn, post_ffn, w_gu, cv_wb, w_down_b,
                      i, tm_ctx, t_ctx, FFN_TF).reshape(b, t_ctx, d)
    return x
```

```python
import functools
import math

import jax
import jax.numpy as jnp
from jax import lax
from jax.experimental import pallas as pl
from jax.experimental.pallas import tpu as pltpu

F32 = jnp.float32
BF16 = jnp.bfloat16

EPS = 1e-6
LANE = 128
SUBLANES = 8
HEAD_DIM = 128
GROUP_W = 512
ML_HEADS = 4
CHUNK = 128
SG_GROUPS = 4
CONV_W = 31
CONV_HALO = 16
ATT_Q_HEADS = 4
ATT_KV_HEADS = 2
GRID_W = 64
ROPE_THETA = 10000.0
FFN_HALO = 16
N_GATE = 16
GATE_ROWS = 32

COL_MQ, COL_MK, COL_MV, COL_MO = 0, 512, 1024, 1536
COL_SU, COL_SV = 2048, 2560
COL_CA, COL_CG = 3072, 3584
COL_AQ, COL_AK, COL_AV = 4096, 4608, 4864
NZ = 5120

VMEM_LIMIT = 52 * 1024 * 1024


def _cparams(sem):
    return pltpu.CompilerParams(dimension_semantics=sem, vmem_limit_bytes=VMEM_LIMIT)


def _layer_row(layer, n):
    return pl.BlockSpec((None, 1, n), lambda *_: (layer, 0, 0))


def _mod_spec(layer, k, d, row=None):
    if row is None:
        return pl.BlockSpec((None, None, 1, d), lambda bb, *_: (layer, bb, 0, k))
    return pl.BlockSpec((None, None, 1, d), lambda *_: (layer, row, 0, k))


def _rms(x, g):
    ms = jnp.mean(x * x, axis=-1, keepdims=True)
    return x * lax.rsqrt(ms + EPS) * g


def _layernorm(x, g, b):
    mu = jnp.mean(x, axis=-1, keepdims=True)
    d = x - mu
    var = jnp.mean(d * d, axis=-1, keepdims=True)
    return d * lax.rsqrt(var + EPS) * g + b


def _sigmoid(x):
    return 1.0 / (1.0 + jnp.exp(-x))


def _gelu_tanh(x):
    c = math.sqrt(2.0 / math.pi)
    return x * (0.5 * (1.0 + jnp.tanh(c * (x + 0.044715 * (x * x * x)))))


def _log_sigmoid(x):
    return jnp.minimum(x, 0.0) - jnp.log1p(jnp.exp(-jnp.abs(x)))


def _ada_kernel(s_ref, w_ref, b_ref, o_ref):
    s = s_ref[...]
    s = s * _sigmoid(s)
    o_ref[...] = jnp.dot(s.astype(BF16), w_ref[...].astype(BF16), preferred_element_type=F32) + b_ref[...]


def _ada(s_rows, w_ada, b_ada, tn=1024):
    depth, d, n = w_ada.shape
    r = s_rows.shape[0]
    return pl.pallas_call(
        _ada_kernel,
        grid=(depth, n // tn),
        in_specs=[pl.BlockSpec((r, d), lambda l, j: (0, 0)),
                  pl.BlockSpec((None, d, tn), lambda l, j: (l, 0, j)),
                  pl.BlockSpec((None, 1, tn), lambda l, j: (l, 0, j))],
        out_specs=pl.BlockSpec((None, r, tn), lambda l, j: (l, 0, j)),
        out_shape=jax.ShapeDtypeStruct((depth, r, n), F32),
        compiler_params=_cparams(("parallel", "parallel")),
        name="ada_mod",
    )(s_rows, w_ada, b_ada.reshape(depth, 1, n))


def _in_proj_kernel(x_ref, xc_ref, sh_ref, sc_ref, shc_ref, scc_ref, g_ref, w_ref, wg_ref, o_ref, og_ref, h_scr,
                    *, tm, t_ctx, n_lat_tiles, rows_per_step):
    i = pl.program_id(1)
    first_col = pl.program_id(2) == 0
    is_lat = i < n_lat_tiles

    def pre(x, sh, sc):
        return (_rms(x, g_ref[...]) * (1.0 + sc) + sh).astype(BF16)

    def first_step(x_rows, sh, sc, n_rows):
        for r in range(n_rows // rows_per_step):
            sl = slice(r * rows_per_step, (r + 1) * rows_per_step)
            h = pre(x_rows[sl, :], sh[...], sc[...])
            h_scr[sl, :] = h
            o_ref[sl, :] = jnp.dot(h, w_ref[...], preferred_element_type=F32).astype(BF16)
            og_ref[sl, :] = jnp.dot(h, wg_ref[...], preferred_element_type=F32)

    @pl.when(jnp.logical_and(first_col, is_lat))
    def _():
        first_step(x_ref, sh_ref, sc_ref, tm)

    @pl.when(jnp.logical_and(jnp.logical_not(first_col), is_lat))
    def _():
        o_ref[...] = jnp.dot(h_scr[...], w_ref[...], preferred_element_type=F32).astype(BF16)

    @pl.when(jnp.logical_and(first_col, jnp.logical_not(is_lat)))
    def _():
        first_step(xc_ref, shc_ref, scc_ref, t_ctx)
        if t_ctx < tm:
            og_ref[t_ctx:, :] = jnp.zeros((tm - t_ctx, og_ref.shape[1]), F32)

    @pl.when(jnp.logical_and(jnp.logical_not(first_col), jnp.logical_not(is_lat)))
    def _():
        o_ref[0:t_ctx, :] = jnp.dot(h_scr[0:t_ctx, :], w_ref[...], preferred_element_type=F32).astype(BF16)

    if t_ctx < tm:
        @pl.when(jnp.logical_not(is_lat))
        def _():
            o_ref[t_ctx:, :] = jnp.zeros((tm - t_ctx, o_ref.shape[1]), BF16)


def _in_proj(x, xc, mod, g_all, w_all, wg_all, layer, tm, tn=1280):
    b, t_lat, d = x.shape
    t_ctx = xc.shape[1]
    tt = t_lat + t_ctx
    nz = w_all.shape[2]
    n_lat_tiles = t_lat // tm
    assert t_lat % tm == 0 and t_ctx <= tm
    return pl.pallas_call(
        functools.partial(_in_proj_kernel, tm=tm, t_ctx=t_ctx, n_lat_tiles=n_lat_tiles, rows_per_step=min(tm, 256)),
        grid=(b, n_lat_tiles + 1, nz // tn),
        in_specs=[pl.BlockSpec((None, tm, d), lambda bb, i, j: (bb, jnp.minimum(i, n_lat_tiles - 1), 0)),
                  pl.BlockSpec((None, t_ctx, d), lambda bb, i, j: (bb, 0, 0)),
                  _mod_spec(layer, 0, d), _mod_spec(layer, 1, d),
                  _mod_spec(layer, 0, d, row=b), _mod_spec(layer, 1, d, row=b),
                  _layer_row(layer, d),
                  pl.BlockSpec((None, d, tn), lambda bb, i, j: (layer, 0, j)),
                  pl.BlockSpec((None, d, LANE), lambda *_: (layer, 0, 0))],
        out_specs=[pl.BlockSpec((None, tm, tn), lambda bb, i, j: (bb, i, j)),
                   pl.BlockSpec((None, tm, LANE), lambda bb, i, j: (bb, i, 0))],
        out_shape=[jax.ShapeDtypeStruct((b, tt, nz), BF16), jax.ShapeDtypeStruct((b, tt, LANE), F32)],
        scratch_shapes=[pltpu.VMEM((tm, d), BF16)],
        compiler_params=_cparams(("parallel", "parallel", "arbitrary")),
        name="in_proj",
    )(x, xc, mod, mod, mod, mod, g_all, w_all, wg_all)


def _prefix_lanes(x, op, fill):
    lane = lax.broadcasted_iota(jnp.int32, x.shape, 1)
    sh = 1
    while sh < x.shape[1]:
        x = op(x, jnp.where(lane >= sh, pltpu.roll(x, sh, 1), fill))
        sh *= 2
    return x


def _suffix_lanes(x, op, fill):
    n = x.shape[1]
    lane = lax.broadcasted_iota(jnp.int32, x.shape, 1)
    sh = 1
    while sh < n:
        x = op(x, jnp.where(lane < n - sh, pltpu.roll(x, n - sh, 1), fill))
        sh *= 2
    return x


def _mlstm_kernel(q_ref, k_ref, v_ref, o_ref, gt_ref, gb_ref, ng_ref, out_ref,
                  rb_scr, gr_scr, vt_scr, u_scr, cp_scr, st_scr, sa_scr, sc_scr, sm_scr, *, n_lat, n_ctx):
    L = CHUNK
    nc = n_lat + n_ctx
    VT_ROWS = vt_scr.shape[1]
    head = pl.program_id(1)
    row = lax.broadcasted_iota(jnp.int32, (L, L), 0)
    col = lax.broadcasted_iota(jnp.int32, (L, L), 1)
    lower = col <= row
    upper = col >= row
    k_scale = HEAD_DIM ** -0.5
    HALF = GATE_ROWS // 2

    def rows_of(c):
        return pl.ds(pl.multiple_of(c * L, L), L)

    def step_of(c, direction):
        if direction == 0:
            return jnp.where(c < n_lat, c + n_ctx, c - n_lat)
        return nc - 1 - c

    @pl.when(head == 0)
    def _():
        lower_f = lower.astype(F32)
        upper_f = upper.astype(F32)
        n_all = nc * HALF
        rg = lax.broadcasted_iota(jnp.int32, (n_all, L), 0) % HALF
        is_cum = (rg % 8) >= 4
        is_bwd = rg >= 8

        def to_rows(c, carry):
            gr_scr[c, 0:HALF, :] = (gt_ref[rows_of(c), :] + gb_ref[...]).T[0:HALF, :]
            return carry

        lax.fori_loop(0, nc, to_rows, 0, unroll=2)

        G = gr_scr[:, 0:HALF, :].reshape(n_all, L)
        Gf = _log_sigmoid(G)
        Bf = jnp.dot(Gf, upper_f, precision=lax.Precision.HIGHEST, preferred_element_type=F32)
        Bb = jnp.dot(Gf, lower_f, precision=lax.Precision.HIGHEST, preferred_element_type=F32)
        Bsel = jnp.where(is_bwd, Bb, Bf)
        R = G - pltpu.roll(Bsel, n_all - 4, 0)
        CM = jnp.where(is_bwd, _suffix_lanes(R, jnp.maximum, -jnp.inf), _prefix_lanes(R, jnp.maximum, -jnp.inf))
        gr_scr[:, 0:HALF, :] = jnp.where(is_cum, Bsel, G).reshape(nc, HALF, L)
        gr_scr[:, HALF:GATE_ROWS, :] = jnp.where(is_cum, pltpu.roll(CM, 4, 0), R).reshape(nc, HALF, L)

    row_g = lax.broadcasted_iota(jnp.int32, (GATE_ROWS, L), 0)
    lane_r = lax.broadcasted_iota(jnp.int32, (1, LANE), 1)

    def gate_row(x, idx):
        return jnp.sum(jnp.where(row_g == idx, x, 0.0), axis=0, keepdims=True)

    st_scr[...] = jnp.zeros_like(st_scr)
    ones_rows = jnp.ones((VT_ROWS - HEAD_DIM, L), F32)

    def summarize(c, carry):
        vt = jnp.concatenate([v_ref[rows_of(c), :].astype(F32).T, ones_rows], axis=0)
        vt_scr[c] = vt.astype(BF16)
        k = k_ref[rows_of(c), :]
        grv = gr_scr[c]
        for direction in (0, 1):
            ci = direction * 8 + head
            b_row = gate_row(grv, ci + 4)
            i_row = gate_row(grv, ci)
            total = b_row[:, L - 1:L] if direction == 0 else b_row[:, 0:1]
            g_row = total - b_row + i_row
            mg = jnp.max(g_row, axis=1, keepdims=True)
            w_row = jnp.exp(g_row - mg) * k_scale
            step = step_of(c, direction)
            u_scr[direction * nc + step] = jnp.dot((vt * w_row).astype(BF16), k, preferred_element_type=F32)
            st_scr[direction, pl.ds(step, 1), :] = jnp.where(lane_r < 64, total, mg)
            rb_scr[direction * nc + c] = jnp.broadcast_to(gate_row(grv, 16 + ci), (L, L)).T
        return carry

    lax.fori_loop(0, nc, summarize, 0, unroll=4)

    lane8 = lax.broadcasted_iota(jnp.int32, (8, LANE), 1)
    for direction in (0, 1):
        xt = st_scr[direction].T
        valid = lane8 < nc
        T = jnp.where(valid, jnp.broadcast_to(xt[0:1, :], (8, LANE)), 0.0)
        G = jnp.where(valid, jnp.broadcast_to(xt[64:65, :], (8, LANE)), -jnp.inf)
        PT = _prefix_lanes(T, jnp.add, 0.0)
        mvec = PT + jnp.maximum(_prefix_lanes(G - PT, jnp.maximum, -jnp.inf), 0.0)
        mprev = jnp.where(lane8 >= 1, pltpu.roll(mvec, 1, 1), 0.0)
        a = jnp.exp(T + mprev - mvec)
        c2 = jnp.exp(G - mvec)
        sa_scr[direction] = jnp.broadcast_to(a[0:1, :], (LANE, LANE)).T
        sc_scr[direction] = jnp.broadcast_to(c2[0:1, :], (LANE, LANE)).T
        sm_scr[direction] = jnp.broadcast_to(mprev[0:1, :], (LANE, LANE)).T

    def scan(s, carry):
        new = []
        for direction in (0, 1):
            ct = carry[direction]
            idx = direction * nc + s
            cp_scr[idx] = ct.astype(BF16)
            a = sa_scr[direction, pl.ds(s, 1), :]
            c2 = sc_scr[direction, pl.ds(s, 1), :]
            new.append(a * ct + c2 * u_scr[idx])
        return tuple(new)

    zero = jnp.zeros((VT_ROWS, HEAD_DIM), F32)
    lax.fori_loop(0, nc, scan, (zero, zero))

    nt_dims = (((1,), (1,)), ((), ()))

    def emit(c, carry):
        q = q_ref[rows_of(c), :]
        vt = vt_scr[c]
        grv = gr_scr[c]
        qk_t = lax.dot_general(k_ref[rows_of(c), :], q, nt_dims, preferred_element_type=F32) * k_scale
        h_t = jnp.zeros((HEAD_DIM, L), F32)
        for direction in (0, 1):
            ci = direction * 8 + head
            step = step_of(c, direction)
            r_col = rb_scr[direction * nc + c]
            cm_row = gate_row(grv, 16 + ci + 4)
            b_row = gate_row(grv, ci + 4)
            p_t = jnp.exp(jnp.where(upper if direction == 0 else lower, r_col - cm_row, -jnp.inf))
            intra = jnp.dot(vt, (qk_t * p_t).astype(BF16), preferred_element_type=F32)
            inter = lax.dot_general(cp_scr[direction * nc + step], q, nt_dims, preferred_element_type=F32)
            mp = sm_scr[direction, pl.ds(step, 1), :]
            mx = jnp.maximum(mp, cm_row)
            e1 = jnp.exp(mp - mx)
            e2 = jnp.exp(cm_row - mx)
            num = e1 * inter[:HEAD_DIM] + e2 * intra[:HEAD_DIM]
            den = e1 * inter[HEAD_DIM:HEAD_DIM + 1] + e2 * intra[HEAD_DIM:HEAD_DIM + 1]
            h_t = h_t + num * (1.0 / jnp.maximum(jnp.abs(den), jnp.exp(-(b_row + mx))))
        mu = jnp.mean(h_t, axis=0, keepdims=True)
        dev = h_t - mu
        var = jnp.mean(dev * dev, axis=0, keepdims=True)
        hn = (dev * lax.rsqrt(var + EPS)).T * ng_ref[...]
        out_ref[rows_of(c), :] = (hn * _sigmoid(o_ref[rows_of(c), :].astype(F32))).astype(BF16)
        return carry

    lax.fori_loop(0, nc, emit, 0, unroll=8)


def _mlstm(z, zg, gate_b_all, norm_g_all, layer, t_lat, t_ctx):
    b, tt, _ = z.shape
    nc = tt // CHUNK
    assert nc <= 64
    vt_rows = HEAD_DIM + 16
    blk = lambda off: pl.BlockSpec((None, tt, LANE), lambda bb, h: (bb, 0, off // LANE + h))
    return pl.pallas_call(
        functools.partial(_mlstm_kernel, n_lat=t_lat // CHUNK, n_ctx=t_ctx // CHUNK),
        grid=(b, ML_HEADS),
        in_specs=[blk(COL_MQ), blk(COL_MK), blk(COL_MV), blk(COL_MO),
                  pl.BlockSpec((None, tt, LANE), lambda bb, h: (bb, 0, 0)),
                  _layer_row(layer, LANE),
                  pl.BlockSpec((None, 1, LANE), lambda bb, h: (layer, 0, h))],
        out_specs=pl.BlockSpec((None, tt, LANE), lambda bb, h: (bb, 0, h)),
        out_shape=jax.ShapeDtypeStruct((b, tt, GROUP_W), BF16),
        scratch_shapes=[pltpu.VMEM((2 * nc, CHUNK, CHUNK), F32),
                        pltpu.VMEM((nc, GATE_ROWS, LANE), F32),
                        pltpu.VMEM((nc, vt_rows, CHUNK), BF16),
                        pltpu.VMEM((2 * nc, vt_rows, HEAD_DIM), F32),
                        pltpu.VMEM((2 * nc, vt_rows, HEAD_DIM), BF16),
                        pltpu.VMEM((2, LANE, LANE), F32),
                        pltpu.VMEM((2, LANE, LANE), F32),
                        pltpu.VMEM((2, LANE, LANE), F32),
                        pltpu.VMEM((2, LANE, LANE), F32)],
        compiler_params=_cparams(("parallel", "arbitrary")),
        name="mlstm",
    )(z, z, z, z, zg, gate_b_all, norm_g_all)


def _sg_kernel(u_ref, v_ref, lng_ref, lnb_ref, w_ref, bs_ref, o_ref, *, n_chunks):
    def body(c, carry):
        sl = pl.ds(pl.multiple_of(c * CHUNK, CHUNK), CHUNK)
        u = _gelu_tanh(u_ref[sl, :].astype(F32))
        v = _layernorm(_gelu_tanh(v_ref[sl, :].astype(F32)), lng_ref[...], lnb_ref[...]).astype(BF16)
        for g in range(SG_GROUPS):
            gs = slice(g * LANE, (g + 1) * LANE)
            mixed = jnp.dot(w_ref[g], v[:, gs], preferred_element_type=F32) + bs_ref[g]
            o_ref[sl, gs] = (u[:, gs] * mixed).astype(BF16)
        return carry

    lax.fori_loop(0, n_chunks, body, 0, unroll=2)


def _spatial_gating(z, ln_g_all, ln_b_all, w_s_all, b_s_all, layer, n_rows, tr):
    b = z.shape[0]
    blk = lambda off: pl.BlockSpec((None, tr, GROUP_W), lambda bb, i: (bb, i, off // GROUP_W))
    per_layer = lambda *_: (layer, 0, 0, 0)
    return pl.pallas_call(
        functools.partial(_sg_kernel, n_chunks=tr // CHUNK),
        grid=(b, n_rows // tr),
        in_specs=[blk(COL_SU), blk(COL_SV),
                  _layer_row(layer, GROUP_W), _layer_row(layer, GROUP_W),
                  pl.BlockSpec((None, SG_GROUPS, CHUNK, CHUNK), per_layer),
                  pl.BlockSpec((None, SG_GROUPS, CHUNK, LANE), per_layer)],
        out_specs=pl.BlockSpec((None, tr, GROUP_W), lambda bb, i: (bb, i, 0)),
        out_shape=jax.ShapeDtypeStruct((b, n_rows, GROUP_W), BF16),
        compiler_params=_cparams(("parallel", "parallel")),
        name="spatial_gating",
    )(z, z, ln_g_all, ln_b_all, w_s_all, b_s_all)


def _conf_kernel(a_ref, g_ref, ap_ref, gp_ref, an_ref, gn_ref, w_ref, b_ref, lng_ref, lnb_ref, o_ref,
                 y_scr, ysh_scr, cv_scr, *, tc, seq_starts, seq_ends):
    i = pl.program_id(1)
    has_prev = functools.reduce(jnp.logical_and, [i != s for s in seq_starts])
    has_next = functools.reduce(jnp.logical_and, [i != e for e in seq_ends])
    H = CONV_HALO

    def glu(a, g):
        return a[...].astype(F32) * _sigmoid(g[...].astype(F32))

    y_scr[H:H + tc, :] = glu(a_ref, g_ref)
    y_scr[0:H, :] = jnp.where(has_prev, glu(ap_ref, gp_ref), 0.0)
    y_scr[H + tc:, :] = jnp.where(has_next, glu(an_ref, gn_ref), 0.0)
    n_sh = ysh_scr.shape[1]
    for j in range(SUBLANES):
        ysh_scr[j] = y_scr[j:j + n_sh, :]
    rb = 64
    base = H - (CONV_W // 2)
    for cb in range(GROUP_W // LANE):
        cs = slice(cb * LANE, (cb + 1) * LANE)
        for r in range(tc // rb):
            acc = jnp.zeros((rb, LANE), F32)
            for k in range(CONV_W):
                a, j = divmod(base + k, SUBLANES)
                acc = acc + ysh_scr[j, r * rb + SUBLANES * a:r * rb + SUBLANES * a + rb, cs] * w_ref[k:k + 1, cs]
            cv_scr[r * rb:(r + 1) * rb, cs] = acc
    y = _layernorm(cv_scr[...] + b_ref[...], lng_ref[...], lnb_ref[...])
    o_ref[...] = (y * _sigmoid(y)).astype(BF16)


def _conformer(z, w_dw_all, b_dw_all, ln_g_all, ln_b_all, layer, t_lat, n_rows, tc):
    b = z.shape[0]
    hb = tc // CONV_HALO
    n_tiles = n_rows // tc
    n_halo = z.shape[1] // CONV_HALO
    ca, cg = COL_CA // GROUP_W, COL_CG // GROUP_W
    cur = lambda cblk: pl.BlockSpec((None, tc, GROUP_W), lambda bb, i: (bb, i, cblk))
    prev = lambda cblk: pl.BlockSpec((None, CONV_HALO, GROUP_W),
                                     lambda bb, i: (bb, jnp.maximum(i * hb - 1, 0), cblk))
    nxt = lambda cblk: pl.BlockSpec((None, CONV_HALO, GROUP_W),
                                    lambda bb, i: (bb, jnp.minimum((i + 1) * hb, n_halo - 1), cblk))
    seq_starts = (0, t_lat // tc)
    seq_ends = (t_lat // tc - 1, z.shape[1] // tc - 1)
    return pl.pallas_call(
        functools.partial(_conf_kernel, tc=tc, seq_starts=seq_starts, seq_ends=seq_ends),
        grid=(b, n_tiles),
        in_specs=[cur(ca), cur(cg), prev(ca), prev(cg), nxt(ca), nxt(cg),
                  pl.BlockSpec((None, CONV_W, GROUP_W), lambda *_: (layer, 0, 0)), _layer_row(layer, GROUP_W),
                  _layer_row(layer, GROUP_W), _layer_row(layer, GROUP_W)],
        out_specs=pl.BlockSpec((None, tc, GROUP_W), lambda bb, i: (bb, i, 0)),
        out_shape=jax.ShapeDtypeStruct((b, n_rows, GROUP_W), BF16),
        scratch_shapes=[pltpu.VMEM((tc + 2 * CONV_HALO, GROUP_W), F32),
                        pltpu.VMEM((SUBLANES, tc + 2 * CONV_HALO - SUBLANES, GROUP_W), F32),
                        pltpu.VMEM((tc, GROUP_W), F32)],
        compiler_params=_cparams(("parallel", "parallel")),
        name="conformer_conv",
    )(z, z, z, z, z, z, w_dw_all, b_dw_all, ln_g_all, ln_b_all)


def _attn_prep_kernel(q_ref, k_ref, v_ref, cos_ref, sin_ref, qg_ref, kg_ref, qo_ref, ko_ref, vo_ref):
    cosf = cos_ref[...]
    sinf = sin_ref[...]

    def norm_rope(x, g, scale):
        y = _rms(x.astype(F32), g)
        return (y * cosf + pltpu.roll(y, HEAD_DIM // 2, 1) * sinf) * scale

    q_scale = HEAD_DIM ** -0.5 * math.log2(math.e)
    for h in range(ATT_Q_HEADS):
        hs = slice(h * HEAD_DIM, (h + 1) * HEAD_DIM)
        qo_ref[:, hs] = norm_rope(q_ref[:, hs], qg_ref[...], q_scale).astype(BF16)
    ones = jnp.ones((v_ref.shape[0], HEAD_DIM), BF16)
    for h in range(ATT_KV_HEADS):
        hs = slice(h * HEAD_DIM, (h + 1) * HEAD_DIM)
        ko_ref[:, hs] = norm_rope(k_ref[:, hs], kg_ref[...], 1.0).astype(BF16)
        vo_ref[:, 2 * h * HEAD_DIM:(2 * h + 1) * HEAD_DIM] = v_ref[:, hs]
        vo_ref[:, (2 * h + 1) * HEAD_DIM:(2 * h + 2) * HEAD_DIM] = ones


def _attn_prep(z, cosf, sinf, qg_all, kg_all, layer, tr):
    b, tt, _ = z.shape
    qw, kw = ATT_Q_HEADS * HEAD_DIM, ATT_KV_HEADS * HEAD_DIM
    return pl.pallas_call(
        _attn_prep_kernel,
        grid=(b, tt // tr),
        in_specs=[pl.BlockSpec((None, tr, qw), lambda bb, i: (bb, i, COL_AQ // qw)),
                  pl.BlockSpec((None, tr, kw), lambda bb, i: (bb, i, COL_AK // kw)),
                  pl.BlockSpec((None, tr, kw), lambda bb, i: (bb, i, COL_AV // kw)),
                  pl.BlockSpec((tr, HEAD_DIM), lambda bb, i: (i, 0)),
                  pl.BlockSpec((tr, HEAD_DIM), lambda bb, i: (i, 0)),
                  _layer_row(layer, HEAD_DIM), _layer_row(layer, HEAD_DIM)],
        out_specs=[pl.BlockSpec((None, tr, qw), lambda bb, i: (bb, i, 0)),
                   pl.BlockSpec((None, tr, kw), lambda bb, i: (bb, i, 0)),
                   pl.BlockSpec((None, tr, 2 * kw), lambda bb, i: (bb, i, 0))],
        out_shape=[jax.ShapeDtypeStruct((b, tt, qw), BF16),
                   jax.ShapeDtypeStruct((b, tt, kw), BF16),
                   jax.ShapeDtypeStruct((b, tt, 2 * kw), BF16)],
        compiler_params=_cparams(("parallel", "parallel")),
        name="attn_prep",
    )(z, z, z, cosf, sinf, qg_all, kg_all)


def _attn_kernel(q_ref, k_ref, v_ref, o_ref, *, tq, sub):
    k = k_ref[...]
    vaug = v_ref[...]
    for g in range(ATT_Q_HEADS // ATT_KV_HEADS):
        hs = slice(g * HEAD_DIM, (g + 1) * HEAD_DIM)
        for r in range(tq // sub):
            rs = slice(r * sub, (r + 1) * sub)
            s = lax.dot_general(q_ref[rs, hs], k, (((1,), (1,)), ((), ())), preferred_element_type=F32)
            p = jnp.exp2(s - jnp.max(s, axis=-1, keepdims=True)).astype(BF16)
            res = jnp.dot(p, vaug, preferred_element_type=F32)
            o_ref[rs, hs] = (res[:, :HEAD_DIM] / res[:, HEAD_DIM:]).astype(BF16)


def _attention(qr, kr, vaug, q_row0, n_q, k_row0, n_k, tq):
    b = qr.shape[0]
    gw = (ATT_Q_HEADS // ATT_KV_HEADS) * HEAD_DIM
    assert q_row0 % tq == 0 and n_q % tq == 0 and k_row0 % n_k == 0
    qb, kb = q_row0 // tq, k_row0 // n_k
    return pl.pallas_call(
        functools.partial(_attn_kernel, tq=tq, sub=min(tq, 128)),
        grid=(b, ATT_KV_HEADS, n_q // tq),
        in_specs=[pl.BlockSpec((None, tq, gw), lambda bb, h, i: (bb, i + qb, h)),
                  pl.BlockSpec((None, n_k, HEAD_DIM), lambda bb, h, i: (bb, kb, h)),
                  pl.BlockSpec((None, n_k, 2 * HEAD_DIM), lambda bb, h, i: (bb, kb, h))],
        out_specs=pl.BlockSpec((None, tq, gw), lambda bb, h, i: (bb, i, h)),
        out_shape=jax.ShapeDtypeStruct((b, n_q, ATT_Q_HEADS * HEAD_DIM), BF16),
        compiler_params=_cparams(("parallel", "parallel", "arbitrary")),
        name="attention",
    )(qr, kr, vaug)


def _out_proj_kernel(ya_ref, yb_ref, yc_ref, yd_ref, w_ref, x_ref, gate_ref, pg_ref, o_ref):
    acc = jnp.dot(ya_ref[...], w_ref[0 * GROUP_W:1 * GROUP_W, :], preferred_element_type=F32)
    acc += jnp.dot(yb_ref[...], w_ref[1 * GROUP_W:2 * GROUP_W, :], preferred_element_type=F32)
    acc += jnp.dot(yc_ref[...], w_ref[2 * GROUP_W:3 * GROUP_W, :], preferred_element_type=F32)
    acc += jnp.dot(yd_ref[...], w_ref[3 * GROUP_W:4 * GROUP_W, :], preferred_element_type=F32)
    o_ref[...] = x_ref[...] + gate_ref[...] * _rms(acc, pg_ref[...])


def _out_proj(ys, y_row_offs, w_out_all, xs, mod, mod_row, post_g_all, layer, tm):
    b, t, d = xs.shape
    assert all(off % tm == 0 for off in y_row_offs)
    yspecs = [pl.BlockSpec((None, tm, GROUP_W), functools.partial(lambda bb, i, o: (bb, i + o, 0), o=off // tm))
              for off in y_row_offs]
    return pl.pallas_call(
        _out_proj_kernel,
        grid=(b, t // tm),
        in_specs=[*yspecs,
                  pl.BlockSpec((None,) + w_out_all.shape[1:], lambda *_: (layer, 0, 0)),
                  pl.BlockSpec((None, tm, d), lambda bb, i: (bb, i, 0)),
                  _mod_spec(layer, 2, d, mod_row),
                  _layer_row(layer, d)],
        out_specs=pl.BlockSpec((None, tm, d), lambda bb, i: (bb, i, 0)),
        out_shape=jax.ShapeDtypeStruct((b, t, d), F32),
        compiler_params=_cparams(("parallel", "parallel")),
        name="out_proj",
    )(*ys, w_out_all, xs, mod, post_g_all)


def _ffn_kernel(x_ref, xp_ref, xn_ref, sh_ref, sc_ref, gate_ref, pre_g_ref, post_g_ref, wg_ref, wu_ref,
                cw_ref, cb_ref, wd_ref, o_ref, h_scr, acc_scr, *, tm, n_tiles, n_ff, seq_len):
    i = pl.program_id(1)
    j = pl.program_id(2)
    H = FFN_HALO
    n_ext = tm + 2 * H
    n_blk = 4 if tm % 64 == 0 else 1
    rb = tm // n_blk
    ext_bounds = [0] + [H + r * rb for r in range(1, n_blk)] + [n_ext]

    def pre(x):
        return _rms(x, pre_g_ref[...]) * (1.0 + sc_ref[...]) + sh_ref[...]

    def up_first():
        gs, us = [], []
        for r in range(n_blk):
            lo, hi = ext_bounds[r], ext_bounds[r + 1]
            m_lo, m_hi = max(lo, H), min(hi, H + tm)
            pieces = []
            if r == 0:
                pieces.append(jnp.where(i > 0, pre(xp_ref[...]), 0.0))
            pieces.append(pre(x_ref[m_lo - H:m_hi - H, :]))
            if r == n_blk - 1:
                pieces.append(jnp.where(i < n_tiles - 1, pre(xn_ref[...]), 0.0))
            hc = (pieces[0] if len(pieces) == 1 else jnp.concatenate(pieces, axis=0)).astype(BF16)
            h_scr[lo:hi, :] = hc
            gs.append(jnp.dot(hc, wg_ref[...], preferred_element_type=F32))
            us.append(jnp.dot(hc[m_lo - lo:m_hi - lo, :], wu_ref[...], preferred_element_type=F32))
        return jnp.concatenate(gs, axis=0), jnp.concatenate(us, axis=0)

    def step(first, last):
        if first:
            g_ext, u = up_first()
        else:
            g_ext = jnp.dot(h_scr[...], wg_ref[...], preferred_element_type=F32)
            u = jnp.dot(h_scr[H:H + tm, :], wu_ref[...], preferred_element_type=F32)
        g_prev = pltpu.roll(g_ext, 1, 0)[H:H + tm]
        g_next = pltpu.roll(g_ext, n_ext - 1, 0)[H:H + tm]
        if seq_len < tm:
            pos = lax.broadcasted_iota(jnp.int32, (tm, 1), 0) % seq_len
            g_prev = jnp.where(pos == 0, 0.0, g_prev)
            g_next = jnp.where(pos == seq_len - 1, 0.0, g_next)
        conv = g_prev * cw_ref[0:1, :] + g_ext[H:H + tm] * cw_ref[1:2, :] + g_next * cw_ref[2:3, :] + cb_ref[...]
        act = ((conv * _sigmoid(conv)) * u).astype(BF16)
        if not last:
            down = jnp.dot(act, wd_ref[...], preferred_element_type=F32)
            acc_scr[...] = down if first else acc_scr[...] + down
            return
        for r in range(n_blk):
            rs = slice(r * rb, (r + 1) * rb)
            down = jnp.dot(act[rs, :], wd_ref[...], preferred_element_type=F32)
            acc = down if first else acc_scr[rs, :] + down
            o_ref[rs, :] = x_ref[rs, :] + gate_ref[...] * _rms(acc, post_g_ref[...])

    @pl.when(j == 0)
    def _():
        step(True, n_ff == 1)

    if n_ff > 2:
        @pl.when(jnp.logical_and(j > 0, j < n_ff - 1))
        def _():
            step(False, False)

    if n_ff > 1:
        @pl.when(j == n_ff - 1)
        def _():
            step(False, True)


def _ffn(xs, mod, mod_row, pre_g_all, post_g_all, w_up_all, cv_w_all, cv_b_all, w_down_all, layer, tm, seq_len, tf=512):
    b, t, d = xs.shape
    assert seq_len == t or (tm % seq_len == 0 and t % tm == 0)
    d_ff = w_down_all.shape[1]
    n_ff = d_ff // tf
    n_tiles = t // tm
    hb = tm // FFN_HALO
    n_halo = t // FFN_HALO
    return pl.pallas_call(
        functools.partial(_ffn_kernel, tm=tm, n_tiles=n_tiles, n_ff=n_ff, seq_len=seq_len),
        grid=(b, n_tiles, n_ff),
        in_specs=[pl.BlockSpec((None, tm, d), lambda bb, i, j: (bb, i, 0)),
                  pl.BlockSpec((None, FFN_HALO, d), lambda bb, i, j: (bb, jnp.maximum(i * hb - 1, 0), 0)),
                  pl.BlockSpec((None, FFN_HALO, d), lambda bb, i, j: (bb, jnp.minimum((i + 1) * hb, n_halo - 1), 0)),
                  _mod_spec(layer, 3, d, mod_row), _mod_spec(layer, 4, d, mod_row), _mod_spec(layer, 5, d, mod_row),
                  _layer_row(layer, d), _layer_row(layer, d),
                  pl.BlockSpec((None, d, tf), lambda bb, i, j: (layer, 0, j)),
                  pl.BlockSpec((None, d, tf), lambda bb, i, j: (layer, 0, j + n_ff)),
                  pl.BlockSpec((None, cv_w_all.shape[1], tf), lambda bb, i, j: (layer, 0, j)),
                  pl.BlockSpec((None, 1, tf), lambda bb, i, j: (layer, 0, j)),
                  pl.BlockSpec((None, tf, d), lambda bb, i, j: (layer, j, 0))],
        out_specs=pl.BlockSpec((None, tm, d), lambda bb, i, j: (bb, i, 0)),
        out_shape=jax.ShapeDtypeStruct((b, t, d), F32),
        scratch_shapes=[pltpu.VMEM((tm + 2 * FFN_HALO, d), BF16), pltpu.VMEM((tm, d), F32)],
        compiler_params=_cparams(("parallel", "parallel", "arbitrary")),
        name="conv_glu_ffn",
    )(xs, xs, xs, mod, mod, mod, pre_g_all, post_g_all, w_up_all, w_up_all, cv_w_all, cv_b_all, w_down_all)


def _pack_w_in(w_in):
    depth, d, _ = w_in.shape
    main = jnp.concatenate([w_in[..., :4 * GROUP_W], w_in[..., 4 * GROUP_W + N_GATE:]], axis=-1).astype(BF16)
    gates = w_in[..., 4 * GROUP_W:4 * GROUP_W + N_GATE]
    gates = jnp.concatenate([gates, jnp.zeros((depth, d, LANE - N_GATE), w_in.dtype)], axis=-1).astype(BF16)
    return main, gates


def _rope_tables(t_lat, t_ctx):
    t = jnp.arange(t_lat)
    row = (t // GRID_W).astype(F32)
    colp = (t % GRID_W).astype(F32)
    axis_dim = HEAD_DIM // 2
    inv = jnp.power(ROPE_THETA, -jnp.arange(0, axis_dim, 2, dtype=F32) / axis_dim)
    ang = jnp.concatenate([row[:, None] * inv, colp[:, None] * inv], axis=-1)
    cos, sin = jnp.cos(ang), jnp.sin(ang)
    cosf = jnp.concatenate([cos, cos], axis=-1)
    sinf = jnp.concatenate([-sin, sin], axis=-1)
    cosf = jnp.concatenate([cosf, jnp.ones((t_ctx, HEAD_DIM), F32)], axis=0)
    sinf = jnp.concatenate([sinf, jnp.zeros((t_ctx, HEAD_DIM), F32)], axis=0)
    return cosf, sinf


def kernel(x, c, ctx, c_ctx, w_ada, b_ada, pre_mix_g, post_mix_g, w_in, ml_gate_b, ml_norm_g, sg_ln_g, sg_ln_b,
           sg_w, sg_b, cv_w, cv_b, cv_ln_g, cv_ln_b, at_qn_g, at_kn_g, w_out, pre_ffn_g, post_ffn_g, w_up,
           ffn_cv_w, ffn_cv_b, w_down):
    b, t_lat, d = x.shape
    t_ctx = ctx.shape[1]
    tt = t_lat + t_ctx
    depth = w_ada.shape[0]
    tm_in = min(1024, t_lat)
    tm_lat = min(512, t_lat)
    assert t_lat % tm_in == 0 and t_lat % t_ctx == 0 and t_ctx % CHUNK == 0 and t_ctx % 256 == 0

    n_rows = -(-(b + 1) // 8) * 8
    s_rows = jnp.concatenate([c, c_ctx[None, :], jnp.zeros((n_rows - b - 1, d), F32)], axis=0)
    mod = _ada(s_rows, w_ada, b_ada).reshape(depth, n_rows, 1, 6 * d)
    cosf, sinf = _rope_tables(t_lat, t_ctx)

    rows = lambda a: a.reshape(depth, 1, -1)
    w_main, w_gate = _pack_w_in(w_in)
    w_out_b, w_up_b, w_down_b = w_out.astype(BF16), w_up.astype(BF16), w_down.astype(BF16)
    gate_b = rows(jnp.concatenate([ml_gate_b, jnp.zeros((depth, LANE - N_GATE), F32)], axis=1))
    sg_w_b = sg_w.astype(BF16)
    sg_b_full = jnp.broadcast_to(sg_b[..., None], (depth, SG_GROUPS, CHUNK, LANE))
    pre_mix, post_mix, pre_ffn, post_ffn = rows(pre_mix_g), rows(post_mix_g), rows(pre_ffn_g), rows(post_ffn_g)
    ml_norm, sg_g, sg_bb, cv_bias, cv_g, cv_bb = (rows(ml_norm_g), rows(sg_ln_g), rows(sg_ln_b), rows(cv_b),
                                                  rows(cv_ln_g), rows(cv_ln_b))
    qn_g, kn_g, ffn_bias = rows(at_qn_g), rows(at_kn_g), rows(ffn_cv_b)

    xc = ctx
    for i in range(depth):
        need_ctx = i < depth - 1
        z, zg = _in_proj(x, xc, mod, pre_mix, w_main, w_gate, i, tm_in)
        ya = _mlstm(z, zg, gate_b, ml_norm, i, t_lat, t_ctx)

        mix_rows = tt if need_ctx else t_lat
        sg_tr = 2176 if (need_ctx and tt % 2176 == 0) else (2048 if t_lat % 2048 == 0 else 256)
        sg_tr = sg_tr if mix_rows % sg_tr == 0 else 256
        yb = _spatial_gating(z, sg_g, sg_bb, sg_w_b, sg_b_full, i, mix_rows, sg_tr)
        yc = _conformer(z, cv_w, cv_bias, cv_g, cv_bb, i, t_lat, mix_rows, 256)

        prep_tr = 1088 if tt % 1088 == 0 else 256
        qr, kr, vaug = _attn_prep(z, cosf, sinf, qn_g, kn_g, i, prep_tr)
        yd = _attention(qr, kr, vaug, 0, t_lat, 0, tt, min(1024, t_lat))

        x = _out_proj((ya, yb, yc, yd), (0, 0, 0, 0), w_out_b, x, mod, None, post_mix, i, tm_lat)
        x = _ffn(x, mod, None, pre_ffn, post_ffn, w_up_b, ffn_cv_w, ffn_bias, w_down_b, i, tm_lat, t_lat)
        if need_ctx:
            yd_c = _attention(qr, kr, vaug, t_lat, t_ctx, t_lat, t_ctx, t_ctx)
            xc = _out_proj((ya, yb, yc, yd_c), (t_lat, t_lat, t_lat, 0), w_out_b, xc, mod, b, post_mix, i, t_ctx)
            tm_ctx = t_ctx * max(1, min(b, 512 // t_ctx))
            tm_ctx = tm_ctx if (b * t_ctx) % tm_ctx == 0 else t_ctx
            xc = _ffn(xc.reshape(1, b * t_ctx, d), mod, b, pre_ffn, post_ffn, w_up_b, ffn_cv_w, ffn_bias, w_down_b,
                      i, tm_ctx, t_ctx).reshape(b, t_ctx, d)
    return x
```

```python
import functools
import math
from typing import NamedTuple

import jax
import jax.numpy as jnp
from jax import lax
from jax.experimental import pallas as pl
from jax.experimental.pallas import tpu as pltpu

F32 = jnp.float32
BF16 = jnp.bfloat16

EPS = 1e-6
LANE = 128
SUBLANES = 8
HEAD_DIM = 128
GROUP_W = 512
ML_HEADS = 4
CHUNK = 128
SG_GROUPS = 4
CONV_W = 31
CONV_HALO = 16
ATT_Q_HEADS = 4
ATT_KV_HEADS = 2
GRID_W = 64
ROPE_THETA = 10000.0
FFN_HALO = 16
N_GATE = 16
GATE_ROWS = 32

COL_MQ, COL_MK, COL_MV, COL_MO = 0, 512, 1024, 1536
COL_SU, COL_SV = 2048, 2560
COL_CA, COL_CG = 3072, 3584
COL_AQ, COL_AK, COL_AV = 4096, 4608, 4864

VMEM_LIMIT = 52 * 1024 * 1024


def _cparams(sem):
    return pltpu.CompilerParams(dimension_semantics=sem, vmem_limit_bytes=VMEM_LIMIT)


def _layer_row(layer, n):
    return pl.BlockSpec((None, 1, n), lambda *_: (layer, 0, 0))


def _mod_spec(layer, k, d, row=None):
    if row is None:
        return pl.BlockSpec((None, None, 1, d), lambda bb, *_: (layer, bb, 0, k))
    return pl.BlockSpec((None, None, 1, d), lambda *_: (layer, row, 0, k))


def _rms(x, g):
    ms = jnp.mean(x * x, axis=-1, keepdims=True)
    return x * lax.rsqrt(ms + EPS) * g


def _layernorm(x, g, b):
    mu = jnp.mean(x, axis=-1, keepdims=True)
    d = x - mu
    var = jnp.mean(d * d, axis=-1, keepdims=True)
    return d * lax.rsqrt(var + EPS) * g + b


def _sigmoid(x):
    return 1.0 / (1.0 + jnp.exp(-x))


def _gelu_tanh(x):
    c = math.sqrt(2.0 / math.pi)
    return x * (0.5 * (1.0 + jnp.tanh(c * (x + 0.044715 * (x * x * x)))))


def _log_sigmoid(x):
    return jnp.minimum(x, 0.0) - jnp.log1p(jnp.exp(-jnp.abs(x)))


def _ada_kernel(s_ref, w_ref, b_ref, o_ref):
    s = s_ref[...]
    s = s * _sigmoid(s)
    o_ref[...] = jnp.dot(s.astype(BF16), w_ref[...].astype(BF16), preferred_element_type=F32) + b_ref[...]


def _ada(s_rows, w_ada, b_ada, tn=1024):
    depth, d, n = w_ada.shape
    r = s_rows.shape[0]
    return pl.pallas_call(
        _ada_kernel,
        grid=(depth, n // tn),
        in_specs=[pl.BlockSpec((r, d), lambda l, j: (0, 0)),
                  pl.BlockSpec((None, d, tn), lambda l, j: (l, 0, j)),
                  pl.BlockSpec((None, 1, tn), lambda l, j: (l, 0, j))],
        out_specs=pl.BlockSpec((None, r, tn), lambda l, j: (l, 0, j)),
        out_shape=jax.ShapeDtypeStruct((depth, r, n), F32),
        compiler_params=_cparams(("parallel", "parallel")),
        name="ada_mod",
    )(s_rows, w_ada, b_ada.reshape(depth, 1, n))


def _in_proj_kernel(x_ref, xc_ref, sh_ref, sc_ref, shc_ref, scc_ref, g_ref, w_ref, wg_ref, o_ref, og_ref, h_scr,
                    *, tm, t_ctx, n_lat_tiles, rows_per_step):
    i = pl.program_id(1)
    first_col = pl.program_id(2) == 0
    is_lat = i < n_lat_tiles

    def pre(x, sh, sc):
        return (_rms(x, g_ref[...]) * (1.0 + sc) + sh).astype(BF16)

    def first_step(x_rows, sh, sc, n_rows):
        for r in range(n_rows // rows_per_step):
            sl = slice(r * rows_per_step, (r + 1) * rows_per_step)
            h = pre(x_rows[sl, :], sh[...], sc[...])
            h_scr[sl, :] = h
            o_ref[sl, :] = jnp.dot(h, w_ref[...], preferred_element_type=F32).astype(BF16)
            og_ref[sl, :] = jnp.dot(h, wg_ref[...], preferred_element_type=F32)

    @pl.when(jnp.logical_and(first_col, is_lat))
    def _():
        first_step(x_ref, sh_ref, sc_ref, tm)

    @pl.when(jnp.logical_and(jnp.logical_not(first_col), is_lat))
    def _():
        o_ref[...] = jnp.dot(h_scr[...], w_ref[...], preferred_element_type=F32).astype(BF16)

    @pl.when(jnp.logical_and(first_col, jnp.logical_not(is_lat)))
    def _():
        first_step(xc_ref, shc_ref, scc_ref, t_ctx)
        if t_ctx < tm:
            og_ref[t_ctx:, :] = jnp.zeros((tm - t_ctx, og_ref.shape[1]), F32)

    @pl.when(jnp.logical_and(jnp.logical_not(first_col), jnp.logical_not(is_lat)))
    def _():
        o_ref[0:t_ctx, :] = jnp.dot(h_scr[0:t_ctx, :], w_ref[...], preferred_element_type=F32).astype(BF16)

    if t_ctx < tm:
        @pl.when(jnp.logical_not(is_lat))
        def _():
            o_ref[t_ctx:, :] = jnp.zeros((tm - t_ctx, o_ref.shape[1]), BF16)


def _in_proj(x, xc, mod, g_all, w_all, wg_all, layer, tm, tn=1280):
    b, t_lat, d = x.shape
    t_ctx = xc.shape[1]
    tt = t_lat + t_ctx
    nz = w_all.shape[2]
    n_lat_tiles = t_lat // tm
    assert t_lat % tm == 0 and t_ctx <= tm
    return pl.pallas_call(
        functools.partial(_in_proj_kernel, tm=tm, t_ctx=t_ctx, n_lat_tiles=n_lat_tiles, rows_per_step=min(tm, 256)),
        grid=(b, n_lat_tiles + 1, nz // tn),
        in_specs=[pl.BlockSpec((None, tm, d), lambda bb, i, j: (bb, jnp.minimum(i, n_lat_tiles - 1), 0)),
                  pl.BlockSpec((None, t_ctx, d), lambda bb, i, j: (bb, 0, 0)),
                  _mod_spec(layer, 0, d), _mod_spec(layer, 1, d),
                  _mod_spec(layer, 0, d, row=b), _mod_spec(layer, 1, d, row=b),
                  _layer_row(layer, d),
                  pl.BlockSpec((None, d, tn), lambda bb, i, j: (layer, 0, j)),
                  pl.BlockSpec((None, d, LANE), lambda *_: (layer, 0, 0))],
        out_specs=[pl.BlockSpec((None, tm, tn), lambda bb, i, j: (bb, i, j)),
                   pl.BlockSpec((None, tm, LANE), lambda bb, i, j: (bb, i, 0))],
        out_shape=[jax.ShapeDtypeStruct((b, tt, nz), BF16), jax.ShapeDtypeStruct((b, tt, LANE), F32)],
        scratch_shapes=[pltpu.VMEM((tm, d), BF16)],
        compiler_params=_cparams(("parallel", "parallel", "arbitrary")),
        name="in_proj",
    )(x, xc, mod, mod, mod, mod, g_all, w_all, wg_all)


def _prefix_lanes(x, op, fill):
    lane = lax.broadcasted_iota(jnp.int32, x.shape, 1)
    sh = 1
    while sh < x.shape[1]:
        x = op(x, jnp.where(lane >= sh, pltpu.roll(x, sh, 1), fill))
        sh *= 2
    return x


def _suffix_lanes(x, op, fill):
    n = x.shape[1]
    lane = lax.broadcasted_iota(jnp.int32, x.shape, 1)
    sh = 1
    while sh < n:
        x = op(x, jnp.where(lane < n - sh, pltpu.roll(x, n - sh, 1), fill))
        sh *= 2
    return x


def _mlstm_kernel(q_ref, k_ref, v_ref, o_ref, gt_ref, gb_ref, ng_ref, out_ref,
                  rb_scr, gr_scr, vt_scr, u_scr, cp_scr, st_scr, sa_scr, sc_scr, sm_scr, *, n_lat, n_ctx):
    L = CHUNK
    nc = n_lat + n_ctx
    VT_ROWS = vt_scr.shape[1]
    head = pl.program_id(1)
    row = lax.broadcasted_iota(jnp.int32, (L, L), 0)
    col = lax.broadcasted_iota(jnp.int32, (L, L), 1)
    lower = col <= row
    upper = col >= row
    k_scale = HEAD_DIM ** -0.5
    HALF = GATE_ROWS // 2

    def rows_of(c):
        return pl.ds(pl.multiple_of(c * L, L), L)

    def step_of(c, direction):
        if direction == 0:
            return jnp.where(c < n_lat, c + n_ctx, c - n_lat)
        return nc - 1 - c

    @pl.when(head == 0)
    def _():
        lower_f = lower.astype(F32)
        upper_f = upper.astype(F32)
        n_all = nc * HALF
        rg = lax.broadcasted_iota(jnp.int32, (n_all, L), 0) % HALF
        is_cum = (rg % 8) >= 4
        is_bwd = rg >= 8

        def to_rows(c, carry):
            gr_scr[c, 0:HALF, :] = (gt_ref[rows_of(c), :] + gb_ref[...]).T[0:HALF, :]
            return carry

        lax.fori_loop(0, nc, to_rows, 0, unroll=2)

        G = gr_scr[:, 0:HALF, :].reshape(n_all, L)
        Gf = _log_sigmoid(G)
        Bf = jnp.dot(Gf, upper_f, precision=lax.Precision.HIGHEST, preferred_element_type=F32)
        Bb = jnp.dot(Gf, lower_f, precision=lax.Precision.HIGHEST, preferred_element_type=F32)
        Bsel = jnp.where(is_bwd, Bb, Bf)
        R = G - pltpu.roll(Bsel, n_all - 4, 0)
        CM = jnp.where(is_bwd, _suffix_lanes(R, jnp.maximum, -jnp.inf), _prefix_lanes(R, jnp.maximum, -jnp.inf))
        gr_scr[:, 0:HALF, :] = jnp.where(is_cum, Bsel, G).reshape(nc, HALF, L)
        gr_scr[:, HALF:GATE_ROWS, :] = jnp.where(is_cum, pltpu.roll(CM, 4, 0), R).reshape(nc, HALF, L)

    row_g = lax.broadcasted_iota(jnp.int32, (GATE_ROWS, L), 0)
    lane_r = lax.broadcasted_iota(jnp.int32, (1, LANE), 1)

    def gate_row(x, idx):
        return jnp.sum(jnp.where(row_g == idx, x, 0.0), axis=0, keepdims=True)

    st_scr[...] = jnp.zeros_like(st_scr)
    ones_rows = jnp.ones((VT_ROWS - HEAD_DIM, L), F32)

    def summarize(c, carry):
        vt = jnp.concatenate([v_ref[rows_of(c), :].astype(F32).T, ones_rows], axis=0)
        vt_scr[c] = vt.astype(BF16)
        k = k_ref[rows_of(c), :]
        grv = gr_scr[c]
        for direction in (0, 1):
            ci = direction * 8 + head
            b_row = gate_row(grv, ci + 4)
            i_row = gate_row(grv, ci)
            total = b_row[:, L - 1:L] if direction == 0 else b_row[:, 0:1]
            g_row = total - b_row + i_row
            mg = jnp.max(g_row, axis=1, keepdims=True)
            w_row = jnp.exp(g_row - mg) * k_scale
            step = step_of(c, direction)
            u_scr[direction * nc + step] = jnp.dot((vt * w_row).astype(BF16), k, preferred_element_type=F32)
            st_scr[direction, pl.ds(step, 1), :] = jnp.where(lane_r < 64, total, mg)
            rb_scr[direction * nc + c] = jnp.broadcast_to(gate_row(grv, 16 + ci), (L, L)).T
        return carry

    lax.fori_loop(0, nc, summarize, 0, unroll=4)

    lane8 = lax.broadcasted_iota(jnp.int32, (8, LANE), 1)
    for direction in (0, 1):
        xt = st_scr[direction].T
        valid = lane8 < nc
        T = jnp.where(valid, jnp.broadcast_to(xt[0:1, :], (8, LANE)), 0.0)
        G = jnp.where(valid, jnp.broadcast_to(xt[64:65, :], (8, LANE)), -jnp.inf)
        PT = _prefix_lanes(T, jnp.add, 0.0)
        mvec = PT + jnp.maximum(_prefix_lanes(G - PT, jnp.maximum, -jnp.inf), 0.0)
        mprev = jnp.where(lane8 >= 1, pltpu.roll(mvec, 1, 1), 0.0)
        a = jnp.exp(T + mprev - mvec)
        c2 = jnp.exp(G - mvec)
        sa_scr[direction] = jnp.broadcast_to(a[0:1, :], (LANE, LANE)).T
        sc_scr[direction] = jnp.broadcast_to(c2[0:1, :], (LANE, LANE)).T
        sm_scr[direction] = jnp.broadcast_to(mprev[0:1, :], (LANE, LANE)).T

    def scan(s, carry):
        new = []
        for direction in (0, 1):
            ct = carry[direction]
            idx = direction * nc + s
            cp_scr[idx] = ct.astype(BF16)
            a = sa_scr[direction, pl.ds(s, 1), :]
            c2 = sc_scr[direction, pl.ds(s, 1), :]
            new.append(a * ct + c2 * u_scr[idx])
        return tuple(new)

    zero = jnp.zeros((VT_ROWS, HEAD_DIM), F32)
    lax.fori_loop(0, nc, scan, (zero, zero))

    nt_dims = (((1,), (1,)), ((), ()))

    def emit(c, carry):
        q = q_ref[rows_of(c), :]
        vt = vt_scr[c]
        grv = gr_scr[c]
        qk_t = lax.dot_general(k_ref[rows_of(c), :], q, nt_dims, preferred_element_type=F32) * k_scale
        h_t = jnp.zeros((HEAD_DIM, L), F32)
        for direction in (0, 1):
            ci = direction * 8 + head
            step = step_of(c, direction)
            r_col = rb_scr[direction * nc + c]
            cm_row = gate_row(grv, 16 + ci + 4)
            b_row = gate_row(grv, ci + 4)
            p_t = jnp.exp(jnp.where(upper if direction == 0 else lower, r_col - cm_row, -jnp.inf))
            intra = jnp.dot(vt, (qk_t * p_t).astype(BF16), preferred_element_type=F32)
            inter = lax.dot_general(cp_scr[direction * nc + step], q, nt_dims, preferred_element_type=F32)
            mp = sm_scr[direction, pl.ds(step, 1), :]
            mx = jnp.maximum(mp, cm_row)
            e1 = jnp.exp(mp - mx)
            e2 = jnp.exp(cm_row - mx)
            num = e1 * inter[:HEAD_DIM] + e2 * intra[:HEAD_DIM]
            den = e1 * inter[HEAD_DIM:HEAD_DIM + 1] + e2 * intra[HEAD_DIM:HEAD_DIM + 1]
            h_t = h_t + num * (1.0 / jnp.maximum(jnp.abs(den), jnp.exp(-(b_row + mx))))
        mu = jnp.mean(h_t, axis=0, keepdims=True)
        dev = h_t - mu
        var = jnp.mean(dev * dev, axis=0, keepdims=True)
        hn = (dev * lax.rsqrt(var + EPS)).T * ng_ref[...]
        out_ref[rows_of(c), :] = (hn * _sigmoid(o_ref[rows_of(c), :].astype(F32))).astype(BF16)
        return carry

    lax.fori_loop(0, nc, emit, 0, unroll=8)


def _mlstm(z, zg, gate_b_all, norm_g_all, layer, t_lat, t_ctx):
    b, tt, _ = z.shape
    nc = tt // CHUNK
    assert nc <= 64
    vt_rows = HEAD_DIM + 16
    blk = lambda off: pl.BlockSpec((None, tt, LANE), lambda bb, h: (bb, 0, off // LANE + h))
    return pl.pallas_call(
        functools.partial(_mlstm_kernel, n_lat=t_lat // CHUNK, n_ctx=t_ctx // CHUNK),
        grid=(b, ML_HEADS),
        in_specs=[blk(COL_MQ), blk(COL_MK), blk(COL_MV), blk(COL_MO),
                  pl.BlockSpec((None, tt, LANE), lambda bb, h: (bb, 0, 0)),
                  _layer_row(layer, LANE),
                  pl.BlockSpec((None, 1, LANE), lambda bb, h: (layer, 0, h))],
        out_specs=pl.BlockSpec((None, tt, LANE), lambda bb, h: (bb, 0, h)),
        out_shape=jax.ShapeDtypeStruct((b, tt, GROUP_W), BF16),
        scratch_shapes=[pltpu.VMEM((2 * nc, CHUNK, CHUNK), F32),
                        pltpu.VMEM((nc, GATE_ROWS, LANE), F32),
                        pltpu.VMEM((nc, vt_rows, CHUNK), BF16),
                        pltpu.VMEM((2 * nc, vt_rows, HEAD_DIM), F32),
                        pltpu.VMEM((2 * nc, vt_rows, HEAD_DIM), BF16),
                        pltpu.VMEM((2, LANE, LANE), F32),
                        pltpu.VMEM((2, LANE, LANE), F32),
                        pltpu.VMEM((2, LANE, LANE), F32),
                        pltpu.VMEM((2, LANE, LANE), F32)],
        compiler_params=_cparams(("parallel", "arbitrary")),
        name="mlstm",
    )(z, z, z, z, zg, gate_b_all, norm_g_all)


def _sg_kernel(u_ref, v_ref, lng_ref, lnb_ref, w_ref, bs_ref, o_ref, *, n_chunks):
    def body(c, carry):
        sl = pl.ds(pl.multiple_of(c * CHUNK, CHUNK), CHUNK)
        u = _gelu_tanh(u_ref[sl, :].astype(F32))
        v = _layernorm(_gelu_tanh(v_ref[sl, :].astype(F32)), lng_ref[...], lnb_ref[...]).astype(BF16)
        for g in range(SG_GROUPS):
            gs = slice(g * LANE, (g + 1) * LANE)
            mixed = jnp.dot(w_ref[g], v[:, gs], preferred_element_type=F32) + bs_ref[g]
            o_ref[sl, gs] = (u[:, gs] * mixed).astype(BF16)
        return carry

    lax.fori_loop(0, n_chunks, body, 0, unroll=2)


def _spatial_gating(z, ln_g_all, ln_b_all, w_s_all, b_s_all, layer, n_rows, tr):
    b = z.shape[0]
    blk = lambda off: pl.BlockSpec((None, tr, GROUP_W), lambda bb, i: (bb, i, off // GROUP_W))
    per_layer = lambda *_: (layer, 0, 0, 0)
    return pl.pallas_call(
        functools.partial(_sg_kernel, n_chunks=tr // CHUNK),
        grid=(b, n_rows // tr),
        in_specs=[blk(COL_SU), blk(COL_SV),
                  _layer_row(layer, GROUP_W), _layer_row(layer, GROUP_W),
                  pl.BlockSpec((None, SG_GROUPS, CHUNK, CHUNK), per_layer),
                  pl.BlockSpec((None, SG_GROUPS, CHUNK, LANE), per_layer)],
        out_specs=pl.BlockSpec((None, tr, GROUP_W), lambda bb, i: (bb, i, 0)),
        out_shape=jax.ShapeDtypeStruct((b, n_rows, GROUP_W), BF16),
        compiler_params=_cparams(("parallel", "parallel")),
        name="spatial_gating",
    )(z, z, ln_g_all, ln_b_all, w_s_all, b_s_all)


def _conf_kernel(a_ref, g_ref, ap_ref, gp_ref, an_ref, gn_ref, w_ref, b_ref, lng_ref, lnb_ref, o_ref,
                 y_scr, ysh_scr, cv_scr, *, tc, seq_starts, seq_ends):
    i = pl.program_id(1)
    has_prev = functools.reduce(jnp.logical_and, [i != s for s in seq_starts])
    has_next = functools.reduce(jnp.logical_and, [i != e for e in seq_ends])
    H = CONV_HALO

    def glu(a, g):
        return a[...].astype(F32) * _sigmoid(g[...].astype(F32))

    y_scr[H:H + tc, :] = glu(a_ref, g_ref)
    y_scr[0:H, :] = jnp.where(has_prev, glu(ap_ref, gp_ref), 0.0)
    y_scr[H + tc:, :] = jnp.where(has_next, glu(an_ref, gn_ref), 0.0)
    n_sh = ysh_scr.shape[1]
    for j in range(SUBLANES):
        ysh_scr[j] = y_scr[j:j + n_sh, :]
    rb = 64
    base = H - (CONV_W // 2)
    for cb in range(GROUP_W // LANE):
        cs = slice(cb * LANE, (cb + 1) * LANE)
        for r in range(tc // rb):
            acc = jnp.zeros((rb, LANE), F32)
            for k in range(CONV_W):
                a, j = divmod(base + k, SUBLANES)
                acc = acc + ysh_scr[j, r * rb + SUBLANES * a:r * rb + SUBLANES * a + rb, cs] * w_ref[k:k + 1, cs]
            cv_scr[r * rb:(r + 1) * rb, cs] = acc
    y = _layernorm(cv_scr[...] + b_ref[...], lng_ref[...], lnb_ref[...])
    o_ref[...] = (y * _sigmoid(y)).astype(BF16)


def _conformer(z, w_dw_all, b_dw_all, ln_g_all, ln_b_all, layer, t_lat, n_rows, tc):
    b = z.shape[0]
    hb = tc // CONV_HALO
    n_tiles = n_rows // tc
    n_halo = z.shape[1] // CONV_HALO
    ca, cg = COL_CA // GROUP_W, COL_CG // GROUP_W
    cur = lambda cblk: pl.BlockSpec((None, tc, GROUP_W), lambda bb, i: (bb, i, cblk))
    prev = lambda cblk: pl.BlockSpec((None, CONV_HALO, GROUP_W),
                                     lambda bb, i: (bb, jnp.maximum(i * hb - 1, 0), cblk))
    nxt = lambda cblk: pl.BlockSpec((None, CONV_HALO, GROUP_W),
                                    lambda bb, i: (bb, jnp.minimum((i + 1) * hb, n_halo - 1), cblk))
    seq_starts = (0, t_lat // tc)
    seq_ends = (t_lat // tc - 1, z.shape[1] // tc - 1)
    return pl.pallas_call(
        functools.partial(_conf_kernel, tc=tc, seq_starts=seq_starts, seq_ends=seq_ends),
        grid=(b, n_tiles),
        in_specs=[cur(ca), cur(cg), prev(ca), prev(cg), nxt(ca), nxt(cg),
                  pl.BlockSpec((None, CONV_W, GROUP_W), lambda *_: (layer, 0, 0)), _layer_row(layer, GROUP_W),
                  _layer_row(layer, GROUP_W), _layer_row(layer, GROUP_W)],
        out_specs=pl.BlockSpec((None, tc, GROUP_W), lambda bb, i: (bb, i, 0)),
        out_shape=jax.ShapeDtypeStruct((b, n_rows, GROUP_W), BF16),
        scratch_shapes=[pltpu.VMEM((tc + 2 * CONV_HALO, GROUP_W), F32),
                        pltpu.VMEM((SUBLANES, tc + 2 * CONV_HALO - SUBLANES, GROUP_W), F32),
                        pltpu.VMEM((tc, GROUP_W), F32)],
        compiler_params=_cparams(("parallel", "parallel")),
        name="conformer_conv",
    )(z, z, z, z, z, z, w_dw_all, b_dw_all, ln_g_all, ln_b_all)


def _attn_prep_kernel(q_ref, k_ref, v_ref, cos_ref, sin_ref, qg_ref, kg_ref, qo_ref, ko_ref, vo_ref):
    cosf = cos_ref[...]
    sinf = sin_ref[...]

    def norm_rope(x, g, scale):
        y = _rms(x.astype(F32), g)
        return (y * cosf + pltpu.roll(y, HEAD_DIM // 2, 1) * sinf) * scale

    q_scale = HEAD_DIM ** -0.5 * math.log2(math.e)
    for h in range(ATT_Q_HEADS):
        hs = slice(h * HEAD_DIM, (h + 1) * HEAD_DIM)
        qo_ref[:, hs] = norm_rope(q_ref[:, hs], qg_ref[...], q_scale).astype(BF16)
    ones = jnp.ones((v_ref.shape[0], HEAD_DIM), BF16)
    for h in range(ATT_KV_HEADS):
        hs = slice(h * HEAD_DIM, (h + 1) * HEAD_DIM)
        ko_ref[:, hs] = norm_rope(k_ref[:, hs], kg_ref[...], 1.0).astype(BF16)
        vo_ref[:, 2 * h * HEAD_DIM:(2 * h + 1) * HEAD_DIM] = v_ref[:, hs]
        vo_ref[:, (2 * h + 1) * HEAD_DIM:(2 * h + 2) * HEAD_DIM] = ones


def _attn_prep(z, cosf, sinf, qg_all, kg_all, layer, tr):
    b, tt, _ = z.shape
    qw, kw = ATT_Q_HEADS * HEAD_DIM, ATT_KV_HEADS * HEAD_DIM
    return pl.pallas_call(
        _attn_prep_kernel,
        grid=(b, tt // tr),
        in_specs=[pl.BlockSpec((None, tr, qw), lambda bb, i: (bb, i, COL_AQ // qw)),
                  pl.BlockSpec((None, tr, kw), lambda bb, i: (bb, i, COL_AK // kw)),
                  pl.BlockSpec((None, tr, kw), lambda bb, i: (bb, i, COL_AV // kw)),
                  pl.BlockSpec((tr, HEAD_DIM), lambda bb, i: (i, 0)),
                  pl.BlockSpec((tr, HEAD_DIM), lambda bb, i: (i, 0)),
                  _layer_row(layer, HEAD_DIM), _layer_row(layer, HEAD_DIM)],
        out_specs=[pl.BlockSpec((None, tr, qw), lambda bb, i: (bb, i, 0)),
                   pl.BlockSpec((None, tr, kw), lambda bb, i: (bb, i, 0)),
                   pl.BlockSpec((None, tr, 2 * kw), lambda bb, i: (bb, i, 0))],
        out_shape=[jax.ShapeDtypeStruct((b, tt, qw), BF16),
                   jax.ShapeDtypeStruct((b, tt, kw), BF16),
                   jax.ShapeDtypeStruct((b, tt, 2 * kw), BF16)],
        compiler_params=_cparams(("parallel", "parallel")),
        name="attn_prep",
    )(z, z, z, cosf, sinf, qg_all, kg_all)


def _attn_kernel(q_ref, k_ref, v_ref, o_ref, *, tq, sub):
    k = k_ref[...]
    vaug = v_ref[...]
    for g in range(ATT_Q_HEADS // ATT_KV_HEADS):
        hs = slice(g * HEAD_DIM, (g + 1) * HEAD_DIM)
        for r in range(tq // sub):
            rs = slice(r * sub, (r + 1) * sub)
            s = lax.dot_general(q_ref[rs, hs], k, (((1,), (1,)), ((), ())), preferred_element_type=F32)
            p = jnp.exp2(s - jnp.max(s, axis=-1, keepdims=True)).astype(BF16)
            res = jnp.dot(p, vaug, preferred_element_type=F32)
            o_ref[rs, hs] = (res[:, :HEAD_DIM] / res[:, HEAD_DIM:]).astype(BF16)


def _attention(qr, kr, vaug, q_row0, n_q, k_row0, n_k, tq):
    b = qr.shape[0]
    gw = (ATT_Q_HEADS // ATT_KV_HEADS) * HEAD_DIM
    assert q_row0 % tq == 0 and n_q % tq == 0 and k_row0 % n_k == 0
    qb, kb = q_row0 // tq, k_row0 // n_k
    return pl.pallas_call(
        functools.partial(_attn_kernel, tq=tq, sub=min(tq, 128)),
        grid=(b, ATT_KV_HEADS, n_q // tq),
        in_specs=[pl.BlockSpec((None, tq, gw), lambda bb, h, i: (bb, i + qb, h)),
                  pl.BlockSpec((None, n_k, HEAD_DIM), lambda bb, h, i: (bb, kb, h)),
                  pl.BlockSpec((None, n_k, 2 * HEAD_DIM), lambda bb, h, i: (bb, kb, h))],
        out_specs=pl.BlockSpec((None, tq, gw), lambda bb, h, i: (bb, i, h)),
        out_shape=jax.ShapeDtypeStruct((b, n_q, ATT_Q_HEADS * HEAD_DIM), BF16),
        compiler_params=_cparams(("parallel", "parallel", "arbitrary")),
        name="attention",
    )(qr, kr, vaug)


def _out_proj_kernel(ya_ref, yb_ref, yc_ref, yd_ref, w_ref, x_ref, gate_ref, pg_ref, o_ref):
    acc = jnp.dot(ya_ref[...], w_ref[0 * GROUP_W:1 * GROUP_W, :], preferred_element_type=F32)
    acc += jnp.dot(yb_ref[...], w_ref[1 * GROUP_W:2 * GROUP_W, :], preferred_element_type=F32)
    acc += jnp.dot(yc_ref[...], w_ref[2 * GROUP_W:3 * GROUP_W, :], preferred_element_type=F32)
    acc += jnp.dot(yd_ref[...], w_ref[3 * GROUP_W:4 * GROUP_W, :], preferred_element_type=F32)
    o_ref[...] = x_ref[...] + gate_ref[...] * _rms(acc, pg_ref[...])


def _out_proj(ys, y_row_offs, w_out_all, xs, mod, mod_row, post_g_all, layer, tm):
    b, t, d = xs.shape
    assert all(off % tm == 0 for off in y_row_offs)
    yspecs = [pl.BlockSpec((None, tm, GROUP_W), functools.partial(lambda bb, i, o: (bb, i + o, 0), o=off // tm))
              for off in y_row_offs]
    return pl.pallas_call(
        _out_proj_kernel,
        grid=(b, t // tm),
        in_specs=[*yspecs,
                  pl.BlockSpec((None,) + w_out_all.shape[1:], lambda *_: (layer, 0, 0)),
                  pl.BlockSpec((None, tm, d), lambda bb, i: (bb, i, 0)),
                  _mod_spec(layer, 2, d, mod_row),
                  _layer_row(layer, d)],
        out_specs=pl.BlockSpec((None, tm, d), lambda bb, i: (bb, i, 0)),
        out_shape=jax.ShapeDtypeStruct((b, t, d), F32),
        compiler_params=_cparams(("parallel", "parallel")),
        name="out_proj",
    )(*ys, w_out_all, xs, mod, post_g_all)


def _ffn_kernel(x_ref, xp_ref, xn_ref, sh_ref, sc_ref, gate_ref, pre_g_ref, post_g_ref, wg_ref, wu_ref,
                cw_ref, cb_ref, wd_ref, o_ref, h_scr, acc_scr, *, tm, n_tiles, n_ff, seq_len):
    i = pl.program_id(1)
    j = pl.program_id(2)
    H = FFN_HALO
    n_ext = tm + 2 * H
    n_blk = 4 if tm % 64 == 0 else 1
    rb = tm // n_blk
    ext_bounds = [0] + [H + r * rb for r in range(1, n_blk)] + [n_ext]

    def pre(x):
        return _rms(x, pre_g_ref[...]) * (1.0 + sc_ref[...]) + sh_ref[...]

    def up_first():
        gs, us = [], []
        for r in range(n_blk):
            lo, hi = ext_bounds[r], ext_bounds[r + 1]
            m_lo, m_hi = max(lo, H), min(hi, H + tm)
            pieces = []
            if r == 0:
                pieces.append(jnp.where(i > 0, pre(xp_ref[...]), 0.0))
            pieces.append(pre(x_ref[m_lo - H:m_hi - H, :]))
            if r == n_blk - 1:
                pieces.append(jnp.where(i < n_tiles - 1, pre(xn_ref[...]), 0.0))
            hc = (pieces[0] if len(pieces) == 1 else jnp.concatenate(pieces, axis=0)).astype(BF16)
            h_scr[lo:hi, :] = hc
            gs.append(jnp.dot(hc, wg_ref[...], preferred_element_type=F32))
            us.append(jnp.dot(hc[m_lo - lo:m_hi - lo, :], wu_ref[...], preferred_element_type=F32))
        return jnp.concatenate(gs, axis=0), jnp.concatenate(us, axis=0)

    def step(first, last):
        if first:
            g_ext, u = up_first()
        else:
            g_ext = jnp.dot(h_scr[...], wg_ref[...], preferred_element_type=F32)
            u = jnp.dot(h_scr[H:H + tm, :], wu_ref[...], preferred_element_type=F32)
        g_prev = pltpu.roll(g_ext, 1, 0)[H:H + tm]
        g_next = pltpu.roll(g_ext, n_ext - 1, 0)[H:H + tm]
        if seq_len < tm:
            pos = lax.broadcasted_iota(jnp.int32, (tm, 1), 0) % seq_len
            g_prev = jnp.where(pos == 0, 0.0, g_prev)
            g_next = jnp.where(pos == seq_len - 1, 0.0, g_next)
        conv = g_prev * cw_ref[0:1, :] + g_ext[H:H + tm] * cw_ref[1:2, :] + g_next * cw_ref[2:3, :] + cb_ref[...]
        act = ((conv * _sigmoid(conv)) * u).astype(BF16)
        if not last:
            down = jnp.dot(act, wd_ref[...], preferred_element_type=F32)
            acc_scr[...] = down if first else acc_scr[...] + down
            return
        for r in range(n_blk):
            rs = slice(r * rb, (r + 1) * rb)
            down = jnp.dot(act[rs, :], wd_ref[...], preferred_element_type=F32)
            acc = down if first else acc_scr[rs, :] + down
            o_ref[rs, :] = x_ref[rs, :] + gate_ref[...] * _rms(acc, post_g_ref[...])

    @pl.when(j == 0)
    def _():
        step(True, n_ff == 1)

    if n_ff > 2:
        @pl.when(jnp.logical_and(j > 0, j < n_ff - 1))
        def _():
            step(False, False)

    if n_ff > 1:
        @pl.when(j == n_ff - 1)
        def _():
            step(False, True)


def _ffn(xs, mod, mod_row, pre_g_all, post_g_all, w_up_all, cv_w_all, cv_b_all, w_down_all, layer, tm, seq_len, tf=512):
    b, t, d = xs.shape
    assert seq_len == t or (tm % seq_len == 0 and t % tm == 0)
    d_ff = w_down_all.shape[1]
    n_ff = d_ff // tf
    n_tiles = t // tm
    hb = tm // FFN_HALO
    n_halo = t // FFN_HALO
    return pl.pallas_call(
        functools.partial(_ffn_kernel, tm=tm, n_tiles=n_tiles, n_ff=n_ff, seq_len=seq_len),
        grid=(b, n_tiles, n_ff),
        in_specs=[pl.BlockSpec((None, tm, d), lambda bb, i, j: (bb, i, 0)),
                  pl.BlockSpec((None, FFN_HALO, d), lambda bb, i, j: (bb, jnp.maximum(i * hb - 1, 0), 0)),
                  pl.BlockSpec((None, FFN_HALO, d), lambda bb, i, j: (bb, jnp.minimum((i + 1) * hb, n_halo - 1), 0)),
                  _mod_spec(layer, 3, d, mod_row), _mod_spec(layer, 4, d, mod_row), _mod_spec(layer, 5, d, mod_row),
                  _layer_row(layer, d), _layer_row(layer, d),
                  pl.BlockSpec((None, d, tf), lambda bb, i, j: (layer, 0, j)),
                  pl.BlockSpec((None, d, tf), lambda bb, i, j: (layer, 0, j + n_ff)),
                  pl.BlockSpec((None, cv_w_all.shape[1], tf), lambda bb, i, j: (layer, 0, j)),
                  pl.BlockSpec((None, 1, tf), lambda bb, i, j: (layer, 0, j)),
                  pl.BlockSpec((None, tf, d), lambda bb, i, j: (layer, j, 0))],
        out_specs=pl.BlockSpec((None, tm, d), lambda bb, i, j: (bb, i, 0)),
        out_shape=jax.ShapeDtypeStruct((b, t, d), F32),
        scratch_shapes=[pltpu.VMEM((tm + 2 * FFN_HALO, d), BF16), pltpu.VMEM((tm, d), F32)],
        compiler_params=_cparams(("parallel", "parallel", "arbitrary")),
        name="conv_glu_ffn",
    )(xs, xs, xs, mod, mod, mod, pre_g_all, post_g_all, w_up_all, w_up_all, cv_w_all, cv_b_all, w_down_all)


def _pack_w_in(w_in):
    depth, d, _ = w_in.shape
    main = jnp.concatenate([w_in[..., :4 * GROUP_W], w_in[..., 4 * GROUP_W + N_GATE:]], axis=-1).astype(BF16)
    gates = w_in[..., 4 * GROUP_W:4 * GROUP_W + N_GATE]
    gates = jnp.concatenate([gates, jnp.zeros((depth, d, LANE - N_GATE), w_in.dtype)], axis=-1).astype(BF16)
    return main, gates


class _Tiles(NamedTuple):
    in_proj: int
    rows: int
    ffn_ctx: int
    attn_q: int
    attn_prep: int
    sg_all: int
    sg_lat: int
    conf: int


def _largest_tile(n, cap, step):
    best = step
    for t in range(step, min(n, cap) + 1, step):
        if n % t == 0:
            best = t
    return best


def _choose_tiles(b, t_lat, t_ctx):
    tt = t_lat + t_ctx
    seqs_per_tile = max(1, min(b, 512 // t_ctx))
    ffn_ctx = t_ctx * seqs_per_tile if b % seqs_per_tile == 0 else t_ctx
    return _Tiles(in_proj=_largest_tile(t_lat, 1024, t_ctx), rows=_largest_tile(t_lat, 512, 256), ffn_ctx=ffn_ctx,
                  attn_q=_largest_tile(t_lat, 1024, 256), attn_prep=_largest_tile(tt, 1088, 64),
                  sg_all=_largest_tile(tt, 2176, CHUNK), sg_lat=_largest_tile(t_lat, 2048, CHUNK),
                  conf=_largest_tile(t_ctx, 256, 64))


def _rope_tables(t_lat, t_ctx):
    t = jnp.arange(t_lat)
    row = (t // GRID_W).astype(F32)
    colp = (t % GRID_W).astype(F32)
    axis_dim = HEAD_DIM // 2
    inv = jnp.power(ROPE_THETA, -jnp.arange(0, axis_dim, 2, dtype=F32) / axis_dim)
    ang = jnp.concatenate([row[:, None] * inv, colp[:, None] * inv], axis=-1)
    cos, sin = jnp.cos(ang), jnp.sin(ang)
    cosf = jnp.concatenate([cos, cos], axis=-1)
    sinf = jnp.concatenate([-sin, sin], axis=-1)
    cosf = jnp.concatenate([cosf, jnp.ones((t_ctx, HEAD_DIM), F32)], axis=0)
    sinf = jnp.concatenate([sinf, jnp.zeros((t_ctx, HEAD_DIM), F32)], axis=0)
    return cosf, sinf


def kernel(x, c, ctx, c_ctx, w_ada, b_ada, pre_mix_g, post_mix_g, w_in, ml_gate_b, ml_norm_g, sg_ln_g, sg_ln_b,
           sg_w, sg_b, cv_w, cv_b, cv_ln_g, cv_ln_b, at_qn_g, at_kn_g, w_out, pre_ffn_g, post_ffn_g, w_up,
           ffn_cv_w, ffn_cv_b, w_down):
    b, t_lat, d = x.shape
    t_ctx = ctx.shape[1]
    tt = t_lat + t_ctx
    depth = w_ada.shape[0]
    assert t_lat % t_ctx == 0 and t_ctx % 256 == 0
    tiles = _choose_tiles(b, t_lat, t_ctx)

    n_rows = -(-(b + 1) // 8) * 8
    s_rows = jnp.concatenate([c, c_ctx[None, :], jnp.zeros((n_rows - b - 1, d), F32)], axis=0)
    mod = _ada(s_rows, w_ada, b_ada).reshape(depth, n_rows, 1, 6 * d)
    cosf, sinf = _rope_tables(t_lat, t_ctx)

    rows = lambda a: a.reshape(depth, 1, -1)
    w_main, w_gate = _pack_w_in(w_in)
    w_out_b, w_up_b, w_down_b = w_out.astype(BF16), w_up.astype(BF16), w_down.astype(BF16)
    gate_b = rows(jnp.concatenate([ml_gate_b, jnp.zeros((depth, LANE - N_GATE), F32)], axis=1))
    sg_w_b = sg_w.astype(BF16)
    sg_b_full = jnp.broadcast_to(sg_b[..., None], (depth, SG_GROUPS, CHUNK, LANE))
    pre_mix, post_mix, pre_ffn, post_ffn = rows(pre_mix_g), rows(post_mix_g), rows(pre_ffn_g), rows(post_ffn_g)
    ml_norm, sg_g, sg_bb, cv_bias, cv_g, cv_bb = (rows(ml_norm_g), rows(sg_ln_g), rows(sg_ln_b), rows(cv_b),
                                                  rows(cv_ln_g), rows(cv_ln_b))
    qn_g, kn_g, ffn_bias = rows(at_qn_g), rows(at_kn_g), rows(ffn_cv_b)

    xc = ctx
    for i in range(depth):
        need_ctx = i < depth - 1
        z, zg = _in_proj(x, xc, mod, pre_mix, w_main, w_gate, i, tiles.in_proj)
        ya = _mlstm(z, zg, gate_b, ml_norm, i, t_lat, t_ctx)

        mix_rows = tt if need_ctx else t_lat
        yb = _spatial_gating(z, sg_g, sg_bb, sg_w_b, sg_b_full, i, mix_rows, tiles.sg_all if need_ctx else tiles.sg_lat)
        yc = _conformer(z, cv_w, cv_bias, cv_g, cv_bb, i, t_lat, mix_rows, tiles.conf)

        qr, kr, vaug = _attn_prep(z, cosf, sinf, qn_g, kn_g, i, tiles.attn_prep)
        yd = _attention(qr, kr, vaug, 0, t_lat, 0, tt, tiles.attn_q)

        x = _out_proj((ya, yb, yc, yd), (0, 0, 0, 0), w_out_b, x, mod, None, post_mix, i, tiles.rows)
        x = _ffn(x, mod, None, pre_ffn, post_ffn, w_up_b, ffn_cv_w, ffn_bias, w_down_b, i, tiles.rows, t_lat)
        if need_ctx:
            yd_c = _attention(qr, kr, vaug, t_lat, t_ctx, t_lat, t_ctx, t_ctx)
            xc = _out_proj((ya, yb, yc, yd_c), (t_lat, t_lat, t_lat, 0), w_out_b, xc, mod, b, post_mix, i, t_ctx)
            xc = _ffn(xc.reshape(1, b * t_ctx, d), mod, b, pre_ffn, post_ffn, w_up_b, ffn_cv_w, ffn_bias, w_down_b,
                      i, tiles.ffn_ctx, t_ctx).reshape(b, t_ctx, d)
    return x
```

```python
import functools
import math
from typing import NamedTuple

import jax
import jax.numpy as jnp
from jax import lax
from jax.experimental import pallas as pl
from jax.experimental.pallas import tpu as pltpu

F32 = jnp.float32
BF16 = jnp.bfloat16

EPS = 1e-6
LANE = 128
SUBLANES = 8
HEAD_DIM = 128
GROUP_W = 512
ML_HEADS = 4
CHUNK = 128
SG_GROUPS = 4
CONV_W = 31
CONV_HALO = 16
ATT_Q_HEADS = 4
ATT_KV_HEADS = 2
GRID_W = 64
ROPE_THETA = 10000.0
FFN_HALO = 16
N_GATE = 16
GATE_ROWS = 32

COL_MQ, COL_MK, COL_MV, COL_MO = 0, 512, 1024, 1536
COL_SU, COL_SV = 2048, 2560
COL_CA, COL_CG = 3072, 3584
COL_AQ, COL_AK, COL_AV = 4096, 4608, 4864

VMEM_LIMIT = 52 * 1024 * 1024


def _cparams(sem):
    return pltpu.CompilerParams(dimension_semantics=sem, vmem_limit_bytes=VMEM_LIMIT)


def _layer_row(layer, n):
    return pl.BlockSpec((None, 1, n), lambda *_: (layer, 0, 0))


def _mod_spec(layer, k, d, row=None):
    if row is None:
        return pl.BlockSpec((None, None, 1, d), lambda bb, *_: (layer, bb, 0, k))
    return pl.BlockSpec((None, None, 1, d), lambda *_: (layer, row, 0, k))


def _rms(x, g):
    ms = jnp.mean(x * x, axis=-1, keepdims=True)
    return x * lax.rsqrt(ms + EPS) * g


def _layernorm(x, g, b):
    mu = jnp.mean(x, axis=-1, keepdims=True)
    d = x - mu
    var = jnp.mean(d * d, axis=-1, keepdims=True)
    return d * lax.rsqrt(var + EPS) * g + b


def _sigmoid(x):
    return 1.0 / (1.0 + jnp.exp(-x))


def _gelu_tanh(x):
    c = math.sqrt(2.0 / math.pi)
    half = 0.5 * x
    return half + half * jnp.tanh(x * (c + (c * 0.044715) * (x * x)))


def _log_sigmoid(x):
    return jnp.minimum(x, 0.0) - jnp.log1p(jnp.exp(-jnp.abs(x)))


def _ada_kernel(s_ref, w_ref, b_ref, o_ref):
    s = s_ref[...]
    s = s * _sigmoid(s)
    o_ref[...] = jnp.dot(s.astype(BF16), w_ref[...].astype(BF16), preferred_element_type=F32) + b_ref[...]


def _ada(s_rows, w_ada, b_ada, tn=1024):
    depth, d, n = w_ada.shape
    r = s_rows.shape[0]
    return pl.pallas_call(
        _ada_kernel,
        grid=(depth, n // tn),
        in_specs=[pl.BlockSpec((r, d), lambda l, j: (0, 0)),
                  pl.BlockSpec((None, d, tn), lambda l, j: (l, 0, j)),
                  pl.BlockSpec((None, 1, tn), lambda l, j: (l, 0, j))],
        out_specs=pl.BlockSpec((None, r, tn), lambda l, j: (l, 0, j)),
        out_shape=jax.ShapeDtypeStruct((depth, r, n), F32),
        compiler_params=_cparams(("parallel", "parallel")),
        name="ada_mod",
    )(s_rows, w_ada, b_ada.reshape(depth, 1, n))


def _in_proj_kernel(x_ref, xc_ref, sh_ref, sc_ref, shc_ref, scc_ref, g_ref, w_ref, wg_ref, o_ref, og_ref, h_scr,
                    *, tm, t_ctx, n_lat_tiles, rows_per_step):
    i = pl.program_id(1)
    first_col = pl.program_id(2) == 0
    is_lat = i < n_lat_tiles

    def pre(x, sh, sc):
        return (_rms(x, g_ref[...]) * (1.0 + sc) + sh).astype(BF16)

    def first_step(x_rows, sh, sc, n_rows):
        for r in range(n_rows // rows_per_step):
            sl = slice(r * rows_per_step, (r + 1) * rows_per_step)
            h = pre(x_rows[sl, :], sh[...], sc[...])
            h_scr[sl, :] = h
            o_ref[sl, :] = jnp.dot(h, w_ref[...], preferred_element_type=F32).astype(BF16)
            og_ref[sl, :] = jnp.dot(h, wg_ref[...], preferred_element_type=F32)

    @pl.when(jnp.logical_and(first_col, is_lat))
    def _():
        first_step(x_ref, sh_ref, sc_ref, tm)

    @pl.when(jnp.logical_and(jnp.logical_not(first_col), is_lat))
    def _():
        o_ref[...] = jnp.dot(h_scr[...], w_ref[...], preferred_element_type=F32).astype(BF16)

    @pl.when(jnp.logical_and(first_col, jnp.logical_not(is_lat)))
    def _():
        first_step(xc_ref, shc_ref, scc_ref, t_ctx)
        if t_ctx < tm:
            og_ref[t_ctx:, :] = jnp.zeros((tm - t_ctx, og_ref.shape[1]), F32)

    @pl.when(jnp.logical_and(jnp.logical_not(first_col), jnp.logical_not(is_lat)))
    def _():
        o_ref[0:t_ctx, :] = jnp.dot(h_scr[0:t_ctx, :], w_ref[...], preferred_element_type=F32).astype(BF16)

    if t_ctx < tm:
        @pl.when(jnp.logical_not(is_lat))
        def _():
            o_ref[t_ctx:, :] = jnp.zeros((tm - t_ctx, o_ref.shape[1]), BF16)


def _in_proj(x, xc, mod, g_all, w_all, wg_all, layer, tm, tn=1280):
    b, t_lat, d = x.shape
    t_ctx = xc.shape[1]
    tt = t_lat + t_ctx
    nz = w_all.shape[2]
    n_lat_tiles = t_lat // tm
    assert t_lat % tm == 0 and t_ctx <= tm
    return pl.pallas_call(
        functools.partial(_in_proj_kernel, tm=tm, t_ctx=t_ctx, n_lat_tiles=n_lat_tiles, rows_per_step=min(tm, 256)),
        grid=(b, n_lat_tiles + 1, nz // tn),
        in_specs=[pl.BlockSpec((None, tm, d), lambda bb, i, j: (bb, jnp.minimum(i, n_lat_tiles - 1), 0)),
                  pl.BlockSpec((None, t_ctx, d), lambda bb, i, j: (bb, 0, 0)),
                  _mod_spec(layer, 0, d), _mod_spec(layer, 1, d),
                  _mod_spec(layer, 0, d, row=b), _mod_spec(layer, 1, d, row=b),
                  _layer_row(layer, d),
                  pl.BlockSpec((None, d, tn), lambda bb, i, j: (layer, 0, j)),
                  pl.BlockSpec((None, d, LANE), lambda *_: (layer, 0, 0))],
        out_specs=[pl.BlockSpec((None, tm, tn), lambda bb, i, j: (bb, i, j)),
                   pl.BlockSpec((None, tm, LANE), lambda bb, i, j: (bb, i, 0))],
        out_shape=[jax.ShapeDtypeStruct((b, tt, nz), BF16), jax.ShapeDtypeStruct((b, tt, LANE), F32)],
        scratch_shapes=[pltpu.VMEM((tm, d), BF16)],
        compiler_params=_cparams(("parallel", "parallel", "arbitrary")),
        name="in_proj",
    )(x, xc, mod, mod, mod, mod, g_all, w_all, wg_all)


def _prefix_lanes(x, op, fill):
    lane = lax.broadcasted_iota(jnp.int32, x.shape, 1)
    sh = 1
    while sh < x.shape[1]:
        x = op(x, jnp.where(lane >= sh, pltpu.roll(x, sh, 1), fill))
        sh *= 2
    return x


def _suffix_lanes(x, op, fill):
    n = x.shape[1]
    lane = lax.broadcasted_iota(jnp.int32, x.shape, 1)
    sh = 1
    while sh < n:
        x = op(x, jnp.where(lane < n - sh, pltpu.roll(x, n - sh, 1), fill))
        sh *= 2
    return x


def _mlstm_kernel(q_ref, k_ref, v_ref, o_ref, gt_ref, gb_ref, ng_ref, out_ref,
                  rb_scr, gr_scr, vt_scr, u_scr, cp_scr, st_scr, sa_scr, sc_scr, sm_scr, *, n_lat, n_ctx):
    L = CHUNK
    nc = n_lat + n_ctx
    VT_ROWS = vt_scr.shape[1]
    head = pl.program_id(1)
    row = lax.broadcasted_iota(jnp.int32, (L, L), 0)
    col = lax.broadcasted_iota(jnp.int32, (L, L), 1)
    lower = col <= row
    upper = col >= row
    k_scale = HEAD_DIM ** -0.5
    HALF = GATE_ROWS // 2

    def rows_of(c):
        return pl.ds(pl.multiple_of(c * L, L), L)

    def step_of(c, direction):
        if direction == 0:
            return jnp.where(c < n_lat, c + n_ctx, c - n_lat)
        return nc - 1 - c

    @pl.when(head == 0)
    def _():
        lower_f = lower.astype(F32)
        upper_f = upper.astype(F32)
        n_all = nc * HALF
        rg = lax.broadcasted_iota(jnp.int32, (n_all, L), 0) % HALF
        is_cum = (rg % 8) >= 4
        is_bwd = rg >= 8

        def to_rows(c, carry):
            gr_scr[c, 0:HALF, :] = (gt_ref[rows_of(c), :] + gb_ref[...]).T[0:HALF, :]
            return carry

        lax.fori_loop(0, nc, to_rows, 0, unroll=2)

        G = gr_scr[:, 0:HALF, :].reshape(n_all, L)
        Gf = _log_sigmoid(G)
        Bf = jnp.dot(Gf, upper_f, precision=lax.Precision.HIGHEST, preferred_element_type=F32)
        Bb = jnp.dot(Gf, lower_f, precision=lax.Precision.HIGHEST, preferred_element_type=F32)
        Bsel = jnp.where(is_bwd, Bb, Bf)
        R = G - pltpu.roll(Bsel, n_all - 4, 0)
        CM = jnp.where(is_bwd, _suffix_lanes(R, jnp.maximum, -jnp.inf), _prefix_lanes(R, jnp.maximum, -jnp.inf))
        gr_scr[:, 0:HALF, :] = jnp.where(is_cum, Bsel, G).reshape(nc, HALF, L)
        gr_scr[:, HALF:GATE_ROWS, :] = jnp.where(is_cum, pltpu.roll(CM, 4, 0), R).reshape(nc, HALF, L)

    row_g = lax.broadcasted_iota(jnp.int32, (GATE_ROWS, L), 0)
    lane_r = lax.broadcasted_iota(jnp.int32, (1, LANE), 1)

    def gate_row(x, idx):
        return jnp.sum(jnp.where(row_g == idx, x, 0.0), axis=0, keepdims=True)

    st_scr[...] = jnp.zeros_like(st_scr)
    ones_rows = jnp.ones((VT_ROWS - HEAD_DIM, L), F32)

    def summarize(c, carry):
        vt = jnp.concatenate([v_ref[rows_of(c), :].astype(F32).T, ones_rows], axis=0)
        vt_scr[c] = vt.astype(BF16)
        k = k_ref[rows_of(c), :]
        grv = gr_scr[c]
        for direction in (0, 1):
            ci = direction * 8 + head
            b_row = gate_row(grv, ci + 4)
            i_row = gate_row(grv, ci)
            total = b_row[:, L - 1:L] if direction == 0 else b_row[:, 0:1]
            g_row = total - b_row + i_row
            mg = jnp.max(g_row, axis=1, keepdims=True)
            w_row = jnp.exp(g_row - mg) * k_scale
            step = step_of(c, direction)
            u_scr[direction * nc + step] = jnp.dot((vt * w_row).astype(BF16), k, preferred_element_type=F32)
            st_scr[direction, pl.ds(step, 1), :] = jnp.where(lane_r < 64, total, mg)
            rb_scr[direction * nc + c] = jnp.broadcast_to(gate_row(grv, 16 + ci), (L, L)).T
        return carry

    lax.fori_loop(0, nc, summarize, 0, unroll=4)

    lane8 = lax.broadcasted_iota(jnp.int32, (8, LANE), 1)
    for direction in (0, 1):
        xt = st_scr[direction].T
        valid = lane8 < nc
        T = jnp.where(valid, jnp.broadcast_to(xt[0:1, :], (8, LANE)), 0.0)
        G = jnp.where(valid, jnp.broadcast_to(xt[64:65, :], (8, LANE)), -jnp.inf)
        PT = _prefix_lanes(T, jnp.add, 0.0)
        mvec = PT + jnp.maximum(_prefix_lanes(G - PT, jnp.maximum, -jnp.inf), 0.0)
        mprev = jnp.where(lane8 >= 1, pltpu.roll(mvec, 1, 1), 0.0)
        a = jnp.exp(T + mprev - mvec)
        c2 = jnp.exp(G - mvec)
        sa_scr[direction] = jnp.broadcast_to(a[0:1, :], (LANE, LANE)).T
        sc_scr[direction] = jnp.broadcast_to(c2[0:1, :], (LANE, LANE)).T
        sm_scr[direction] = jnp.broadcast_to(mprev[0:1, :], (LANE, LANE)).T

    def scan(s, carry):
        new = []
        for direction in (0, 1):
            ct = carry[direction]
            idx = direction * nc + s
            cp_scr[idx] = ct.astype(BF16)
            a = sa_scr[direction, pl.ds(s, 1), :]
            c2 = sc_scr[direction, pl.ds(s, 1), :]
            new.append(a * ct + c2 * u_scr[idx])
        return tuple(new)

    zero = jnp.zeros((VT_ROWS, HEAD_DIM), F32)
    lax.fori_loop(0, nc, scan, (zero, zero))

    nt_dims = (((1,), (1,)), ((), ()))

    def emit(c, carry):
        q = q_ref[rows_of(c), :]
        vt = vt_scr[c]
        grv = gr_scr[c]
        qk_t = lax.dot_general(k_ref[rows_of(c), :], q, nt_dims, preferred_element_type=F32) * k_scale
        h_t = jnp.zeros((HEAD_DIM, L), F32)
        for direction in (0, 1):
            ci = direction * 8 + head
            step = step_of(c, direction)
            r_col = rb_scr[direction * nc + c]
            cm_row = gate_row(grv, 16 + ci + 4)
            b_row = gate_row(grv, ci + 4)
            p_t = jnp.exp(jnp.where(upper if direction == 0 else lower, r_col - cm_row, -jnp.inf))
            intra = jnp.dot(vt, (qk_t * p_t).astype(BF16), preferred_element_type=F32)
            inter = lax.dot_general(cp_scr[direction * nc + step], q, nt_dims, preferred_element_type=F32)
            mp = sm_scr[direction, pl.ds(step, 1), :]
            mx = jnp.maximum(mp, cm_row)
            e1 = jnp.exp(mp - mx)
            e2 = jnp.exp(cm_row - mx)
            num = e1 * inter[:HEAD_DIM] + e2 * intra[:HEAD_DIM]
            den = e1 * inter[HEAD_DIM:HEAD_DIM + 1] + e2 * intra[HEAD_DIM:HEAD_DIM + 1]
            h_t = h_t + num * (1.0 / jnp.maximum(jnp.abs(den), jnp.exp(-(b_row + mx))))
        mu = jnp.mean(h_t, axis=0, keepdims=True)
        dev = h_t - mu
        var = jnp.mean(dev * dev, axis=0, keepdims=True)
        hn = (dev * lax.rsqrt(var + EPS)).T * ng_ref[...]
        out_ref[rows_of(c), :] = (hn * _sigmoid(o_ref[rows_of(c), :].astype(F32))).astype(BF16)
        return carry

    lax.fori_loop(0, nc, emit, 0, unroll=8)


def _mlstm(z, zg, gate_b_all, norm_g_all, layer, t_lat, t_ctx):
    b, tt, _ = z.shape
    nc = tt // CHUNK
    assert nc <= 64
    vt_rows = HEAD_DIM + 16
    blk = lambda off: pl.BlockSpec((None, tt, LANE), lambda bb, h: (bb, 0, off // LANE + h))
    return pl.pallas_call(
        functools.partial(_mlstm_kernel, n_lat=t_lat // CHUNK, n_ctx=t_ctx // CHUNK),
        grid=(b, ML_HEADS),
        in_specs=[blk(COL_MQ), blk(COL_MK), blk(COL_MV), blk(COL_MO),
                  pl.BlockSpec((None, tt, LANE), lambda bb, h: (bb, 0, 0)),
                  _layer_row(layer, LANE),
                  pl.BlockSpec((None, 1, LANE), lambda bb, h: (layer, 0, h))],
        out_specs=pl.BlockSpec((None, tt, LANE), lambda bb, h: (bb, 0, h)),
        out_shape=jax.ShapeDtypeStruct((b, tt, GROUP_W), BF16),
        scratch_shapes=[pltpu.VMEM((2 * nc, CHUNK, CHUNK), F32),
                        pltpu.VMEM((nc, GATE_ROWS, LANE), F32),
                        pltpu.VMEM((nc, vt_rows, CHUNK), BF16),
                        pltpu.VMEM((2 * nc, vt_rows, HEAD_DIM), F32),
                        pltpu.VMEM((2 * nc, vt_rows, HEAD_DIM), BF16),
                        pltpu.VMEM((2, LANE, LANE), F32),
                        pltpu.VMEM((2, LANE, LANE), F32),
                        pltpu.VMEM((2, LANE, LANE), F32),
                        pltpu.VMEM((2, LANE, LANE), F32)],
        compiler_params=_cparams(("parallel", "arbitrary")),
        name="mlstm",
    )(z, z, z, z, zg, gate_b_all, norm_g_all)


def _sg_kernel(u_ref, v_ref, lng_ref, lnb_ref, w_ref, bs_ref, o_ref, *, n_chunks):
    def body(c, carry):
        sl = pl.ds(pl.multiple_of(c * CHUNK, CHUNK), CHUNK)
        u = _gelu_tanh(u_ref[sl, :].astype(F32))
        v = _layernorm(_gelu_tanh(v_ref[sl, :].astype(F32)), lng_ref[...], lnb_ref[...]).astype(BF16)
        for g in range(SG_GROUPS):
            gs = slice(g * LANE, (g + 1) * LANE)
            mixed = jnp.dot(w_ref[g], v[:, gs], preferred_element_type=F32) + bs_ref[g]
            o_ref[sl, gs] = (u[:, gs] * mixed).astype(BF16)
        return carry

    lax.fori_loop(0, n_chunks, body, 0, unroll=2)


def _spatial_gating(z, ln_g_all, ln_b_all, w_s_all, b_s_all, layer, n_rows, tr):
    b = z.shape[0]
    blk = lambda off: pl.BlockSpec((None, tr, GROUP_W), lambda bb, i: (bb, i, off // GROUP_W))
    per_layer = lambda *_: (layer, 0, 0, 0)
    return pl.pallas_call(
        functools.partial(_sg_kernel, n_chunks=tr // CHUNK),
        grid=(b, n_rows // tr),
        in_specs=[blk(COL_SU), blk(COL_SV),
                  _layer_row(layer, GROUP_W), _layer_row(layer, GROUP_W),
                  pl.BlockSpec((None, SG_GROUPS, CHUNK, CHUNK), per_layer),
                  pl.BlockSpec((None, SG_GROUPS, CHUNK, LANE), per_layer)],
        out_specs=pl.BlockSpec((None, tr, GROUP_W), lambda bb, i: (bb, i, 0)),
        out_shape=jax.ShapeDtypeStruct((b, n_rows, GROUP_W), BF16),
        compiler_params=_cparams(("parallel", "parallel")),
        name="spatial_gating",
    )(z, z, ln_g_all, ln_b_all, w_s_all, b_s_all)


def _conf_kernel(a_ref, g_ref, ap_ref, gp_ref, an_ref, gn_ref, w_ref, b_ref, lng_ref, lnb_ref, o_ref,
                 y_scr, ysh_scr, cv_scr, *, tc, seq_starts, seq_ends):
    i = pl.program_id(1)
    has_prev = functools.reduce(jnp.logical_and, [i != s for s in seq_starts])
    has_next = functools.reduce(jnp.logical_and, [i != e for e in seq_ends])
    H = CONV_HALO

    def glu(a, g):
        return a[...].astype(F32) * _sigmoid(g[...].astype(F32))

    y_scr[H:H + tc, :] = glu(a_ref, g_ref)
    y_scr[0:H, :] = jnp.where(has_prev, glu(ap_ref, gp_ref), 0.0)
    y_scr[H + tc:, :] = jnp.where(has_next, glu(an_ref, gn_ref), 0.0)
    n_sh = ysh_scr.shape[1]
    for j in range(SUBLANES):
        ysh_scr[j] = y_scr[j:j + n_sh, :]
    rb = 64
    base = H - (CONV_W // 2)
    for cb in range(GROUP_W // LANE):
        cs = slice(cb * LANE, (cb + 1) * LANE)
        for r in range(tc // rb):
            acc = jnp.zeros((rb, LANE), F32)
            for k in range(CONV_W):
                a, j = divmod(base + k, SUBLANES)
                acc = acc + ysh_scr[j, r * rb + SUBLANES * a:r * rb + SUBLANES * a + rb, cs] * w_ref[k:k + 1, cs]
            cv_scr[r * rb:(r + 1) * rb, cs] = acc
    y = _layernorm(cv_scr[...] + b_ref[...], lng_ref[...], lnb_ref[...])
    o_ref[...] = (y * _sigmoid(y)).astype(BF16)


def _conformer(z, w_dw_all, b_dw_all, ln_g_all, ln_b_all, layer, t_lat, n_rows, tc):
    b = z.shape[0]
    hb = tc // CONV_HALO
    n_tiles = n_rows // tc
    n_halo = z.shape[1] // CONV_HALO
    ca, cg = COL_CA // GROUP_W, COL_CG // GROUP_W
    cur = lambda cblk: pl.BlockSpec((None, tc, GROUP_W), lambda bb, i: (bb, i, cblk))
    prev = lambda cblk: pl.BlockSpec((None, CONV_HALO, GROUP_W),
                                     lambda bb, i: (bb, jnp.maximum(i * hb - 1, 0), cblk))
    nxt = lambda cblk: pl.BlockSpec((None, CONV_HALO, GROUP_W),
                                    lambda bb, i: (bb, jnp.minimum((i + 1) * hb, n_halo - 1), cblk))
    seq_starts = (0, t_lat // tc)
    seq_ends = (t_lat // tc - 1, z.shape[1] // tc - 1)
    return pl.pallas_call(
        functools.partial(_conf_kernel, tc=tc, seq_starts=seq_starts, seq_ends=seq_ends),
        grid=(b, n_tiles),
        in_specs=[cur(ca), cur(cg), prev(ca), prev(cg), nxt(ca), nxt(cg),
                  pl.BlockSpec((None, CONV_W, GROUP_W), lambda *_: (layer, 0, 0)), _layer_row(layer, GROUP_W),
                  _layer_row(layer, GROUP_W), _layer_row(layer, GROUP_W)],
        out_specs=pl.BlockSpec((None, tc, GROUP_W), lambda bb, i: (bb, i, 0)),
        out_shape=jax.ShapeDtypeStruct((b, n_rows, GROUP_W), BF16),
        scratch_shapes=[pltpu.VMEM((tc + 2 * CONV_HALO, GROUP_W), F32),
                        pltpu.VMEM((SUBLANES, tc + 2 * CONV_HALO - SUBLANES, GROUP_W), F32),
                        pltpu.VMEM((tc, GROUP_W), F32)],
        compiler_params=_cparams(("parallel", "parallel")),
        name="conformer_conv",
    )(z, z, z, z, z, z, w_dw_all, b_dw_all, ln_g_all, ln_b_all)


def _attn_prep_kernel(q_ref, k_ref, v_ref, cos_ref, sin_ref, qg_ref, kg_ref, qo_ref, ko_ref, vo_ref):
    cosf = cos_ref[...]
    sinf = sin_ref[...]

    def norm_rope(x, g, scale):
        y = _rms(x.astype(F32), g)
        return (y * cosf + pltpu.roll(y, HEAD_DIM // 2, 1) * sinf) * scale

    q_scale = HEAD_DIM ** -0.5 * math.log2(math.e)
    for h in range(ATT_Q_HEADS):
        hs = slice(h * HEAD_DIM, (h + 1) * HEAD_DIM)
        qo_ref[:, hs] = norm_rope(q_ref[:, hs], qg_ref[...], q_scale).astype(BF16)
    ones = jnp.ones((v_ref.shape[0], HEAD_DIM), BF16)
    for h in range(ATT_KV_HEADS):
        hs = slice(h * HEAD_DIM, (h + 1) * HEAD_DIM)
        ko_ref[:, hs] = norm_rope(k_ref[:, hs], kg_ref[...], 1.0).astype(BF16)
        vo_ref[:, 2 * h * HEAD_DIM:(2 * h + 1) * HEAD_DIM] = v_ref[:, hs]
        vo_ref[:, (2 * h + 1) * HEAD_DIM:(2 * h + 2) * HEAD_DIM] = ones


def _attn_prep(z, cosf, sinf, qg_all, kg_all, layer, tr):
    b, tt, _ = z.shape
    qw, kw = ATT_Q_HEADS * HEAD_DIM, ATT_KV_HEADS * HEAD_DIM
    return pl.pallas_call(
        _attn_prep_kernel,
        grid=(b, tt // tr),
        in_specs=[pl.BlockSpec((None, tr, qw), lambda bb, i: (bb, i, COL_AQ // qw)),
                  pl.BlockSpec((None, tr, kw), lambda bb, i: (bb, i, COL_AK // kw)),
                  pl.BlockSpec((None, tr, kw), lambda bb, i: (bb, i, COL_AV // kw)),
                  pl.BlockSpec((tr, HEAD_DIM), lambda bb, i: (i, 0)),
                  pl.BlockSpec((tr, HEAD_DIM), lambda bb, i: (i, 0)),
                  _layer_row(layer, HEAD_DIM), _layer_row(layer, HEAD_DIM)],
        out_specs=[pl.BlockSpec((None, tr, qw), lambda bb, i: (bb, i, 0)),
                   pl.BlockSpec((None, tr, kw), lambda bb, i: (bb, i, 0)),
                   pl.BlockSpec((None, tr, 2 * kw), lambda bb, i: (bb, i, 0))],
        out_shape=[jax.ShapeDtypeStruct((b, tt, qw), BF16),
                   jax.ShapeDtypeStruct((b, tt, kw), BF16),
                   jax.ShapeDtypeStruct((b, tt, 2 * kw), BF16)],
        compiler_params=_cparams(("parallel", "parallel")),
        name="attn_prep",
    )(z, z, z, cosf, sinf, qg_all, kg_all)


def _attn_kernel(q_ref, k_ref, v_ref, o_ref, *, tq, sub):
    k = k_ref[...]
    vaug = v_ref[...]
    for g in range(ATT_Q_HEADS // ATT_KV_HEADS):
        hs = slice(g * HEAD_DIM, (g + 1) * HEAD_DIM)
        for r in range(tq // sub):
            rs = slice(r * sub, (r + 1) * sub)
            s = lax.dot_general(q_ref[rs, hs], k, (((1,), (1,)), ((), ())), preferred_element_type=F32)
            p = jnp.exp2(s - jnp.max(s, axis=-1, keepdims=True)).astype(BF16)
            res = jnp.dot(p, vaug, preferred_element_type=F32)
            o_ref[rs, hs] = (res[:, :HEAD_DIM] / res[:, HEAD_DIM:]).astype(BF16)


def _attention(qr, kr, vaug, q_row0, n_q, k_row0, n_k, tq):
    b = qr.shape[0]
    gw = (ATT_Q_HEADS // ATT_KV_HEADS) * HEAD_DIM
    assert q_row0 % tq == 0 and n_q % tq == 0 and k_row0 % n_k == 0
    qb, kb = q_row0 // tq, k_row0 // n_k
    return pl.pallas_call(
        functools.partial(_attn_kernel, tq=tq, sub=min(tq, 128)),
        grid=(b, ATT_KV_HEADS, n_q // tq),
        in_specs=[pl.BlockSpec((None, tq, gw), lambda bb, h, i: (bb, i + qb, h)),
                  pl.BlockSpec((None, n_k, HEAD_DIM), lambda bb, h, i: (bb, kb, h)),
                  pl.BlockSpec((None, n_k, 2 * HEAD_DIM), lambda bb, h, i: (bb, kb, h))],
        out_specs=pl.BlockSpec((None, tq, gw), lambda bb, h, i: (bb, i, h)),
        out_shape=jax.ShapeDtypeStruct((b, n_q, ATT_Q_HEADS * HEAD_DIM), BF16),
        compiler_params=_cparams(("parallel", "parallel", "arbitrary")),
        name="attention",
    )(qr, kr, vaug)


def _out_proj_kernel(ya_ref, yb_ref, yc_ref, yd_ref, w_ref, x_ref, gate_ref, pg_ref, o_ref):
    acc = jnp.dot(ya_ref[...], w_ref[0 * GROUP_W:1 * GROUP_W, :], preferred_element_type=F32)
    acc += jnp.dot(yb_ref[...], w_ref[1 * GROUP_W:2 * GROUP_W, :], preferred_element_type=F32)
    acc += jnp.dot(yc_ref[...], w_ref[2 * GROUP_W:3 * GROUP_W, :], preferred_element_type=F32)
    acc += jnp.dot(yd_ref[...], w_ref[3 * GROUP_W:4 * GROUP_W, :], preferred_element_type=F32)
    o_ref[...] = x_ref[...] + gate_ref[...] * _rms(acc, pg_ref[...])


def _out_proj(ys, y_row_offs, w_out_all, xs, mod, mod_row, post_g_all, layer, tm):
    b, t, d = xs.shape
    assert all(off % tm == 0 for off in y_row_offs)
    yspecs = [pl.BlockSpec((None, tm, GROUP_W), functools.partial(lambda bb, i, o: (bb, i + o, 0), o=off // tm))
              for off in y_row_offs]
    return pl.pallas_call(
        _out_proj_kernel,
        grid=(b, t // tm),
        in_specs=[*yspecs,
                  pl.BlockSpec((None,) + w_out_all.shape[1:], lambda *_: (layer, 0, 0)),
                  pl.BlockSpec((None, tm, d), lambda bb, i: (bb, i, 0)),
                  _mod_spec(layer, 2, d, mod_row),
                  _layer_row(layer, d)],
        out_specs=pl.BlockSpec((None, tm, d), lambda bb, i: (bb, i, 0)),
        out_shape=jax.ShapeDtypeStruct((b, t, d), F32),
        compiler_params=_cparams(("parallel", "parallel")),
        name="out_proj",
    )(*ys, w_out_all, xs, mod, post_g_all)


def _ffn_kernel(x_ref, xp_ref, xn_ref, sh_ref, sc_ref, gate_ref, pre_g_ref, post_g_ref, wg_ref, wu_ref,
                cw_ref, cb_ref, wd_ref, o_ref, h_scr, acc_scr, *, tm, n_tiles, n_ff, seq_len):
    i = pl.program_id(1)
    j = pl.program_id(2)
    H = FFN_HALO
    n_ext = tm + 2 * H
    n_blk = 4 if tm % 64 == 0 else 1
    rb = tm // n_blk
    ext_bounds = [0] + [H + r * rb for r in range(1, n_blk)] + [n_ext]

    def pre(x):
        return _rms(x, pre_g_ref[...]) * (1.0 + sc_ref[...]) + sh_ref[...]

    def up_first():
        gs, us = [], []
        for r in range(n_blk):
            lo, hi = ext_bounds[r], ext_bounds[r + 1]
            m_lo, m_hi = max(lo, H), min(hi, H + tm)
            pieces = []
            if r == 0:
                pieces.append(jnp.where(i > 0, pre(xp_ref[...]), 0.0))
            pieces.append(pre(x_ref[m_lo - H:m_hi - H, :]))
            if r == n_blk - 1:
                pieces.append(jnp.where(i < n_tiles - 1, pre(xn_ref[...]), 0.0))
            hc = (pieces[0] if len(pieces) == 1 else jnp.concatenate(pieces, axis=0)).astype(BF16)
            h_scr[lo:hi, :] = hc
            gs.append(jnp.dot(hc, wg_ref[...], preferred_element_type=F32))
            us.append(jnp.dot(hc[m_lo - lo:m_hi - lo, :], wu_ref[...], preferred_element_type=F32))
        return jnp.concatenate(gs, axis=0), jnp.concatenate(us, axis=0)

    def step(first, last):
        if first:
            g_ext, u = up_first()
        else:
            g_ext = jnp.dot(h_scr[...], wg_ref[...], preferred_element_type=F32)
            u = jnp.dot(h_scr[H:H + tm, :], wu_ref[...], preferred_element_type=F32)
        g_prev = pltpu.roll(g_ext, 1, 0)[H:H + tm]
        g_next = pltpu.roll(g_ext, n_ext - 1, 0)[H:H + tm]
        if seq_len < tm:
            pos = lax.broadcasted_iota(jnp.int32, (tm, 1), 0) % seq_len
            g_prev = jnp.where(pos == 0, 0.0, g_prev)
            g_next = jnp.where(pos == seq_len - 1, 0.0, g_next)
        conv = g_prev * cw_ref[0:1, :] + g_ext[H:H + tm] * cw_ref[1:2, :] + g_next * cw_ref[2:3, :] + cb_ref[...]
        act = ((conv * _sigmoid(conv)) * u).astype(BF16)
        if not last:
            down = jnp.dot(act, wd_ref[...], preferred_element_type=F32)
            acc_scr[...] = down if first else acc_scr[...] + down
            return
        for r in range(n_blk):
            rs = slice(r * rb, (r + 1) * rb)
            down = jnp.dot(act[rs, :], wd_ref[...], preferred_element_type=F32)
            acc = down if first else acc_scr[rs, :] + down
            o_ref[rs, :] = x_ref[rs, :] + gate_ref[...] * _rms(acc, post_g_ref[...])

    @pl.when(j == 0)
    def _():
        step(True, n_ff == 1)

    if n_ff > 2:
        @pl.when(jnp.logical_and(j > 0, j < n_ff - 1))
        def _():
            step(False, False)

    if n_ff > 1:
        @pl.when(j == n_ff - 1)
        def _():
            step(False, True)


def _ffn(xs, mod, mod_row, pre_g_all, post_g_all, w_up_all, cv_w_all, cv_b_all, w_down_all, layer, tm, seq_len, tf=512):
    b, t, d = xs.shape
    assert seq_len == t or (tm % seq_len == 0 and t % tm == 0)
    d_ff = w_down_all.shape[1]
    n_ff = d_ff // tf
    n_tiles = t // tm
    hb = tm // FFN_HALO
    n_halo = t // FFN_HALO
    return pl.pallas_call(
        functools.partial(_ffn_kernel, tm=tm, n_tiles=n_tiles, n_ff=n_ff, seq_len=seq_len),
        grid=(b, n_tiles, n_ff),
        in_specs=[pl.BlockSpec((None, tm, d), lambda bb, i, j: (bb, i, 0)),
                  pl.BlockSpec((None, FFN_HALO, d), lambda bb, i, j: (bb, jnp.maximum(i * hb - 1, 0), 0)),
                  pl.BlockSpec((None, FFN_HALO, d), lambda bb, i, j: (bb, jnp.minimum((i + 1) * hb, n_halo - 1), 0)),
                  _mod_spec(layer, 3, d, mod_row), _mod_spec(layer, 4, d, mod_row), _mod_spec(layer, 5, d, mod_row),
                  _layer_row(layer, d), _layer_row(layer, d),
                  pl.BlockSpec((None, d, tf), lambda bb, i, j: (layer, 0, j)),
                  pl.BlockSpec((None, d, tf), lambda bb, i, j: (layer, 0, j + n_ff)),
                  pl.BlockSpec((None, cv_w_all.shape[1], tf), lambda bb, i, j: (layer, 0, j)),
                  pl.BlockSpec((None, 1, tf), lambda bb, i, j: (layer, 0, j)),
                  pl.BlockSpec((None, tf, d), lambda bb, i, j: (layer, j, 0))],
        out_specs=pl.BlockSpec((None, tm, d), lambda bb, i, j: (bb, i, 0)),
        out_shape=jax.ShapeDtypeStruct((b, t, d), F32),
        scratch_shapes=[pltpu.VMEM((tm + 2 * FFN_HALO, d), BF16), pltpu.VMEM((tm, d), F32)],
        compiler_params=_cparams(("parallel", "parallel", "arbitrary")),
        name="conv_glu_ffn",
    )(xs, xs, xs, mod, mod, mod, pre_g_all, post_g_all, w_up_all, w_up_all, cv_w_all, cv_b_all, w_down_all)


def _pack_w_in(w_in):
    depth, d, _ = w_in.shape
    main = jnp.concatenate([w_in[..., :4 * GROUP_W], w_in[..., 4 * GROUP_W + N_GATE:]], axis=-1).astype(BF16)
    gates = w_in[..., 4 * GROUP_W:4 * GROUP_W + N_GATE]
    gates = jnp.concatenate([gates, jnp.zeros((depth, d, LANE - N_GATE), w_in.dtype)], axis=-1).astype(BF16)
    return main, gates


class _Tiles(NamedTuple):
    in_proj: int
    rows: int
    ffn_ctx: int
    attn_q: int
    attn_prep: int
    sg_all: int
    sg_lat: int
    conf: int


def _largest_tile(n, cap, step):
    best = step
    for t in range(step, min(n, cap) + 1, step):
        if n % t == 0:
            best = t
    return best


def _choose_tiles(b, t_lat, t_ctx):
    tt = t_lat + t_ctx
    seqs_per_tile = max(1, min(b, 512 // t_ctx))
    ffn_ctx = t_ctx * seqs_per_tile if b % seqs_per_tile == 0 else t_ctx
    return _Tiles(in_proj=_largest_tile(t_lat, 1024, t_ctx), rows=_largest_tile(t_lat, 512, 256), ffn_ctx=ffn_ctx,
                  attn_q=_largest_tile(t_lat, 1024, 256), attn_prep=_largest_tile(tt, 1088, 64),
                  sg_all=_largest_tile(tt, 2176, CHUNK), sg_lat=_largest_tile(t_lat, 2048, CHUNK),
                  conf=_largest_tile(t_ctx, 256, 64))


def _rope_tables(t_lat, t_ctx):
    t = jnp.arange(t_lat)
    row = (t // GRID_W).astype(F32)
    colp = (t % GRID_W).astype(F32)
    axis_dim = HEAD_DIM // 2
    inv = jnp.power(ROPE_THETA, -jnp.arange(0, axis_dim, 2, dtype=F32) / axis_dim)
    ang = jnp.concatenate([row[:, None] * inv, colp[:, None] * inv], axis=-1)
    cos, sin = jnp.cos(ang), jnp.sin(ang)
    cosf = jnp.concatenate([cos, cos], axis=-1)
    sinf = jnp.concatenate([-sin, sin], axis=-1)
    cosf = jnp.concatenate([cosf, jnp.ones((t_ctx, HEAD_DIM), F32)], axis=0)
    sinf = jnp.concatenate([sinf, jnp.zeros((t_ctx, HEAD_DIM), F32)], axis=0)
    return cosf, sinf


def kernel(x, c, ctx, c_ctx, w_ada, b_ada, pre_mix_g, post_mix_g, w_in, ml_gate_b, ml_norm_g, sg_ln_g, sg_ln_b,
           sg_w, sg_b, cv_w, cv_b, cv_ln_g, cv_ln_b, at_qn_g, at_kn_g, w_out, pre_ffn_g, post_ffn_g, w_up,
           ffn_cv_w, ffn_cv_b, w_down):
    b, t_lat, d = x.shape
    t_ctx = ctx.shape[1]
    tt = t_lat + t_ctx
    depth = w_ada.shape[0]
    assert t_lat % t_ctx == 0 and t_ctx % 256 == 0
    tiles = _choose_tiles(b, t_lat, t_ctx)

    n_rows = -(-(b + 1) // 8) * 8
    s_rows = jnp.concatenate([c, c_ctx[None, :], jnp.zeros((n_rows - b - 1, d), F32)], axis=0)
    mod = _ada(s_rows, w_ada, b_ada).reshape(depth, n_rows, 1, 6 * d)
    cosf, sinf = _rope_tables(t_lat, t_ctx)

    rows = lambda a: a.reshape(depth, 1, -1)
    w_main, w_gate = _pack_w_in(w_in)
    w_out_b, w_up_b, w_down_b = w_out.astype(BF16), w_up.astype(BF16), w_down.astype(BF16)
    gate_b = rows(jnp.concatenate([ml_gate_b, jnp.zeros((depth, LANE - N_GATE), F32)], axis=1))
    sg_w_b = sg_w.astype(BF16)
    sg_b_full = jnp.broadcast_to(sg_b[..., None], (depth, SG_GROUPS, CHUNK, LANE))
    pre_mix, post_mix, pre_ffn, post_ffn = rows(pre_mix_g), rows(post_mix_g), rows(pre_ffn_g), rows(post_ffn_g)
    ml_norm, sg_g, sg_bb, cv_bias, cv_g, cv_bb = (rows(ml_norm_g), rows(sg_ln_g), rows(sg_ln_b), rows(cv_b),
                                                  rows(cv_ln_g), rows(cv_ln_b))
    qn_g, kn_g, ffn_bias = rows(at_qn_g), rows(at_kn_g), rows(ffn_cv_b)

    xc = ctx
    for i in range(depth):
        need_ctx = i < depth - 1
        z, zg = _in_proj(x, xc, mod, pre_mix, w_main, w_gate, i, tiles.in_proj)
        ya = _mlstm(z, zg, gate_b, ml_norm, i, t_lat, t_ctx)

        mix_rows = tt if need_ctx else t_lat
        yb = _spatial_gating(z, sg_g, sg_bb, sg_w_b, sg_b_full, i, mix_rows, tiles.sg_all if need_ctx else tiles.sg_lat)
        yc = _conformer(z, cv_w, cv_bias, cv_g, cv_bb, i, t_lat, mix_rows, tiles.conf)

        qr, kr, vaug = _attn_prep(z, cosf, sinf, qn_g, kn_g, i, tiles.attn_prep)
        yd = _attention(qr, kr, vaug, 0, t_lat, 0, tt, tiles.attn_q)

        x = _out_proj((ya, yb, yc, yd), (0, 0, 0, 0), w_out_b, x, mod, None, post_mix, i, tiles.rows)
        x = _ffn(x, mod, None, pre_ffn, post_ffn, w_up_b, ffn_cv_w, ffn_bias, w_down_b, i, tiles.rows, t_lat)
        if need_ctx:
            yd_c = _attention(qr, kr, vaug, t_lat, t_ctx, t_lat, t_ctx, t_ctx)
            xc = _out_proj((ya, yb, yc, yd_c), (t_lat, t_lat, t_lat, 0), w_out_b, xc, mod, b, post_mix, i, t_ctx)
            xc = _ffn(xc.reshape(1, b * t_ctx, d), mod, b, pre_ffn, post_ffn, w_up_b, ffn_cv_w, ffn_bias, w_down_b,
                      i, tiles.ffn_ctx, t_ctx).reshape(b, t_ctx, d)
    return x
```

```python
import functools
import math
from typing import NamedTuple

import jax
import jax.numpy as jnp
from jax import lax
from jax.experimental import pallas as pl
from jax.experimental.pallas import tpu as pltpu

F32 = jnp.float32
BF16 = jnp.bfloat16

EPS = 1e-6
LANE = 128
SUBLANES = 8
HEAD_DIM = 128
GROUP_W = 512
ML_HEADS = 4
CHUNK = 128
SG_GROUPS = 4
CONV_W = 31
CONV_HALO = 16
ATT_Q_HEADS = 4
ATT_KV_HEADS = 2
GRID_W = 64
ROPE_THETA = 10000.0
FFN_HALO = 16
N_GATE = 16
GATE_ROWS = 32

COL_MQ, COL_MK, COL_MV, COL_MO = 0, 512, 1024, 1536
COL_SU, COL_SV = 2048, 2560
COL_CA, COL_CG = 3072, 3584
COL_AQ, COL_AK, COL_AV = 4096, 4608, 4864

VMEM_LIMIT = 52 * 1024 * 1024


def _cparams(sem):
    return pltpu.CompilerParams(dimension_semantics=sem, vmem_limit_bytes=VMEM_LIMIT)


def _layer_row(layer, n):
    return pl.BlockSpec((None, 1, n), lambda *_: (layer, 0, 0))


def _mod_spec(layer, k, d, row=None):
    if row is None:
        return pl.BlockSpec((None, None, 1, d), lambda bb, *_: (layer, bb, 0, k))
    return pl.BlockSpec((None, None, 1, d), lambda *_: (layer, row, 0, k))


def _rms(x, g):
    ms = jnp.mean(x * x, axis=-1, keepdims=True)
    return x * lax.rsqrt(ms + EPS) * g


def _layernorm(x, g, b):
    mu = jnp.mean(x, axis=-1, keepdims=True)
    d = x - mu
    var = jnp.mean(d * d, axis=-1, keepdims=True)
    return d * lax.rsqrt(var + EPS) * g + b


def _sigmoid(x):
    return 1.0 / (1.0 + jnp.exp(-x))


def _gelu_tanh(x):
    c = math.sqrt(2.0 / math.pi)
    return x * (0.5 * (1.0 + jnp.tanh(c * (x + 0.044715 * (x * x * x)))))


def _log_sigmoid(x):
    return jnp.minimum(x, 0.0) - jnp.log1p(jnp.exp(-jnp.abs(x)))


def _ada_kernel(s_ref, w_ref, b_ref, o_ref):
    s = s_ref[...]
    s = s * _sigmoid(s)
    o_ref[...] = jnp.dot(s.astype(BF16), w_ref[...].astype(BF16), preferred_element_type=F32) + b_ref[...]


def _ada(s_rows, w_ada, b_ada, tn=1024):
    depth, d, n = w_ada.shape
    r = s_rows.shape[0]
    return pl.pallas_call(
        _ada_kernel,
        grid=(depth, n // tn),
        in_specs=[pl.BlockSpec((r, d), lambda l, j: (0, 0)),
                  pl.BlockSpec((None, d, tn), lambda l, j: (l, 0, j)),
                  pl.BlockSpec((None, 1, tn), lambda l, j: (l, 0, j))],
        out_specs=pl.BlockSpec((None, r, tn), lambda l, j: (l, 0, j)),
        out_shape=jax.ShapeDtypeStruct((depth, r, n), F32),
        compiler_params=_cparams(("parallel", "parallel")),
        name="ada_mod",
    )(s_rows, w_ada, b_ada.reshape(depth, 1, n))


def _in_proj_kernel(x_ref, xc_ref, sh_ref, sc_ref, shc_ref, scc_ref, g_ref, w_ref, wg_ref, o_ref, og_ref, h_scr,
                    *, tm, t_ctx, n_lat_tiles, rows_per_step):
    i = pl.program_id(1)
    first_col = pl.program_id(2) == 0
    is_lat = i < n_lat_tiles

    def pre(x, sh, sc):
        return (_rms(x, g_ref[...]) * (1.0 + sc) + sh).astype(BF16)

    def first_step(x_rows, sh, sc, n_rows):
        for r in range(n_rows // rows_per_step):
            sl = slice(r * rows_per_step, (r + 1) * rows_per_step)
            h = pre(x_rows[sl, :], sh[...], sc[...])
            h_scr[sl, :] = h
            o_ref[sl, :] = jnp.dot(h, w_ref[...], preferred_element_type=F32).astype(BF16)
            og_ref[sl, :] = jnp.dot(h, wg_ref[...], preferred_element_type=F32)

    @pl.when(jnp.logical_and(first_col, is_lat))
    def _():
        first_step(x_ref, sh_ref, sc_ref, tm)

    @pl.when(jnp.logical_and(jnp.logical_not(first_col), is_lat))
    def _():
        o_ref[...] = jnp.dot(h_scr[...], w_ref[...], preferred_element_type=F32).astype(BF16)

    @pl.when(jnp.logical_and(first_col, jnp.logical_not(is_lat)))
    def _():
        first_step(xc_ref, shc_ref, scc_ref, t_ctx)
        if t_ctx < tm:
            og_ref[t_ctx:, :] = jnp.zeros((tm - t_ctx, og_ref.shape[1]), F32)

    @pl.when(jnp.logical_and(jnp.logical_not(first_col), jnp.logical_not(is_lat)))
    def _():
        o_ref[0:t_ctx, :] = jnp.dot(h_scr[0:t_ctx, :], w_ref[...], preferred_element_type=F32).astype(BF16)

    if t_ctx < tm:
        @pl.when(jnp.logical_not(is_lat))
        def _():
            o_ref[t_ctx:, :] = jnp.zeros((tm - t_ctx, o_ref.shape[1]), BF16)


def _in_proj(x, xc, mod, g_all, w_all, wg_all, layer, tm, tn=1280):
    b, t_lat, d = x.shape
    t_ctx = xc.shape[1]
    tt = t_lat + t_ctx
    nz = w_all.shape[2]
    n_lat_tiles = t_lat // tm
    assert t_lat % tm == 0 and t_ctx <= tm
    return pl.pallas_call(
        functools.partial(_in_proj_kernel, tm=tm, t_ctx=t_ctx, n_lat_tiles=n_lat_tiles, rows_per_step=min(tm, 256)),
        grid=(b, n_lat_tiles + 1, nz // tn),
        in_specs=[pl.BlockSpec((None, tm, d), lambda bb, i, j: (bb, jnp.minimum(i, n_lat_tiles - 1), 0)),
                  pl.BlockSpec((None, t_ctx, d), lambda bb, i, j: (bb, 0, 0)),
                  _mod_spec(layer, 0, d), _mod_spec(layer, 1, d),
                  _mod_spec(layer, 0, d, row=b), _mod_spec(layer, 1, d, row=b),
                  _layer_row(layer, d),
                  pl.BlockSpec((None, d, tn), lambda bb, i, j: (layer, 0, j)),
                  pl.BlockSpec((None, d, LANE), lambda *_: (layer, 0, 0))],
        out_specs=[pl.BlockSpec((None, tm, tn), lambda bb, i, j: (bb, i, j)),
                   pl.BlockSpec((None, tm, LANE), lambda bb, i, j: (bb, i, 0))],
        out_shape=[jax.ShapeDtypeStruct((b, tt, nz), BF16), jax.ShapeDtypeStruct((b, tt, LANE), F32)],
        scratch_shapes=[pltpu.VMEM((tm, d), BF16)],
        compiler_params=_cparams(("parallel", "parallel", "arbitrary")),
        name="in_proj",
    )(x, xc, mod, mod, mod, mod, g_all, w_all, wg_all)


def _prefix_lanes(x, op, fill):
    lane = lax.broadcasted_iota(jnp.int32, x.shape, 1)
    sh = 1
    while sh < x.shape[1]:
        x = op(x, jnp.where(lane >= sh, pltpu.roll(x, sh, 1), fill))
        sh *= 2
    return x


def _suffix_lanes(x, op, fill):
    n = x.shape[1]
    lane = lax.broadcasted_iota(jnp.int32, x.shape, 1)
    sh = 1
    while sh < n:
        x = op(x, jnp.where(lane < n - sh, pltpu.roll(x, n - sh, 1), fill))
        sh *= 2
    return x


def _mlstm_kernel(q_ref, k_ref, v_ref, o_ref, gt_ref, gb_ref, ng_ref, out_ref,
                  rb_scr, gr_scr, vt_scr, u_scr, cp_scr, st_scr, sa_scr, sc_scr, sm_scr, *, n_lat, n_ctx):
    L = CHUNK
    nc = n_lat + n_ctx
    VT_ROWS = vt_scr.shape[1]
    head = pl.program_id(1)
    row = lax.broadcasted_iota(jnp.int32, (L, L), 0)
    col = lax.broadcasted_iota(jnp.int32, (L, L), 1)
    lower = col <= row
    upper = col >= row
    k_scale = HEAD_DIM ** -0.5
    HALF = GATE_ROWS // 2

    def rows_of(c):
        return pl.ds(pl.multiple_of(c * L, L), L)

    def step_of(c, direction):
        if direction == 0:
            return jnp.where(c < n_lat, c + n_ctx, c - n_lat)
        return nc - 1 - c

    @pl.when(head == 0)
    def _():
        lower_f = lower.astype(F32)
        upper_f = upper.astype(F32)
        n_all = nc * HALF
        rg = lax.broadcasted_iota(jnp.int32, (n_all, L), 0) % HALF
        is_cum = (rg % 8) >= 4
        is_bwd = rg >= 8

        def to_rows(c, carry):
            gr_scr[c, 0:HALF, :] = (gt_ref[rows_of(c), :] + gb_ref[...]).T[0:HALF, :]
            return carry

        lax.fori_loop(0, nc, to_rows, 0, unroll=2)

        G = gr_scr[:, 0:HALF, :].reshape(n_all, L)
        Gf = _log_sigmoid(G)
        Bf = jnp.dot(Gf, upper_f, precision=lax.Precision.HIGHEST, preferred_element_type=F32)
        Bb = jnp.dot(Gf, lower_f, precision=lax.Precision.HIGHEST, preferred_element_type=F32)
        Bsel = jnp.where(is_bwd, Bb, Bf)
        R = G - pltpu.roll(Bsel, n_all - 4, 0)
        CM = jnp.where(is_bwd, _suffix_lanes(R, jnp.maximum, -jnp.inf), _prefix_lanes(R, jnp.maximum, -jnp.inf))
        gr_scr[:, 0:HALF, :] = jnp.where(is_cum, Bsel, G).reshape(nc, HALF, L)
        gr_scr[:, HALF:GATE_ROWS, :] = jnp.where(is_cum, pltpu.roll(CM, 4, 0), R).reshape(nc, HALF, L)

    row_g = lax.broadcasted_iota(jnp.int32, (GATE_ROWS, L), 0)
    lane_r = lax.broadcasted_iota(jnp.int32, (1, LANE), 1)

    def gate_row(x, idx):
        return jnp.sum(jnp.where(row_g == idx, x, 0.0), axis=0, keepdims=True)

    st_scr[...] = jnp.zeros_like(st_scr)
    ones_rows = jnp.ones((VT_ROWS - HEAD_DIM, L), F32)

    def summarize(c, carry):
        vt = jnp.concatenate([v_ref[rows_of(c), :].astype(F32).T, ones_rows], axis=0)
        vt_scr[c] = vt.astype(BF16)
        k = k_ref[rows_of(c), :]
        grv = gr_scr[c]
        for direction in (0, 1):
            ci = direction * 8 + head
            b_row = gate_row(grv, ci + 4)
            i_row = gate_row(grv, ci)
            total = b_row[:, L - 1:L] if direction == 0 else b_row[:, 0:1]
            g_row = total - b_row + i_row
            mg = jnp.max(g_row, axis=1, keepdims=True)
            w_row = jnp.exp(g_row - mg) * k_scale
            step = step_of(c, direction)
            u_scr[direction * nc + step] = jnp.dot((vt * w_row).astype(BF16), k, preferred_element_type=F32)
            st_scr[direction, pl.ds(step, 1), :] = jnp.where(lane_r < 64, total, mg)
            rb_scr[direction * nc + c] = jnp.broadcast_to(gate_row(grv, 16 + ci), (L, L)).T
        return carry

    lax.fori_loop(0, nc, summarize, 0, unroll=4)

    lane8 = lax.broadcasted_iota(jnp.int32, (8, LANE), 1)
    for direction in (0, 1):
        xt = st_scr[direction].T
        valid = lane8 < nc
        T = jnp.where(valid, jnp.broadcast_to(xt[0:1, :], (8, LANE)), 0.0)
        G = jnp.where(valid, jnp.broadcast_to(xt[64:65, :], (8, LANE)), -jnp.inf)
        PT = _prefix_lanes(T, jnp.add, 0.0)
        mvec = PT + jnp.maximum(_prefix_lanes(G - PT, jnp.maximum, -jnp.inf), 0.0)
        mprev = jnp.where(lane8 >= 1, pltpu.roll(mvec, 1, 1), 0.0)
        a = jnp.exp(T + mprev - mvec)
        c2 = jnp.exp(G - mvec)
        sa_scr[direction] = jnp.broadcast_to(a[0:1, :], (LANE, LANE)).T
        sc_scr[direction] = jnp.broadcast_to(c2[0:1, :], (LANE, LANE)).T
        sm_scr[direction] = jnp.broadcast_to(mprev[0:1, :], (LANE, LANE)).T

    def scan(s, carry):
        new = []
        for direction in (0, 1):
            ct = carry[direction]
            idx = direction * nc + s
            cp_scr[idx] = ct.astype(BF16)
            a = sa_scr[direction, pl.ds(s, 1), :]
            c2 = sc_scr[direction, pl.ds(s, 1), :]
            new.append(a * ct + c2 * u_scr[idx])
        return tuple(new)

    zero = jnp.zeros((VT_ROWS, HEAD_DIM), F32)
    lax.fori_loop(0, nc, scan, (zero, zero))

    nt_dims = (((1,), (1,)), ((), ()))

    def emit(c, carry):
        q = q_ref[rows_of(c), :]
        vt = vt_scr[c]
        grv = gr_scr[c]
        qk_t = lax.dot_general(k_ref[rows_of(c), :], q, nt_dims, preferred_element_type=F32) * k_scale
        h_t = jnp.zeros((HEAD_DIM, L), F32)
        for direction in (0, 1):
            ci = direction * 8 + head
            step = step_of(c, direction)
            r_col = rb_scr[direction * nc + c]
            cm_row = gate_row(grv, 16 + ci + 4)
            b_row = gate_row(grv, ci + 4)
            p_t = jnp.exp(jnp.where(upper if direction == 0 else lower, r_col - cm_row, -jnp.inf))
            intra = jnp.dot(vt, (qk_t * p_t).astype(BF16), preferred_element_type=F32)
            inter = lax.dot_general(cp_scr[direction * nc + step], q, nt_dims, preferred_element_type=F32)
            mp = sm_scr[direction, pl.ds(step, 1), :]
            mx = jnp.maximum(mp, cm_row)
            e1 = jnp.exp(mp - mx)
            e2 = jnp.exp(cm_row - mx)
            num = e1 * inter[:HEAD_DIM] + e2 * intra[:HEAD_DIM]
            den = e1 * inter[HEAD_DIM:HEAD_DIM + 1] + e2 * intra[HEAD_DIM:HEAD_DIM + 1]
            h_t = h_t + num * (1.0 / jnp.maximum(jnp.abs(den), jnp.exp(-(b_row + mx))))
        mu = jnp.mean(h_t, axis=0, keepdims=True)
        dev = h_t - mu
        var = jnp.mean(dev * dev, axis=0, keepdims=True)
        hn = (dev * lax.rsqrt(var + EPS)).T * ng_ref[...]
        out_ref[rows_of(c), :] = (hn * _sigmoid(o_ref[rows_of(c), :].astype(F32))).astype(BF16)
        return carry

    lax.fori_loop(0, nc, emit, 0, unroll=8)


def _mlstm(z, zg, gate_b_all, norm_g_all, layer, t_lat, t_ctx):
    b, tt, _ = z.shape
    nc = tt // CHUNK
    assert nc <= 64
    vt_rows = HEAD_DIM + 16
    blk = lambda off: pl.BlockSpec((None, tt, LANE), lambda bb, h: (bb, 0, off // LANE + h))
    return pl.pallas_call(
        functools.partial(_mlstm_kernel, n_lat=t_lat // CHUNK, n_ctx=t_ctx // CHUNK),
        grid=(b, ML_HEADS),
        in_specs=[blk(COL_MQ), blk(COL_MK), blk(COL_MV), blk(COL_MO),
                  pl.BlockSpec((None, tt, LANE), lambda bb, h: (bb, 0, 0)),
                  _layer_row(layer, LANE),
                  pl.BlockSpec((None, 1, LANE), lambda bb, h: (layer, 0, h))],
        out_specs=pl.BlockSpec((None, tt, LANE), lambda bb, h: (bb, 0, h)),
        out_shape=jax.ShapeDtypeStruct((b, tt, GROUP_W), BF16),
        scratch_shapes=[pltpu.VMEM((2 * nc, CHUNK, CHUNK), F32),
                        pltpu.VMEM((nc, GATE_ROWS, LANE), F32),
                        pltpu.VMEM((nc, vt_rows, CHUNK), BF16),
                        pltpu.VMEM((2 * nc, vt_rows, HEAD_DIM), F32),
                        pltpu.VMEM((2 * nc, vt_rows, HEAD_DIM), BF16),
                        pltpu.VMEM((2, LANE, LANE), F32),
                        pltpu.VMEM((2, LANE, LANE), F32),
                        pltpu.VMEM((2, LANE, LANE), F32),
                        pltpu.VMEM((2, LANE, LANE), F32)],
        compiler_params=_cparams(("parallel", "arbitrary")),
        name="mlstm",
    )(z, z, z, z, zg, gate_b_all, norm_g_all)


def _sg_kernel(u_ref, v_ref, lng_ref, lnb_ref, w_ref, bs_ref, o_ref, *, n_chunks):
    def body(c, carry):
        sl = pl.ds(pl.multiple_of(c * CHUNK, CHUNK), CHUNK)
        u = _gelu_tanh(u_ref[sl, :].astype(F32))
        v = _layernorm(_gelu_tanh(v_ref[sl, :].astype(F32)), lng_ref[...], lnb_ref[...]).astype(BF16)
        for g in range(SG_GROUPS):
            gs = slice(g * LANE, (g + 1) * LANE)
            mixed = jnp.dot(w_ref[g], v[:, gs], preferred_element_type=F32) + bs_ref[g]
            o_ref[sl, gs] = (u[:, gs] * mixed).astype(BF16)
        return carry

    lax.fori_loop(0, n_chunks, body, 0, unroll=2)


def _spatial_gating(z, ln_g_all, ln_b_all, w_s_all, b_s_all, layer, n_rows, tr):
    b = z.shape[0]
    blk = lambda off: pl.BlockSpec((None, tr, GROUP_W), lambda bb, i: (bb, i, off // GROUP_W))
    per_layer = lambda *_: (layer, 0, 0, 0)
    return pl.pallas_call(
        functools.partial(_sg_kernel, n_chunks=tr // CHUNK),
        grid=(b, n_rows // tr),
        in_specs=[blk(COL_SU), blk(COL_SV),
                  _layer_row(layer, GROUP_W), _layer_row(layer, GROUP_W),
                  pl.BlockSpec((None, SG_GROUPS, CHUNK, CHUNK), per_layer),
                  pl.BlockSpec((None, SG_GROUPS, CHUNK, LANE), per_layer)],
        out_specs=pl.BlockSpec((None, tr, GROUP_W), lambda bb, i: (bb, i, 0)),
        out_shape=jax.ShapeDtypeStruct((b, n_rows, GROUP_W), BF16),
        compiler_params=_cparams(("parallel", "parallel")),
        name="spatial_gating",
    )(z, z, ln_g_all, ln_b_all, w_s_all, b_s_all)


def _conf_kernel(a_ref, g_ref, ap_ref, gp_ref, an_ref, gn_ref, w_ref, b_ref, lng_ref, lnb_ref, o_ref,
                 y_scr, ysh_scr, cv_scr, *, tc, seq_starts, seq_ends):
    i = pl.program_id(1)
    has_prev = functools.reduce(jnp.logical_and, [i != s for s in seq_starts])
    has_next = functools.reduce(jnp.logical_and, [i != e for e in seq_ends])
    H = CONV_HALO

    def glu(a, g):
        return a[...].astype(F32) * _sigmoid(g[...].astype(F32))

    y_scr[H:H + tc, :] = glu(a_ref, g_ref)
    y_scr[0:H, :] = jnp.where(has_prev, glu(ap_ref, gp_ref), 0.0)
    y_scr[H + tc:, :] = jnp.where(has_next, glu(an_ref, gn_ref), 0.0)
    n_sh = ysh_scr.shape[1]
    for j in range(SUBLANES):
        ysh_scr[j] = y_scr[j:j + n_sh, :]
    rb = 64
    base = H - (CONV_W // 2)
    for cb in range(GROUP_W // LANE):
        cs = slice(cb * LANE, (cb + 1) * LANE)
        for r in range(tc // rb):
            acc = jnp.zeros((rb, LANE), F32)
            for k in range(CONV_W):
                a, j = divmod(base + k, SUBLANES)
                acc = acc + ysh_scr[j, r * rb + SUBLANES * a:r * rb + SUBLANES * a + rb, cs] * w_ref[k:k + 1, cs]
            cv_scr[r * rb:(r + 1) * rb, cs] = acc
    y = _layernorm(cv_scr[...] + b_ref[...], lng_ref[...], lnb_ref[...])
    o_ref[...] = (y * _sigmoid(y)).astype(BF16)


def _conformer(z, w_dw_all, b_dw_all, ln_g_all, ln_b_all, layer, t_lat, n_rows, tc):
    b = z.shape[0]
    hb = tc // CONV_HALO
    n_tiles = n_rows // tc
    n_halo = z.shape[1] // CONV_HALO
    ca, cg = COL_CA // GROUP_W, COL_CG // GROUP_W
    cur = lambda cblk: pl.BlockSpec((None, tc, GROUP_W), lambda bb, i: (bb, i, cblk))
    prev = lambda cblk: pl.BlockSpec((None, CONV_HALO, GROUP_W),
                                     lambda bb, i: (bb, jnp.maximum(i * hb - 1, 0), cblk))
    nxt = lambda cblk: pl.BlockSpec((None, CONV_HALO, GROUP_W),
                                    lambda bb, i: (bb, jnp.minimum((i + 1) * hb, n_halo - 1), cblk))
    seq_starts = (0, t_lat // tc)
    seq_ends = (t_lat // tc - 1, z.shape[1] // tc - 1)
    return pl.pallas_call(
        functools.partial(_conf_kernel, tc=tc, seq_starts=seq_starts, seq_ends=seq_ends),
        grid=(b, n_tiles),
        in_specs=[cur(ca), cur(cg), prev(ca), prev(cg), nxt(ca), nxt(cg),
                  pl.BlockSpec((None, CONV_W, GROUP_W), lambda *_: (layer, 0, 0)), _layer_row(layer, GROUP_W),
                  _layer_row(layer, GROUP_W), _layer_row(layer, GROUP_W)],
        out_specs=pl.BlockSpec((None, tc, GROUP_W), lambda bb, i: (bb, i, 0)),
        out_shape=jax.ShapeDtypeStruct((b, n_rows, GROUP_W), BF16),
        scratch_shapes=[pltpu.VMEM((tc + 2 * CONV_HALO, GROUP_W), F32),
                        pltpu.VMEM((SUBLANES, tc + 2 * CONV_HALO - SUBLANES, GROUP_W), F32),
                        pltpu.VMEM((tc, GROUP_W), F32)],
        compiler_params=_cparams(("parallel", "parallel")),
        name="conformer_conv",
    )(z, z, z, z, z, z, w_dw_all, b_dw_all, ln_g_all, ln_b_all)


def _attn_prep_kernel(q_ref, k_ref, v_ref, cos_ref, sin_ref, qg_ref, kg_ref, qo_ref, ko_ref, vo_ref):
    cosf = cos_ref[...]
    sinf = sin_ref[...]

    def norm_rope(x, g, scale):
        y = _rms(x.astype(F32), g)
        return (y * cosf + pltpu.roll(y, HEAD_DIM // 2, 1) * sinf) * scale

    q_scale = HEAD_DIM ** -0.5 * math.log2(math.e)
    for h in range(ATT_Q_HEADS):
        hs = slice(h * HEAD_DIM, (h + 1) * HEAD_DIM)
        qo_ref[:, hs] = norm_rope(q_ref[:, hs], qg_ref[...], q_scale).astype(BF16)
    ones = jnp.ones((v_ref.shape[0], HEAD_DIM), BF16)
    for h in range(ATT_KV_HEADS):
        hs = slice(h * HEAD_DIM, (h + 1) * HEAD_DIM)
        ko_ref[:, hs] = norm_rope(k_ref[:, hs], kg_ref[...], 1.0).astype(BF16)
        vo_ref[:, 2 * h * HEAD_DIM:(2 * h + 1) * HEAD_DIM] = v_ref[:, hs]
        vo_ref[:, (2 * h + 1) * HEAD_DIM:(2 * h + 2) * HEAD_DIM] = ones


def _attn_prep(z, cosf, sinf, qg_all, kg_all, layer, tr):
    b, tt, _ = z.shape
    qw, kw = ATT_Q_HEADS * HEAD_DIM, ATT_KV_HEADS * HEAD_DIM
    return pl.pallas_call(
        _attn_prep_kernel,
        grid=(b, tt // tr),
        in_specs=[pl.BlockSpec((None, tr, qw), lambda bb, i: (bb, i, COL_AQ // qw)),
                  pl.BlockSpec((None, tr, kw), lambda bb, i: (bb, i, COL_AK // kw)),
                  pl.BlockSpec((None, tr, kw), lambda bb, i: (bb, i, COL_AV // kw)),
                  pl.BlockSpec((tr, HEAD_DIM), lambda bb, i: (i, 0)),
                  pl.BlockSpec((tr, HEAD_DIM), lambda bb, i: (i, 0)),
                  _layer_row(layer, HEAD_DIM), _layer_row(layer, HEAD_DIM)],
        out_specs=[pl.BlockSpec((None, tr, qw), lambda bb, i: (bb, i, 0)),
                   pl.BlockSpec((None, tr, kw), lambda bb, i: (bb, i, 0)),
                   pl.BlockSpec((None, tr, 2 * kw), lambda bb, i: (bb, i, 0))],
        out_shape=[jax.ShapeDtypeStruct((b, tt, qw), BF16),
                   jax.ShapeDtypeStruct((b, tt, kw), BF16),
                   jax.ShapeDtypeStruct((b, tt, 2 * kw), BF16)],
        compiler_params=_cparams(("parallel", "parallel")),
        name="attn_prep",
    )(z, z, z, cosf, sinf, qg_all, kg_all)


def _attn_kernel(q_ref, k_ref, v_ref, o_ref, *, tq, sub):
    k = k_ref[...]
    vaug = v_ref[...]
    for g in range(ATT_Q_HEADS // ATT_KV_HEADS):
        hs = slice(g * HEAD_DIM, (g + 1) * HEAD_DIM)
        for r in range(tq // sub):
            rs = slice(r * sub, (r + 1) * sub)
            s = lax.dot_general(q_ref[rs, hs], k, (((1,), (1,)), ((), ())), preferred_element_type=F32)
            p = jnp.exp2(s - jnp.max(s, axis=-1, keepdims=True)).astype(BF16)
            res = jnp.dot(p, vaug, preferred_element_type=F32)
            o_ref[rs, hs] = (res[:, :HEAD_DIM] / res[:, HEAD_DIM:]).astype(BF16)


def _attention(qr, kr, vaug, q_row0, n_q, k_row0, n_k, tq):
    b = qr.shape[0]
    gw = (ATT_Q_HEADS // ATT_KV_HEADS) * HEAD_DIM
    assert q_row0 % tq == 0 and n_q % tq == 0 and k_row0 % n_k == 0
    qb, kb = q_row0 // tq, k_row0 // n_k
    return pl.pallas_call(
        functools.partial(_attn_kernel, tq=tq, sub=min(tq, 128)),
        grid=(b, ATT_KV_HEADS, n_q // tq),
        in_specs=[pl.BlockSpec((None, tq, gw), lambda bb, h, i: (bb, i + qb, h)),
                  pl.BlockSpec((None, n_k, HEAD_DIM), lambda bb, h, i: (bb, kb, h)),
                  pl.BlockSpec((None, n_k, 2 * HEAD_DIM), lambda bb, h, i: (bb, kb, h))],
        out_specs=pl.BlockSpec((None, tq, gw), lambda bb, h, i: (bb, i, h)),
        out_shape=jax.ShapeDtypeStruct((b, n_q, ATT_Q_HEADS * HEAD_DIM), BF16),
        compiler_params=_cparams(("parallel", "parallel", "arbitrary")),
        name="attention",
    )(qr, kr, vaug)


def _out_proj_kernel(ya_ref, yb_ref, yc_ref, yd_ref, w_ref, x_ref, gate_ref, pg_ref, o_ref):
    acc = jnp.dot(ya_ref[...], w_ref[0 * GROUP_W:1 * GROUP_W, :], preferred_element_type=F32)
    acc += jnp.dot(yb_ref[...], w_ref[1 * GROUP_W:2 * GROUP_W, :], preferred_element_type=F32)
    acc += jnp.dot(yc_ref[...], w_ref[2 * GROUP_W:3 * GROUP_W, :], preferred_element_type=F32)
    acc += jnp.dot(yd_ref[...], w_ref[3 * GROUP_W:4 * GROUP_W, :], preferred_element_type=F32)
    o_ref[...] = x_ref[...] + gate_ref[...] * _rms(acc, pg_ref[...])


def _out_proj(ys, y_row_offs, w_out_all, xs, mod, mod_row, post_g_all, layer, tm):
    b, t, d = xs.shape
    assert all(off % tm == 0 for off in y_row_offs)
    yspecs = [pl.BlockSpec((None, tm, GROUP_W), functools.partial(lambda bb, i, o: (bb, i + o, 0), o=off // tm))
              for off in y_row_offs]
    return pl.pallas_call(
        _out_proj_kernel,
        grid=(b, t // tm),
        in_specs=[*yspecs,
                  pl.BlockSpec((None,) + w_out_all.shape[1:], lambda *_: (layer, 0, 0)),
                  pl.BlockSpec((None, tm, d), lambda bb, i: (bb, i, 0)),
                  _mod_spec(layer, 2, d, mod_row),
                  _layer_row(layer, d)],
        out_specs=pl.BlockSpec((None, tm, d), lambda bb, i: (bb, i, 0)),
        out_shape=jax.ShapeDtypeStruct((b, t, d), F32),
        compiler_params=_cparams(("parallel", "parallel")),
        name="out_proj",
    )(*ys, w_out_all, xs, mod, post_g_all)


def _ffn_kernel(x_ref, xp_ref, xn_ref, sh_ref, sc_ref, gate_ref, pre_g_ref, post_g_ref, wg_ref, wu_ref,
                cw_ref, cb_ref, wd_ref, o_ref, h_scr, acc_scr, *, tm, n_tiles, n_ff, seq_len):
    i = pl.program_id(1)
    j = pl.program_id(2)
    H = FFN_HALO
    n_ext = tm + 2 * H
    n_blk = 4 if tm % 64 == 0 else 1
    rb = tm // n_blk
    ext_bounds = [0] + [H + r * rb for r in range(1, n_blk)] + [n_ext]

    def pre(x):
        return _rms(x, pre_g_ref[...]) * (1.0 + sc_ref[...]) + sh_ref[...]

    def up_first():
        gs, us = [], []
        for r in range(n_blk):
            lo, hi = ext_bounds[r], ext_bounds[r + 1]
            m_lo, m_hi = max(lo, H), min(hi, H + tm)
            pieces = []
            if r == 0:
                pieces.append(jnp.where(i > 0, pre(xp_ref[...]), 0.0))
            pieces.append(pre(x_ref[m_lo - H:m_hi - H, :]))
            if r == n_blk - 1:
                pieces.append(jnp.where(i < n_tiles - 1, pre(xn_ref[...]), 0.0))
            hc = (pieces[0] if len(pieces) == 1 else jnp.concatenate(pieces, axis=0)).astype(BF16)
            h_scr[lo:hi, :] = hc
            gs.append(jnp.dot(hc, wg_ref[...], preferred_element_type=F32))
            us.append(jnp.dot(hc[m_lo - lo:m_hi - lo, :], wu_ref[...], preferred_element_type=F32))
        return jnp.concatenate(gs, axis=0), jnp.concatenate(us, axis=0)

    def step(first, last):
        if first:
            g_ext, u = up_first()
        else:
            g_ext = jnp.dot(h_scr[...], wg_ref[...], preferred_element_type=F32)
            u = jnp.dot(h_scr[H:H + tm, :], wu_ref[...], preferred_element_type=F32)
        g_prev = pltpu.roll(g_ext, 1, 0)[H:H + tm]
        g_next = pltpu.roll(g_ext, n_ext - 1, 0)[H:H + tm]
        if seq_len < tm:
            pos = lax.broadcasted_iota(jnp.int32, (tm, 1), 0) % seq_len
            g_prev = jnp.where(pos == 0, 0.0, g_prev)
            g_next = jnp.where(pos == seq_len - 1, 0.0, g_next)
        conv = g_prev * cw_ref[0:1, :] + g_ext[H:H + tm] * cw_ref[1:2, :] + g_next * cw_ref[2:3, :] + cb_ref[...]
        act = ((conv * _sigmoid(conv)) * u).astype(BF16)
        if not last:
            down = jnp.dot(act, wd_ref[...], preferred_element_type=F32)
            acc_scr[...] = down if first else acc_scr[...] + down
            return
        for r in range(n_blk):
            rs = slice(r * rb, (r + 1) * rb)
            down = jnp.dot(act[rs, :], wd_ref[...], preferred_element_type=F32)
            acc = down if first else acc_scr[rs, :] + down
            o_ref[rs, :] = x_ref[rs, :] + gate_ref[...] * _rms(acc, post_g_ref[...])

    @pl.when(j == 0)
    def _():
        step(True, n_ff == 1)

    if n_ff > 2:
        @pl.when(jnp.logical_and(j > 0, j < n_ff - 1))
        def _():
            step(False, False)

    if n_ff > 1:
        @pl.when(j == n_ff - 1)
        def _():
            step(False, True)


def _ffn(xs, mod, mod_row, pre_g_all, post_g_all, w_up_all, cv_w_all, cv_b_all, w_down_all, layer, tm, seq_len, tf=512):
    b, t, d = xs.shape
    assert seq_len == t or (tm % seq_len == 0 and t % tm == 0)
    d_ff = w_down_all.shape[1]
    n_ff = d_ff // tf
    n_tiles = t // tm
    hb = tm // FFN_HALO
    n_halo = t // FFN_HALO
    return pl.pallas_call(
        functools.partial(_ffn_kernel, tm=tm, n_tiles=n_tiles, n_ff=n_ff, seq_len=seq_len),
        grid=(b, n_tiles, n_ff),
        in_specs=[pl.BlockSpec((None, tm, d), lambda bb, i, j: (bb, i, 0)),
                  pl.BlockSpec((None, FFN_HALO, d), lambda bb, i, j: (bb, jnp.maximum(i * hb - 1, 0), 0)),
                  pl.BlockSpec((None, FFN_HALO, d), lambda bb, i, j: (bb, jnp.minimum((i + 1) * hb, n_halo - 1), 0)),
                  _mod_spec(layer, 3, d, mod_row), _mod_spec(layer, 4, d, mod_row), _mod_spec(layer, 5, d, mod_row),
                  _layer_row(layer, d), _layer_row(layer, d),
                  pl.BlockSpec((None, d, tf), lambda bb, i, j: (layer, 0, j)),
                  pl.BlockSpec((None, d, tf), lambda bb, i, j: (layer, 0, j + n_ff)),
                  pl.BlockSpec((None, cv_w_all.shape[1], tf), lambda bb, i, j: (layer, 0, j)),
                  pl.BlockSpec((None, 1, tf), lambda bb, i, j: (layer, 0, j)),
                  pl.BlockSpec((None, tf, d), lambda bb, i, j: (layer, j, 0))],
        out_specs=pl.BlockSpec((None, tm, d), lambda bb, i, j: (bb, i, 0)),
        out_shape=jax.ShapeDtypeStruct((b, t, d), F32),
        scratch_shapes=[pltpu.VMEM((tm + 2 * FFN_HALO, d), BF16), pltpu.VMEM((tm, d), F32)],
        compiler_params=_cparams(("parallel", "parallel", "arbitrary")),
        name="conv_glu_ffn",
    )(xs, xs, xs, mod, mod, mod, pre_g_all, post_g_all, w_up_all, w_up_all, cv_w_all, cv_b_all, w_down_all)


def _pack_w_in(w_in):
    depth, d, _ = w_in.shape
    main = jnp.concatenate([w_in[..., :4 * GROUP_W], w_in[..., 4 * GROUP_W + N_GATE:]], axis=-1).astype(BF16)
    gates = w_in[..., 4 * GROUP_W:4 * GROUP_W + N_GATE]
    gates = jnp.concatenate([gates, jnp.zeros((depth, d, LANE - N_GATE), w_in.dtype)], axis=-1).astype(BF16)
    return main, gates


class _Tiles(NamedTuple):
    in_proj: int
    rows: int
    ffn_ctx: int
    attn_q: int
    attn_prep: int
    sg_all: int
    sg_lat: int
    conf: int


def _largest_tile(n, cap, step):
    best = step
    for t in range(step, min(n, cap) + 1, step):
        if n % t == 0:
            best = t
    return best


def _choose_tiles(b, t_lat, t_ctx):
    tt = t_lat + t_ctx
    seqs_per_tile = max(1, min(b, 512 // t_ctx))
    ffn_ctx = t_ctx * seqs_per_tile if b % seqs_per_tile == 0 else t_ctx
    return _Tiles(in_proj=_largest_tile(t_lat, 1024, t_ctx), rows=_largest_tile(t_lat, 512, 256), ffn_ctx=ffn_ctx,
                  attn_q=_largest_tile(t_lat, 1024, 256), attn_prep=_largest_tile(tt, 1088, 64),
                  sg_all=_largest_tile(tt, 2176, CHUNK), sg_lat=_largest_tile(t_lat, 2048, CHUNK),
                  conf=_largest_tile(t_ctx, 256, 64))


def _rope_tables(t_lat, t_ctx):
    t = jnp.arange(t_lat)
    row = (t // GRID_W).astype(F32)
    colp = (t % GRID_W).astype(F32)
    axis_dim = HEAD_DIM // 2
    inv = jnp.power(ROPE_THETA, -jnp.arange(0, axis_dim, 2, dtype=F32) / axis_dim)
    ang = jnp.concatenate([row[:, None] * inv, colp[:, None] * inv], axis=-1)
    cos, sin = jnp.cos(ang), jnp.sin(ang)
    cosf = jnp.concatenate([cos, cos], axis=-1)
    sinf = jnp.concatenate([-sin, sin], axis=-1)
    cosf = jnp.concatenate([cosf, jnp.ones((t_ctx, HEAD_DIM), F32)], axis=0)
    sinf = jnp.concatenate([sinf, jnp.zeros((t_ctx, HEAD_DIM), F32)], axis=0)
    return cosf, sinf


def kernel(x, c, ctx, c_ctx, w_ada, b_ada, pre_mix_g, post_mix_g, w_in, ml_gate_b, ml_norm_g, sg_ln_g, sg_ln_b,
           sg_w, sg_b, cv_w, cv_b, cv_ln_g, cv_ln_b, at_qn_g, at_kn_g, w_out, pre_ffn_g, post_ffn_g, w_up,
           ffn_cv_w, ffn_cv_b, w_down):
    b, t_lat, d = x.shape
    t_ctx = ctx.shape[1]
    tt = t_lat + t_ctx
    depth = w_ada.shape[0]
    assert t_lat % t_ctx == 0 and t_ctx % 256 == 0
    tiles = _choose_tiles(b, t_lat, t_ctx)

    n_rows = -(-(b + 1) // 8) * 8
    s_rows = jnp.concatenate([c, c_ctx[None, :], jnp.zeros((n_rows - b - 1, d), F32)], axis=0)
    mod = _ada(s_rows, w_ada, b_ada).reshape(depth, n_rows, 1, 6 * d)
    cosf, sinf = _rope_tables(t_lat, t_ctx)

    rows = lambda a: a.reshape(depth, 1, -1)
    w_main, w_gate = _pack_w_in(w_in)
    w_out_b, w_up_b, w_down_b = w_out.astype(BF16), w_up.astype(BF16), w_down.astype(BF16)
    gate_b = rows(jnp.concatenate([ml_gate_b, jnp.zeros((depth, LANE - N_GATE), F32)], axis=1))
    sg_w_b = sg_w.astype(BF16)
    sg_b_full = jnp.broadcast_to(sg_b[..., None], (depth, SG_GROUPS, CHUNK, LANE))
    pre_mix, post_mix, pre_ffn, post_ffn = rows(pre_mix_g), rows(post_mix_g), rows(pre_ffn_g), rows(post_ffn_g)
    ml_norm, sg_g, sg_bb, cv_bias, cv_g, cv_bb = (rows(ml_norm_g), rows(sg_ln_g), rows(sg_ln_b), rows(cv_b),
                                                  rows(cv_ln_g), rows(cv_ln_b))
    qn_g, kn_g, ffn_bias = rows(at_qn_g), rows(at_kn_g), rows(ffn_cv_b)

    xc = ctx
    for i in range(depth):
        need_ctx = i < depth - 1
        z, zg = _in_proj(x, xc, mod, pre_mix, w_main, w_gate, i, tiles.in_proj)
        ya = _mlstm(z, zg, gate_b, ml_norm, i, t_lat, t_ctx)

        mix_rows = tt if need_ctx else t_lat
        yb = _spatial_gating(z, sg_g, sg_bb, sg_w_b, sg_b_full, i, mix_rows, tiles.sg_all if need_ctx else tiles.sg_lat)
        yc = _conformer(z, cv_w, cv_bias, cv_g, cv_bb, i, t_lat, mix_rows, tiles.conf if need_ctx else 2 * tiles.conf)

        qr, kr, vaug = _attn_prep(z, cosf, sinf, qn_g, kn_g, i, tiles.attn_prep)
        yd = _attention(qr, kr, vaug, 0, t_lat, 0, tt, tiles.attn_q)

        x = _out_proj((ya, yb, yc, yd), (0, 0, 0, 0), w_out_b, x, mod, None, post_mix, i, tiles.rows)
        x = _ffn(x, mod, None, pre_ffn, post_ffn, w_up_b, ffn_cv_w, ffn_bias, w_down_b, i, tiles.rows, t_lat)
        if need_ctx:
            yd_c = _attention(qr, kr, vaug, t_lat, t_ctx, t_lat, t_ctx, t_ctx)
            xc = _out_proj((ya, yb, yc, yd_c), (t_lat, t_lat, t_lat, 0), w_out_b, xc, mod, b, post_mix, i, t_ctx)
            xc = _ffn(xc.reshape(1, b * t_ctx, d), mod, b, pre_ffn, post_ffn, w_up_b, ffn_cv_w, ffn_bias, w_down_b,
                      i, tiles.ffn_ctx, t_ctx).reshape(b, t_ctx, d)
    return x
```
